```python
import math
import jax, jax.numpy as jnp
from jax import lax
import numpy as np

D_MODEL = 2048
BATCH = 4
SEQ = 4096
DEPTH = 1
DEC_BATCH = 32
DEC_SEQ = 1
PAST_LEN = 16384
PAGE_SIZE = 128

EPS = 1e-6
GDN_HEADS = 8
GDN_DK = 128
GDN_DV = 128
GDN_QK = GDN_HEADS * GDN_DK
GDN_VW = GDN_HEADS * GDN_DV
GDN_CONV_CH = 2 * GDN_QK + GDN_VW
CONV_W = 4
CHUNK = 64
SWA_GROUPS = ((128, 1), (512, 4), (2048, 16))
N_SWA = 3
SWA_HEADS = 4
SWA_DH = 128
SWA_W = SWA_HEADS * SWA_DH
SWA_QKV = N_SWA * SWA_W
ROT_DIM = SWA_DH // 4
ROPE_THETA = 500000.0
Q_BLOCK = 128
PEER_HEADS = 8
PEER_NKEYS = 128
PEER_EXPERTS = PEER_NKEYS * PEER_NKEYS
PEER_DKEY = 256
PEER_TOPK = 16
PEER_BLOCK = 128
D_IN = 2 * GDN_QK + 2 * GDN_VW + 2 * GDN_HEADS + 3 * SWA_QKV + 2 * D_MODEL

kernel_name = 'hybrid_gdn_dilated_swa_peer_step'


def rmsnorm(x, w):
    xf = x.astype(jnp.float32)
    y = xf * lax.rsqrt(jnp.mean(xf * xf, axis=-1, keepdims=True) + EPS)
    return (y * w.astype(jnp.float32)).astype(x.dtype)


def l2norm(x):
    return x * lax.rsqrt(jnp.sum(x * x, axis=-1, keepdims=True) + EPS)


def split_cols(h):
    sizes = (GDN_QK, GDN_QK, GDN_VW, GDN_VW, GDN_HEADS, GDN_HEADS,
             SWA_QKV, SWA_QKV, SWA_QKV, D_MODEL, D_MODEL)
    cuts = [int(c) for c in np.cumsum(sizes)[:-1]]
    return jnp.split(h, cuts, axis=-1)


def rope(x, pos):
    half = ROT_DIM // 2
    inv = ROPE_THETA ** (-jnp.arange(half, dtype=jnp.float32) * 2.0 / ROT_DIM)
    ang = pos.astype(jnp.float32)[:, None] * inv[None, :]
    ang = ang.reshape((1, ang.shape[0]) + (1,) * (x.ndim - 3) + (half,))
    cos, sin = jnp.cos(ang), jnp.sin(ang)
    xr = x[..., :ROT_DIM].astype(jnp.float32)
    x1, x2 = xr[..., :half], xr[..., half:]
    rot = jnp.concatenate([x1 * cos - x2 * sin, x2 * cos + x1 * sin], axis=-1).astype(x.dtype)
    return jnp.concatenate([rot, x[..., ROT_DIM:]], axis=-1)


def causal_conv(x, buf, w):
    xp = jnp.concatenate([buf.astype(x.dtype), x], axis=1)
    T = x.shape[1]
    y = w[0] * xp[:, 0:T]
    for j in range(1, CONV_W):
        y = y + w[j] * xp[:, j:j + T]
    return jax.nn.silu(y), xp[:, T:]


def gated_delta_chunked(q, k, v, g, beta, S0):
    Bn, T, H, DK = q.shape
    DV = v.shape[-1]
    Tp = -(-T // CHUNK) * CHUNK
    N = Tp // CHUNK

    def to_chunks(t):
        t = jnp.pad(t, [(0, 0), (0, Tp - T)] + [(0, 0)] * (t.ndim - 2))
        t = t.reshape((Bn, N, CHUNK) + t.shape[2:])
        return t.transpose((1, 0, 3, 2) + tuple(range(4, t.ndim)))

    q, k, v, g, beta = [to_chunks(t) for t in (q, k, v, g, beta)]
    gc = jnp.cumsum(g, axis=-1)
    idx = jnp.arange(CHUNK)
    causal = idx[:, None] >= idx[None, :]
    strict = idx[:, None] > idx[None, :]
    decay = jnp.exp(jnp.where(causal, gc[..., :, None] - gc[..., None, :], -jnp.inf))
    kb = k * beta[..., None]
    L = jnp.where(strict, jnp.einsum('nbhid,nbhjd->nbhij', kb, k) * decay, 0.0)
    eye = jnp.eye(CHUNK, dtype=jnp.float32)
    Tinv = lax.linalg.triangular_solve(eye + L, jnp.broadcast_to(eye, L.shape),
                                       left_side=True, lower=True, unit_diagonal=True)
    value = jnp.einsum('nbhij,nbhjd->nbhid', Tinv, v * beta[..., None])
    kcd = jnp.einsum('nbhij,nbhjd->nbhid', Tinv, kb * jnp.exp(gc)[..., None])
    attn = jnp.where(causal, jnp.einsum('nbhid,nbhjd->nbhij', q, k) * decay, 0.0)
    qg = q * jnp.exp(gc)[..., None]
    glast = gc[..., -1]
    kdec = k * jnp.exp(glast[..., None] - gc)[..., None]

    def step(S, xs):
        value_c, kcd_c, attn_c, qg_c, kdec_c, gl_c = xs
        v_new = value_c - jnp.einsum('bhik,bhkv->bhiv', kcd_c, S)
        o = jnp.einsum('bhik,bhkv->bhiv', qg_c, S) + jnp.einsum('bhij,bhjv->bhiv', attn_c, v_new)
        S = S * jnp.exp(gl_c)[..., None, None] + jnp.einsum('bhik,bhiv->bhkv', kdec_c, v_new)
        return S, o

    S, o = lax.scan(step, S0, (value, kcd, attn, qg, kdec, glast))
    o = o.transpose(1, 0, 3, 2, 4).reshape(Bn, Tp, H, DV)[:, :T]
    return o, S


def gdn_branch(q, k, v, z, a, b, conv_buf, S0, conv_w, a_log, dt_bias, gdn_norm):
    Bn, T, _ = q.shape
    qkv, new_buf = causal_conv(jnp.concatenate([q, k, v], axis=-1), conv_buf, conv_w)
    qkv = qkv.astype(jnp.float32)
    qh = l2norm(qkv[..., :GDN_QK].reshape(Bn, T, GDN_HEADS, GDN_DK)) * GDN_DK ** -0.5
    kh = l2norm(qkv[..., GDN_QK:2 * GDN_QK].reshape(Bn, T, GDN_HEADS, GDN_DK))
    vh = qkv[..., 2 * GDN_QK:].reshape(Bn, T, GDN_HEADS, GDN_DV)
    beta = jax.nn.sigmoid(b.astype(jnp.float32))
    g = -jnp.exp(a_log.astype(jnp.float32)) * jax.nn.softplus(a.astype(jnp.float32) + dt_bias.astype(jnp.float32))
    o, S = gated_delta_chunked(qh, kh, vh, g, beta, S0.astype(jnp.float32))
    o = rmsnorm(o, gdn_norm) * jax.nn.silu(z.astype(jnp.float32).reshape(Bn, T, GDN_HEADS, GDN_DV))
    return o.reshape(Bn, T, GDN_VW).astype(z.dtype), S, new_buf


def dilated_attend(q, k, v, q_idx, win, dil):
    n = win // dil + 1
    kidx = q_idx[:, None] - dil * jnp.arange(n)[None, :]
    valid = kidx >= 0
    kidx = jnp.maximum(kidx, 0)
    kg = jnp.take(k, kidx, axis=1)
    vg = jnp.take(v, kidx, axis=1)
    logits = jnp.einsum('bqhd,bqnhd->bqhn', q.astype(jnp.float32), kg.astype(jnp.float32)) * SWA_DH ** -0.5
    logits = jnp.where(valid[None, :, None, :], logits, -jnp.inf)
    m = jnp.max(logits, axis=-1, keepdims=True)
    p = jnp.exp(logits - m)
    s = jnp.sum(p, axis=-1)
    o = jnp.einsum('bqhn,bqnhd->bqhd', p, vg.astype(jnp.float32)) / s[..., None]
    return o, m[..., 0] + jnp.log(s)


def combine_groups(parts):
    os = jnp.stack([p[0] for p in parts])
    lse = jnp.stack([p[1] for p in parts])
    w = jax.nn.softmax(lse, axis=0)
    return jnp.sum(w[..., None] * os, axis=0)


def swa_prompt(q, k, v):
    Bn, T = q.shape[:2]
    nb = T // Q_BLOCK
    ks = [k[:, :, gi] for gi in range(N_SWA)]
    vs = [v[:, :, gi] for gi in range(N_SWA)]
    qb = q.reshape(Bn, nb, Q_BLOCK, N_SWA, SWA_HEADS, SWA_DH).transpose(1, 0, 2, 3, 4, 5)

    def block(args):
        bi, qblk = args
        q_idx = bi * Q_BLOCK + jnp.arange(Q_BLOCK)
        parts = [dilated_attend(qblk[:, :, gi], ks[gi], vs[gi], q_idx, win, dil)
                 for gi, (win, dil) in enumerate(SWA_GROUPS)]
        return combine_groups(parts)

    o = lax.map(block, (jnp.arange(nb), qb))
    o = o.transpose(1, 0, 2, 3, 4).reshape(Bn, T, SWA_HEADS, SWA_DH)
    new = [jnp.stack([ks[gi], vs[gi]], axis=2)[:, T - min(win, T):] for gi, (win, _) in enumerate(SWA_GROUPS)]
    return o, new


def swa_sample(q, k, v, caches):
    T = q.shape[1]
    parts, new = [], []
    for gi, (win, dil) in enumerate(SWA_GROUPS):
        cache = caches[gi]
        Wc = cache.shape[1]
        kcat = jnp.concatenate([cache[:, :, 0], k[:, :, gi].astype(cache.dtype)], axis=1)
        vcat = jnp.concatenate([cache[:, :, 1], v[:, :, gi].astype(cache.dtype)], axis=1)
        parts.append(dilated_attend(q[:, :, gi], kcat, vcat, Wc + jnp.arange(T), win, dil))
        new.append(jnp.stack([kcat, vcat], axis=2)[:, kcat.shape[1] - Wc:])
    return combine_groups(parts), new


def peer_ffn(xn, w_query, sub_keys, expert_down, expert_up):
    shape = xn.shape
    xt = xn.reshape(-1, D_MODEL)
    n = xt.shape[0]
    npad = -(-n // PEER_BLOCK) * PEER_BLOCK
    blocks = jnp.pad(xt, ((0, npad - n), (0, 0))).reshape(-1, PEER_BLOCK, D_MODEL)

    def one(xb):
        qh = (xb @ w_query).reshape(PEER_BLOCK, PEER_HEADS, 2, PEER_DKEY // 2).astype(jnp.float32)
        s = jnp.einsum('thpd,hpkd->thpk', qh, sub_keys.astype(jnp.float32))
        v1, i1 = lax.top_k(s[:, :, 0], PEER_TOPK)
        v2, i2 = lax.top_k(s[:, :, 1], PEER_TOPK)
        cand = (v1[..., :, None] + v2[..., None, :]).reshape(PEER_BLOCK, PEER_HEADS, PEER_TOPK * PEER_TOPK)
        sc, ci = lax.top_k(cand, PEER_TOPK)
        e = (jnp.take_along_axis(i1, ci // PEER_TOPK, axis=-1) * PEER_NKEYS
             + jnp.take_along_axis(i2, ci % PEER_TOPK, axis=-1))
        gates = jax.nn.softmax(sc, axis=-1)
        u = expert_down[e]
        act = jax.nn.gelu(jnp.einsum('td,thkd->thk', xb, u).astype(jnp.float32), approximate=False)
        return jnp.einsum('thk,thkd->td', (gates * act).astype(xb.dtype), expert_up[e])

    y = lax.map(one, blocks).reshape(npad, D_MODEL)[:n]
    return y.reshape(shape)


def decoder_layer(x, pos, conv_buf, S0, kv_caches, norm_mix, w_in, conv_w, a_log, dt_bias, gdn_norm,
                  w_br_a, w_br_b, w_out, norm_ffn, w_query, sub_keys, expert_down, expert_up):
    Bn, T, _ = x.shape
    xn = rmsnorm(x, norm_mix)
    gq, gk, gv, gz, ga, gb, sq, sk, sv, gate_a, gate_b = split_cols(xn @ w_in)
    o_a, S_new, conv_new = gdn_branch(gq, gk, gv, gz, ga, gb, conv_buf, S0, conv_w, a_log, dt_bias, gdn_norm)
    shp = (Bn, T, N_SWA, SWA_HEADS, SWA_DH)
    sq = rope(sq.reshape(shp), pos)
    sk = rope(sk.reshape(shp), pos)
    sv = sv.reshape(shp)
    if kv_caches is None:
        o_b, kv_new = swa_prompt(sq, sk, sv)
    else:
        o_b, kv_new = swa_sample(sq, sk, sv, kv_caches)
    o_b = o_b.astype(x.dtype).reshape(Bn, T, SWA_W)
    merged = jax.nn.sigmoid(gate_a) * (o_a @ w_br_a) + jax.nn.sigmoid(gate_b) * (o_b @ w_br_b)
    x = x + merged @ w_out
    x = x + peer_ffn(rmsnorm(x, norm_ffn), w_query, sub_keys, expert_down, expert_up)
    return x, S_new, conv_new, kv_new


def setup_inputs(seed: int = 0) -> dict:
    key = jax.random.key(seed)
    ks = jax.random.split(key, 22)
    f32 = jnp.float32

    def nrm(k, shape, scale):
        return jax.random.normal(k, shape, f32) * scale

    kv_shapes = [(DEC_BATCH, min(win, PAST_LEN), 2, SWA_HEADS, SWA_DH) for win, _ in SWA_GROUPS]
    dt = jnp.exp(jax.random.uniform(ks[9], (GDN_HEADS,), f32, math.log(1e-3), math.log(1e-1)))
    return {
        'x_prompt': nrm(ks[0], (BATCH, SEQ, D_MODEL), 1.0),
        'x_sample': nrm(ks[1], (DEC_BATCH, DEC_SEQ, D_MODEL), 1.0),
        'state_gdn': nrm(ks[2], (DEC_BATCH, GDN_HEADS, GDN_DK, GDN_DV), 0.2),
        'state_conv': nrm(ks[3], (DEC_BATCH, CONV_W - 1, GDN_CONV_CH), 1.0),
        'cache_kv_w128': nrm(ks[4], kv_shapes[0], 1.0),
        'cache_kv_w512': nrm(ks[5], kv_shapes[1], 1.0),
        'cache_kv_w2048': nrm(ks[6], kv_shapes[2], 1.0),
        'norm_mix': 1.0 + nrm(ks[7], (D_MODEL,), 0.1),
        'w_in': nrm(ks[8], (D_MODEL, D_IN), D_MODEL ** -0.5),
        'conv_w': nrm(ks[10], (CONV_W, GDN_CONV_CH), 0.5),
        'a_log': jnp.log(jax.random.uniform(ks[11], (GDN_HEADS,), f32, 1.0, 16.0)),
        'dt_bias': dt + jnp.log(-jnp.expm1(-dt)),
        'gdn_norm': 1.0 + nrm(ks[12], (GDN_DV,), 0.1),
        'w_br_a': nrm(ks[13], (GDN_VW, D_MODEL), GDN_VW ** -0.5),
        'w_br_b': nrm(ks[14], (SWA_W, D_MODEL), SWA_W ** -0.5),
        'w_out': nrm(ks[15], (D_MODEL, D_MODEL), D_MODEL ** -0.5),
        'norm_ffn': 1.0 + nrm(ks[16], (D_MODEL,), 0.1),
        'w_query': nrm(ks[17], (D_MODEL, PEER_HEADS * PEER_DKEY), D_MODEL ** -0.5),
        'sub_keys': nrm(ks[18], (PEER_HEADS, 2, PEER_NKEYS, PEER_DKEY // 2), (PEER_DKEY // 2) ** -0.5),
        'expert_down': nrm(ks[19], (PEER_EXPERTS, D_MODEL), D_MODEL ** -0.5),
        'expert_up': nrm(ks[20], (PEER_EXPERTS, D_MODEL), 0.5),
        'norm_final': 1.0 + nrm(ks[21], (D_MODEL,), 0.1),
    }


def reference(x_prompt, x_sample, state_gdn, state_conv, cache_kv_w128, cache_kv_w512, cache_kv_w2048,
              norm_mix, w_in, conv_w, a_log, dt_bias, gdn_norm, w_br_a, w_br_b, w_out, norm_ffn,
              w_query, sub_keys, expert_down, expert_up, norm_final):
    weights = (norm_mix, w_in, conv_w, a_log, dt_bias, gdn_norm, w_br_a, w_br_b, w_out, norm_ffn,
               w_query, sub_keys, expert_down, expert_up)
    Bp, Tp_len = x_prompt.shape[0], x_prompt.shape[1]
    yp = x_prompt
    for _ in range(DEPTH):
        yp, gdn_p, conv_p, kv_p = decoder_layer(
            yp, jnp.arange(Tp_len, dtype=jnp.int32),
            jnp.zeros((Bp, CONV_W - 1, GDN_CONV_CH), x_prompt.dtype),
            jnp.zeros((Bp, GDN_HEADS, GDN_DK, GDN_DV), jnp.float32),
            None, *weights)
    y_prompt = rmsnorm(yp, norm_final)
    ys = x_sample
    for _ in range(DEPTH):
        ys, gdn_s, conv_s, kv_s = decoder_layer(
            ys, PAST_LEN + jnp.arange(x_sample.shape[1], dtype=jnp.int32),
            state_conv, state_gdn, (cache_kv_w128, cache_kv_w512, cache_kv_w2048), *weights)
    y_sample = rmsnorm(ys, norm_final)
    gdn_p = gdn_p.astype(x_prompt.dtype)
    gdn_s = gdn_s.astype(state_gdn.dtype)
    return (y_prompt, y_sample, gdn_p, conv_p, kv_p[0], kv_p[1], kv_p[2], gdn_s, conv_s, kv_s[0], kv_s[1], kv_s[2])
```

```python
import functools
import math

import jax
import jax.numpy as jnp
import numpy as np
from jax import lax
from jax.experimental import pallas as pl
from jax.experimental.pallas import tpu as pltpu

F32 = jnp.float32
BF16 = jnp.bfloat16
HIGHEST = lax.Precision.HIGHEST

LANE = 128
SUBLANE = 8
VMEM_LIMIT = 56 * 1024 * 1024

D_MODEL = 2048
PAST_LEN = 16384
EPS = 1e-6
GDN_HEADS = 8
GDN_DK = 128
GDN_DV = 128
GDN_QK = GDN_HEADS * GDN_DK
GDN_VW = GDN_HEADS * GDN_DV
GDN_CONV_CH = 2 * GDN_QK + GDN_VW
CONV_W = 4
CHUNK = 64
SWA_GROUPS = ((128, 1), (512, 4), (2048, 16))
N_SWA = 3
SWA_HEADS = 4
SWA_DH = 128
SWA_W = SWA_HEADS * SWA_DH
SWA_QKV = N_SWA * SWA_W
ROT_DIM = SWA_DH // 4
ROPE_THETA = 500000.0
PEER_HEADS = 8
PEER_NKEYS = 128
PEER_DKEY = 256
PEER_TOPK = 16
PEER_SEL = PEER_HEADS * PEER_TOPK

COL_GA = 0
COL_GB = D_MODEL
COL_Q = 2 * D_MODEL
COL_K = COL_Q + GDN_QK
COL_V = COL_K + GDN_QK
COL_Z = COL_V + GDN_VW
COL_SQ = COL_Z + GDN_VW
COL_SK = COL_SQ + SWA_QKV
COL_SV = COL_SK + SWA_QKV
N_MAIN = COL_SV + SWA_QKV
IN_AB = 2 * GDN_QK + 2 * GDN_VW
IN_SQ = IN_AB + 2 * GDN_HEADS
IN_GA = IN_SQ + 3 * SWA_QKV
NEG_BIG = -1e30


def _cparams(*sem):
    return pltpu.CompilerParams(dimension_semantics=sem, vmem_limit_bytes=VMEM_LIMIT)


def _sigmoid(x):
    return 1.0 / (1.0 + jnp.exp(-x))


def _hdot(a, b):
    return jnp.dot(a, b, precision=HIGHEST, preferred_element_type=F32)


def _hdot_nt(a, b):
    return lax.dot_general(a, b, (((1,), (1,)), ((), ())), precision=HIGHEST, preferred_element_type=F32)


def _hdot_tn(a, b):
    return lax.dot_general(a, b, (((0,), (0,)), ((), ())), precision=HIGHEST, preferred_element_type=F32)


def _dot_nt(a, b):
    return lax.dot_general(a, b, (((1,), (1,)), ((), ())), preferred_element_type=F32)


def _rmsnorm_kernel(x_ref, w_ref, o_ref):
    x = x_ref[...]
    y = x * lax.rsqrt(jnp.mean(x * x, axis=-1, keepdims=True) + EPS) * w_ref[...]
    o_ref[...] = y.astype(o_ref.dtype)


def _rmsnorm(x, w, tm, out_dtype):
    m, d = x.shape
    return pl.pallas_call(
        _rmsnorm_kernel,
        grid=(m // tm,),
        in_specs=[pl.BlockSpec((tm, d), lambda i: (i, 0)), pl.BlockSpec((1, d), lambda i: (0, 0))],
        out_specs=pl.BlockSpec((tm, d), lambda i: (i, 0)),
        out_shape=jax.ShapeDtypeStruct((m, d), out_dtype),
        compiler_params=_cparams("parallel"),
        name="rmsnorm",
    )(x, w.reshape(1, d))


def _mm_kernel(x_ref, w_ref, o_ref):
    o_ref[...] = jnp.dot(x_ref[...].astype(BF16), w_ref[...], preferred_element_type=F32)


def _matmul(x, w, tm, tn, name):
    m, k = x.shape
    n = w.shape[1]
    return pl.pallas_call(
        _mm_kernel,
        grid=(n // tn, m // tm),
        in_specs=[pl.BlockSpec((tm, k), lambda j, i: (i, 0)), pl.BlockSpec((k, tn), lambda j, i: (0, j))],
        out_specs=pl.BlockSpec((tm, tn), lambda j, i: (i, j)),
        out_shape=jax.ShapeDtypeStruct((m, n), F32),
        compiler_params=_cparams("parallel", "parallel"),
        name=name,
    )(x, w)


def _rope_kernel(q_ref, k_ref, cos_ref, sin_ref, qo_ref, ko_ref):
    cosf = cos_ref[...]
    sinf = sin_ref[...]
    lane = lax.broadcasted_iota(jnp.int32, cosf.shape, 1)
    first = lane < ROT_DIM // 2
    for src, dst in ((q_ref, qo_ref), (k_ref, ko_ref)):
        for hh in range(SWA_HEADS):
            sl = slice(hh * SWA_DH, (hh + 1) * SWA_DH)
            x = src[:, sl]
            partner = jnp.where(first, pltpu.roll(x, SWA_DH - ROT_DIM // 2, axis=1), pltpu.roll(x, ROT_DIM // 2, axis=1))
            dst[:, sl] = x * cosf + partner * sinf


def _rope(h_main, cosf, sinf, tm):
    m = h_main.shape[0]
    nt = cosf.shape[0] // tm
    qb, kb = COL_SQ // SWA_W, COL_SK // SWA_W
    return pl.pallas_call(
        _rope_kernel,
        grid=(m // tm, N_SWA),
        in_specs=[
            pl.BlockSpec((tm, SWA_W), lambda i, g: (i, qb + g)),
            pl.BlockSpec((tm, SWA_W), lambda i, g: (i, kb + g)),
            pl.BlockSpec((tm, SWA_DH), lambda i, g: (i % nt, 0)),
            pl.BlockSpec((tm, SWA_DH), lambda i, g: (i % nt, 0)),
        ],
        out_specs=[pl.BlockSpec((tm, SWA_W), lambda i, g: (i, g)), pl.BlockSpec((tm, SWA_W), lambda i, g: (i, g))],
        out_shape=[jax.ShapeDtypeStruct((m, SWA_QKV), F32), jax.ShapeDtypeStruct((m, SWA_QKV), F32)],
        compiler_params=_cparams("parallel", "parallel"),
        name="rope",
    )(h_main, h_main, cosf, sinf)


def _rope_tables(pos):
    half = ROT_DIM // 2
    inv = ROPE_THETA ** (-jnp.arange(half, dtype=F32) * 2.0 / ROT_DIM)
    ang = pos.astype(F32)[:, None] * inv[None, :]
    cos, sin = jnp.cos(ang), jnp.sin(ang)
    n = pos.shape[0]
    cosf = jnp.concatenate([cos, cos, jnp.ones((n, SWA_DH - ROT_DIM), F32)], axis=1)
    sinf = jnp.concatenate([-sin, sin, jnp.zeros((n, SWA_DH - ROT_DIM), F32)], axis=1)
    return cosf, sinf


def _gdn_kernel(alog_ref, dtb_ref, q_ref, k_ref, v_ref, z_ref, ab_ref, cwq_ref, cwk_ref, cwv_ref,
                tq_ref, tk_ref, tv_ref, s0_ref, nw_ref, o_ref, sout_ref,
                s_scr, tailq, tailk, tailv, *, chunk, t_total):
    C = chunk
    h = pl.program_id(1)
    c = pl.program_id(2)

    @pl.when(c == 0)
    def _():
        s_scr[...] = s0_ref[0, 0]
        tailq[...] = tq_ref[0]
        tailk[...] = tk_ref[0]
        tailv[...] = tv_ref[0]

    row8 = lax.broadcasted_iota(jnp.int32, (SUBLANE, LANE), 0)

    def conv(raw, tail_ref, w_ref):
        w = w_ref[...]
        t8 = tail_ref[...]
        y = raw * w[CONV_W - 1:CONV_W, :]
        for s in range(1, CONV_W):
            rolled = pltpu.roll(raw, s, axis=0)
            top = jnp.where(row8 < s, pltpu.roll(t8, s, axis=0), rolled[0:SUBLANE])
            sh = top if C == SUBLANE else jnp.concatenate([top, rolled[SUBLANE:]], axis=0)
            y = y + sh * w[CONV_W - 1 - s:CONV_W - s, :]
        tail_ref[...] = raw[C - SUBLANE:C]
        return y * _sigmoid(y)

    def l2n(x):
        return x * lax.rsqrt(jnp.sum(x * x, axis=-1, keepdims=True) + EPS)

    q = l2n(conv(q_ref[0], tailq, cwq_ref)) * (GDN_DK ** -0.5)
    k = l2n(conv(k_ref[0], tailk, cwk_ref))
    v = conv(v_ref[0], tailv, cwv_ref)

    lane = lax.broadcasted_iota(jnp.int32, (C, LANE), 1)
    ab = ab_ref[0]
    a = jnp.sum(jnp.where(lane == h, ab, 0.0), axis=-1, keepdims=True)
    b = jnp.sum(jnp.where(lane == h + GDN_HEADS, ab, 0.0), axis=-1, keepdims=True)
    beta = _sigmoid(b)
    sp_in = a + dtb_ref[h]
    softplus = jnp.maximum(sp_in, 0.0) + jnp.log(1.0 + jnp.exp(-jnp.abs(sp_in)))
    g = -jnp.exp(jnp.full((C, 1), alog_ref[h], F32)) * softplus
    rowc = lax.broadcasted_iota(jnp.int32, (C, 1), 0)
    valid = (c * C + rowc) < t_total
    beta = jnp.where(valid, beta, 0.0)
    g = jnp.where(valid, g, 0.0)

    ri = lax.broadcasted_iota(jnp.int32, (C, C), 0)
    ci = lax.broadcasted_iota(jnp.int32, (C, C), 1)
    causal = ri >= ci
    strict = ri > ci
    g_b = jnp.broadcast_to(g, (C, LANE))
    gc_b = _hdot(causal.astype(F32), g_b)
    e0 = (lane == 0).astype(F32)
    gc_row = _hdot_nt(e0, gc_b)
    gc_col = gc_b[:, :C]
    decay = jnp.where(causal, jnp.exp(jnp.where(causal, gc_col - gc_row, 0.0)), 0.0)
    kb = k * beta
    lmat = jnp.where(strict, _hdot_nt(kb, k) * decay, 0.0)
    eye = (ri == ci).astype(F32)
    xpow = -lmat
    tinv = eye + xpow
    for _ in range(int(math.log2(C)) - 1):
        xpow = _hdot(xpow, xpow)
        tinv = tinv + _hdot(tinv, xpow)
    eg = jnp.exp(gc_b)
    value = _hdot(tinv, v * beta)
    kcd = _hdot(tinv, kb * eg)
    attn = jnp.where(causal, _hdot_nt(q, k) * decay, 0.0)
    qg = q * eg
    glast = gc_b[C - 1:C, :]
    kdec = k * jnp.exp(glast - gc_b)

    s = s_scr[...]
    v_new = value - _hdot(kcd, s)
    o = _hdot(qg, s) + _hdot(attn, v_new)
    s_new = s * jnp.exp(glast) + _hdot_tn(kdec, v_new)
    s_scr[...] = s_new

    z = z_ref[0]
    on = o * lax.rsqrt(jnp.mean(o * o, axis=-1, keepdims=True) + EPS) * nw_ref[...]
    o_ref[0] = on * (z * _sigmoid(z))

    @pl.when(c == pl.num_programs(2) - 1)
    def _():
        sout_ref[0, 0] = s_new


def _gdn(h_main3, ab3, conv_w, tail8, s0, a_log, dt_bias, gdn_norm, chunk, t_total):
    bn, tp, _ = h_main3.shape
    nc = tp // chunk
    hq, hk, hv, hz = COL_Q // LANE, COL_K // LANE, COL_V // LANE, COL_Z // LANE
    cq, ck, cv = 0, GDN_QK // LANE, 2 * GDN_QK // LANE
    col = lambda off: pl.BlockSpec((1, chunk, LANE), lambda b, h, c: (b, c, off + h))
    cw = lambda off: pl.BlockSpec((CONV_W, LANE), lambda b, h, c: (0, off + h))
    tl = lambda off: pl.BlockSpec((1, SUBLANE, LANE), lambda b, h, c: (b, 0, off + h))
    smem = pl.BlockSpec(memory_space=pltpu.SMEM)
    kern = functools.partial(_gdn_kernel, chunk=chunk, t_total=t_total)
    return pl.pallas_call(
        kern,
        grid=(bn, GDN_HEADS, nc),
        in_specs=[smem, smem, col(hq), col(hk), col(hv), col(hz),
                  pl.BlockSpec((1, chunk, LANE), lambda b, h, c: (b, c, 0)),
                  cw(cq), cw(ck), cw(cv), tl(cq), tl(ck), tl(cv),
                  pl.BlockSpec((1, 1, GDN_DK, GDN_DV), lambda b, h, c: (b, h, 0, 0)),
                  pl.BlockSpec((1, GDN_DV), lambda b, h, c: (0, 0))],
        out_specs=[pl.BlockSpec((1, chunk, LANE), lambda b, h, c: (b, c, h)),
                   pl.BlockSpec((1, 1, GDN_DK, GDN_DV), lambda b, h, c: (b, h, 0, 0))],
        out_shape=[jax.ShapeDtypeStruct((bn, tp, GDN_VW), F32),
                   jax.ShapeDtypeStruct((bn, GDN_HEADS, GDN_DK, GDN_DV), F32)],
        scratch_shapes=[pltpu.VMEM((GDN_DK, GDN_DV), F32)] + [pltpu.VMEM((SUBLANE, LANE), F32)] * 3,
        compiler_params=_cparams("parallel", "parallel", "arbitrary"),
        name="gdn",
    )(a_log, dt_bias, h_main3, h_main3, h_main3, h_main3, ab3, conv_w, conv_w, conv_w,
      tail8, tail8, tail8, s0, gdn_norm.reshape(1, GDN_DV))


def _swa_kernel(q_ref, kp_ref, kc_ref, vp_ref, vc_ref, o_ref, lse_ref):
    qi = pl.program_id(2)
    n = q_ref.shape[1]
    ri = lax.broadcasted_iota(jnp.int32, (n, n), 0)
    ci = lax.broadcasted_iota(jnp.int32, (n, n), 1)
    mask_prev = jnp.logical_and(ci >= ri, qi > 0)
    mask_cur = ci <= ri
    scale = SWA_DH ** -0.5
    for hh in range(SWA_HEADS):
        sl = slice(hh * SWA_DH, (hh + 1) * SWA_DH)
        q = q_ref[0, :, sl].astype(BF16)
        sp = jnp.where(mask_prev, _dot_nt(q, kp_ref[0, :, sl].astype(BF16)) * scale, NEG_BIG)
        sc = jnp.where(mask_cur, _dot_nt(q, kc_ref[0, :, sl].astype(BF16)) * scale, NEG_BIG)
        m = jnp.maximum(jnp.max(sp, axis=-1, keepdims=True), jnp.max(sc, axis=-1, keepdims=True))
        pp = jnp.exp(sp - m)
        pc = jnp.exp(sc - m)
        ssum = jnp.sum(pp, axis=-1, keepdims=True) + jnp.sum(pc, axis=-1, keepdims=True)
        acc = (jnp.dot(pp.astype(BF16), vp_ref[0, :, sl].astype(BF16), preferred_element_type=F32)
               + jnp.dot(pc.astype(BF16), vc_ref[0, :, sl].astype(BF16), preferred_element_type=F32))
        o_ref[0, :, sl] = acc / ssum
        lse_ref[0, :, sl] = jnp.broadcast_to(m + jnp.log(ssum), (n, SWA_DH))


def _swa_prompt_group(q_r, k_r, h_main3, gi, dil, qblk):
    bn, t, _ = q_r.shape
    tl = t // dil
    nq = tl // qblk
    qv = q_r.reshape(bn, tl, dil * SWA_QKV)
    kv = k_r.reshape(bn, tl, dil * SWA_QKV)
    hv = h_main3.reshape(bn, tl, dil * N_MAIN)
    nqk, nmain, voff = SWA_QKV // SWA_W, N_MAIN // SWA_W, COL_SV // SWA_W
    cur = lambda b, r, i: (b, i, r * nqk + gi)
    prev = lambda b, r, i: (b, jnp.maximum(i - 1, 0), r * nqk + gi)
    vcur = lambda b, r, i: (b, i, r * nmain + voff + gi)
    vprev = lambda b, r, i: (b, jnp.maximum(i - 1, 0), r * nmain + voff + gi)
    blk = (1, qblk, SWA_W)
    o, lse = pl.pallas_call(
        _swa_kernel,
        grid=(bn, dil, nq),
        in_specs=[pl.BlockSpec(blk, cur), pl.BlockSpec(blk, prev), pl.BlockSpec(blk, cur),
                  pl.BlockSpec(blk, vprev), pl.BlockSpec(blk, vcur)],
        out_specs=[pl.BlockSpec(blk, lambda b, r, i: (b, i, r))] * 2,
        out_shape=[jax.ShapeDtypeStruct((bn, tl, dil * SWA_W), F32)] * 2,
        compiler_params=_cparams("parallel", "parallel", "arbitrary"),
        name=f"swa_prompt_g{gi}",
    )(qv, kv, kv, hv, hv)
    return o.reshape(bn * t, SWA_W), lse.reshape(bn * t, SWA_W)


def _swa_sample_kernel(q_ref, kn_ref, vn_ref, c0_ref, c1_ref, c2_ref, *out_refs):
    scale = SWA_DH ** -0.5
    for gi, c_ref in enumerate((c0_ref, c1_ref, c2_ref)):
        o_ref, lse_ref = out_refs[2 * gi], out_refs[2 * gi + 1]
        for hh in range(SWA_HEADS):
            sl = slice(gi * SWA_W + hh * SWA_DH, gi * SWA_W + (hh + 1) * SWA_DH)
            q = q_ref[0, :, sl]
            kn = kn_ref[0, :, sl]
            vn = vn_ref[0, :, sl]
            kc = c_ref[0, :, hh * SWA_DH:(hh + 1) * SWA_DH]
            vc = c_ref[0, :, SWA_W + hh * SWA_DH:SWA_W + (hh + 1) * SWA_DH]
            s = jnp.sum(kc * q, axis=-1, keepdims=True) * scale
            sn = jnp.sum(kn * q, axis=-1, keepdims=True) * scale
            m = jnp.maximum(jnp.max(s, axis=0, keepdims=True), sn)
            p = jnp.exp(s - m)
            pn = jnp.exp(sn - m)
            den = jnp.sum(p, axis=0, keepdims=True) + pn
            osl = slice(hh * SWA_DH, (hh + 1) * SWA_DH)
            o_ref[0, :, osl] = (jnp.sum(p * vc, axis=0, keepdims=True) + pn * vn) / den
            lse_ref[0, :, osl] = jnp.broadcast_to(m + jnp.log(den), (1, SWA_DH))


def _swa_sample(q_r, k_r, v_new, caches):
    bn = q_r.shape[0]
    nkeys = SWA_GROUPS[0][0] // SWA_GROUPS[0][1]
    views = []
    for (win, dil), cache in zip(SWA_GROUPS, caches):
        assert cache.shape[1] == win and win // dil == nkeys
        views.append(cache.reshape(bn, win // dil, dil * 2 * SWA_W))
    row = pl.BlockSpec((1, 1, SWA_QKV), lambda b: (b, 0, 0))
    cspec = pl.BlockSpec((1, nkeys, 2 * SWA_W), lambda b: (b, 0, 0))
    ospec = pl.BlockSpec((1, 1, SWA_W), lambda b: (b, 0, 0))
    outs = pl.pallas_call(
        _swa_sample_kernel,
        grid=(bn,),
        in_specs=[row, row, row, cspec, cspec, cspec],
        out_specs=[ospec] * (2 * N_SWA),
        out_shape=[jax.ShapeDtypeStruct((bn, 1, SWA_W), F32)] * (2 * N_SWA),
        compiler_params=_cparams("parallel"),
        name="swa_sample",
    )(q_r.reshape(bn, 1, SWA_QKV), k_r.reshape(bn, 1, SWA_QKV), v_new.reshape(bn, 1, SWA_QKV), *views)
    return [(outs[2 * g].reshape(bn, SWA_W), outs[2 * g + 1].reshape(bn, SWA_W)) for g in range(N_SWA)]


def _mix_kernel(oa_ref, o0_ref, l0_ref, o1_ref, l1_ref, o2_ref, l2_ref, ga_ref, gb_ref, x_ref,
                wa_ref, wb_ref, wo_ref, nw_ref, x1_ref, xn_ref):
    l0, l1, l2 = l0_ref[...], l1_ref[...], l2_ref[...]
    lmax = jnp.maximum(jnp.maximum(l0, l1), l2)
    e0, e1, e2 = jnp.exp(l0 - lmax), jnp.exp(l1 - lmax), jnp.exp(l2 - lmax)
    ob = (e0 * o0_ref[...] + e1 * o1_ref[...] + e2 * o2_ref[...]) / (e0 + e1 + e2)
    pa = jnp.dot(oa_ref[...].astype(BF16), wa_ref[...], preferred_element_type=F32)
    pb = jnp.dot(ob.astype(BF16), wb_ref[...], preferred_element_type=F32)
    merged = _sigmoid(ga_ref[...]) * pa + _sigmoid(gb_ref[...]) * pb
    x1 = x_ref[...] + jnp.dot(merged.astype(BF16), wo_ref[...], preferred_element_type=F32)
    x1_ref[...] = x1
    xn_ref[...] = x1 * lax.rsqrt(jnp.mean(x1 * x1, axis=-1, keepdims=True) + EPS) * nw_ref[...]


def _mix(o_a, parts, h_main, x, wa, wb, wo, norm_ffn, tm):
    m = x.shape[0]
    rowblk = lambda w: pl.BlockSpec((tm, w), lambda i: (i, 0))
    full = lambda a: pl.BlockSpec(a.shape, lambda i: (0, 0))
    nw = norm_ffn.reshape(1, D_MODEL)
    flat = [a for p in parts for a in p]
    return pl.pallas_call(
        _mix_kernel,
        grid=(m // tm,),
        in_specs=[rowblk(GDN_VW)] + [rowblk(SWA_W)] * 6
                 + [pl.BlockSpec((tm, D_MODEL), lambda i: (i, COL_GA // D_MODEL)),
                    pl.BlockSpec((tm, D_MODEL), lambda i: (i, COL_GB // D_MODEL)),
                    rowblk(D_MODEL), full(wa), full(wb), full(wo), full(nw)],
        out_specs=[rowblk(D_MODEL), rowblk(D_MODEL)],
        out_shape=[jax.ShapeDtypeStruct((m, D_MODEL), F32)] * 2,
        compiler_params=_cparams("parallel"),
        name="mix_out",
    )(o_a, *flat, h_main, h_main, x, wa, wb, wo, nw)


def _topk_rows(vals, k, payload=None):
    n, t = vals.shape
    idx = lax.broadcasted_iota(jnp.int32, (n, t), 0)
    out_v, out_i = [], []
    for _ in range(k):
        m = jnp.max(vals, axis=0, keepdims=True)
        am = jnp.min(jnp.where(vals == m, idx, n), axis=0, keepdims=True)
        hit = idx == am
        out_v.append(m)
        out_i.append(am if payload is None else jnp.sum(jnp.where(hit, payload, 0), axis=0, keepdims=True))
        vals = jnp.where(hit, -jnp.inf, vals)
    return jnp.concatenate(out_v, axis=0), jnp.concatenate(out_i, axis=0)


def _route_kernel(q_ref, keys_ref, eid_ref, gate_ref):
    half = PEER_DKEY // 2
    eids, gates = [], []
    for h in range(PEER_HEADS):
        tops = []
        for p in range(2):
            qs = q_ref[:, (2 * h + p) * half:(2 * h + p + 1) * half].astype(BF16)
            st = _dot_nt(keys_ref[h, p].astype(BF16), qs)
            tops.append(_topk_rows(st, PEER_TOPK))
        (v1, i1), (v2, i2) = tops
        cand = jnp.concatenate([v1[i:i + 1] + v2 for i in range(PEER_TOPK)], axis=0)
        ecand = jnp.concatenate([i1[i:i + 1] * PEER_NKEYS + i2 for i in range(PEER_TOPK)], axis=0)
        sc, e = _topk_rows(cand, PEER_TOPK, payload=ecand)
        ex = jnp.exp(sc - sc[0:1])
        gates.append(ex / jnp.sum(ex, axis=0, keepdims=True))
        eids.append(e)
    eid_ref[...] = jnp.concatenate(eids, axis=0).T
    gate_ref[...] = jnp.concatenate(gates, axis=0)


def _route(qp, sub_keys, tt):
    m = qp.shape[0]
    return pl.pallas_call(
        _route_kernel,
        grid=(m // tt,),
        in_specs=[pl.BlockSpec((tt, PEER_HEADS * PEER_DKEY), lambda i: (i, 0)),
                  pl.BlockSpec(sub_keys.shape, lambda i: (0, 0, 0, 0))],
        out_specs=[pl.BlockSpec((tt, PEER_SEL), lambda i: (i, 0)), pl.BlockSpec((PEER_SEL, tt), lambda i: (0, i))],
        out_shape=[jax.ShapeDtypeStruct((m, PEER_SEL), jnp.int32), jax.ShapeDtypeStruct((PEER_SEL, m), F32)],
        compiler_params=_cparams("parallel"),
        name="peer_route",
    )(qp, sub_keys)


def _peer_kernel(eid_ref, gate_ref, xn_ref, x1_ref, nw_ref, down_hbm, up_hbm, y_ref,
                 dbuf, ubuf, dsem, usem, *, n_tok):
    def copies(t, j, slot):
        e = eid_ref[t, j]
        return (pltpu.make_async_copy(down_hbm.at[pl.ds(e, 1)], dbuf.at[slot, pl.ds(j, 1)], dsem.at[slot]),
                pltpu.make_async_copy(up_hbm.at[pl.ds(e, 1)], ubuf.at[slot, pl.ds(j, 1)], usem.at[slot]))

    def issue(t, slot):
        def body(j, carry):
            cd, cu = copies(t, j, slot)
            cd.start()
            cu.start()
            return carry
        lax.fori_loop(0, PEER_SEL, body, 0, unroll=8)

    def wait(slot):
        pltpu.make_async_copy(down_hbm.at[pl.ds(0, PEER_SEL)], dbuf.at[slot], dsem.at[slot]).wait()
        pltpu.make_async_copy(up_hbm.at[pl.ds(0, PEER_SEL)], ubuf.at[slot], usem.at[slot]).wait()

    lane = lax.broadcasted_iota(jnp.int32, (PEER_SEL, gate_ref.shape[1]), 1)
    nw = nw_ref[...]
    issue(0, 0)

    def token(t, carry):
        slot = t % 2

        @pl.when(t + 1 < n_tok)
        def _():
            issue(t + 1, 1 - slot)

        wait(slot)
        x = xn_ref[pl.ds(t, 1), :]
        act = jnp.sum(dbuf[slot] * x, axis=-1, keepdims=True)
        gate = jnp.sum(jnp.where(lane == t, gate_ref[...], 0.0), axis=-1, keepdims=True)
        gelu = 0.5 * act * (1.0 + lax.erf(act * (2.0 ** -0.5)))
        w = gate * gelu
        out = jnp.sum(ubuf[slot] * w, axis=0, keepdims=True)
        x2 = x1_ref[pl.ds(t, 1), :] + out
        y_ref[pl.ds(t, 1), :] = x2 * lax.rsqrt(jnp.mean(x2 * x2, axis=-1, keepdims=True) + EPS) * nw
        return carry

    lax.fori_loop(0, n_tok, token, 0)


def _peer(eid, gates_t, xn2, x1, norm_final, expert_down, expert_up, tt, n_tok):
    m = xn2.shape[0]
    kern = functools.partial(_peer_kernel, n_tok=n_tok)
    return pl.pallas_call(
        kern,
        grid=(m // n_tok,),
        in_specs=[pl.BlockSpec((tt, PEER_SEL), lambda i: (i, 0), memory_space=pltpu.SMEM),
                  pl.BlockSpec((PEER_SEL, tt), lambda i: (0, i)),
                  pl.BlockSpec((n_tok, D_MODEL), lambda i: (i, 0)),
                  pl.BlockSpec((n_tok, D_MODEL), lambda i: (i, 0)),
                  pl.BlockSpec((1, D_MODEL), lambda i: (0, 0)),
                  pl.BlockSpec(memory_space=pl.ANY), pl.BlockSpec(memory_space=pl.ANY)],
        out_specs=pl.BlockSpec((n_tok, D_MODEL), lambda i: (i, 0)),
        out_shape=jax.ShapeDtypeStruct((m, D_MODEL), F32),
        scratch_shapes=[pltpu.VMEM((2, PEER_SEL, D_MODEL), F32), pltpu.VMEM((2, PEER_SEL, D_MODEL), F32),
                        pltpu.SemaphoreType.DMA((2,)), pltpu.SemaphoreType.DMA((2,))],
        compiler_params=_cparams("arbitrary"),
        name="peer_experts",
    )(eid, gates_t, xn2, x1, norm_final.reshape(1, D_MODEL), expert_down, expert_up)


def _pick(m, candidates):
    for c in candidates:
        if m % c == 0:
            return c
    return m


def _layer(x, pos, tail8, s0, caches, w, chunk):
    bn, t, _ = x.shape
    m = bn * t
    x2d = x.reshape(m, D_MODEL)
    tm_big = _pick(m, (1024, 512, 256, 128, 64, 32))

    xn = _rmsnorm(x2d, w["norm_mix"], _pick(m, (512, 256, 128, 64, 32)), BF16)
    h_main = _matmul(xn, w["w_main"], tm_big, 640, "proj_in")
    h_ab = _matmul(xn, w["w_ab"], tm_big, LANE, "proj_ab")

    tp = -(-t // chunk) * chunk
    h3 = h_main.reshape(bn, t, N_MAIN)
    ab3 = h_ab.reshape(bn, t, LANE)
    h3p = h3 if tp == t else jnp.pad(h3, ((0, 0), (0, tp - t), (0, 0)))
    ab3p = ab3 if tp == t else jnp.pad(ab3, ((0, 0), (0, tp - t), (0, 0)))
    o_a, s_new = _gdn(h3p, ab3p, w["conv_w"], tail8, s0, w["a_log"], w["dt_bias"], w["gdn_norm"], chunk, t)
    o_a = o_a[:, :t].reshape(m, GDN_VW)

    cosf, sinf = _rope_tables(pos)
    if caches is not None:
        cosf, sinf = jnp.tile(cosf, (bn, 1)), jnp.tile(sinf, (bn, 1))
    tm_rope = _pick(cosf.shape[0], (512, 256, 128, 64, 32))
    q_r, k_r = _rope(h_main, cosf, sinf, tm_rope)
    if caches is None:
        q3, k3 = q_r.reshape(bn, t, SWA_QKV), k_r.reshape(bn, t, SWA_QKV)
        parts = [_swa_prompt_group(q3, k3, h3, gi, dil, 128) for gi, (_, dil) in enumerate(SWA_GROUPS)]
    else:
        parts = _swa_sample(q_r, k_r, h_main[:, COL_SV:COL_SV + SWA_QKV], caches)

    x1, xn2 = _mix(o_a, parts, h_main, x2d, w["w_br_a"], w["w_br_b"], w["w_out"], w["norm_ffn"],
                   _pick(m, (256, 128, 64, 32)))

    mp = -(-m // LANE) * LANE
    xq = xn2 if mp == m else jnp.pad(xn2, ((0, mp - m), (0, 0)))
    qp = _matmul(xq, w["w_query"], _pick(mp, (1024, 512, 256, 128)), 512, "peer_query")
    eid, gates_t = _route(qp, w["sub_keys"], LANE)
    n_tok = min(m, LANE)
    y = _peer(eid, gates_t, xn2, x1, w["norm_final"], w["expert_down"], w["expert_up"], LANE, n_tok)
    return y.reshape(bn, t, D_MODEL), s_new, h3, k_r.reshape(bn, t, SWA_QKV)


def kernel(x_prompt, x_sample, state_gdn, state_conv, cache_kv_w128, cache_kv_w512, cache_kv_w2048,
           norm_mix, w_in, conv_w, a_log, dt_bias, gdn_norm, w_br_a, w_br_b, w_out, norm_ffn,
           w_query, sub_keys, expert_down, expert_up, norm_final):
    w_main = jnp.concatenate([w_in[:, IN_GA:], w_in[:, :IN_AB], w_in[:, IN_SQ:IN_GA]], axis=1).astype(BF16)
    w_ab = jnp.pad(w_in[:, IN_AB:IN_SQ], ((0, 0), (0, LANE - 2 * GDN_HEADS))).astype(BF16)
    w = dict(norm_mix=norm_mix, w_main=w_main, w_ab=w_ab, conv_w=conv_w, a_log=a_log, dt_bias=dt_bias,
             gdn_norm=gdn_norm, w_br_a=w_br_a.astype(BF16), w_br_b=w_br_b.astype(BF16), w_out=w_out.astype(BF16),
             norm_ffn=norm_ffn, w_query=w_query.astype(BF16), sub_keys=sub_keys,
             expert_down=expert_down, expert_up=expert_up, norm_final=norm_final)

    bp, tlen = x_prompt.shape[:2]
    bs, ts = x_sample.shape[:2]
    caches = (cache_kv_w128, cache_kv_w512, cache_kv_w2048)

    y_p, gdn_p, h3_p, k_p = _layer(
        x_prompt, jnp.arange(tlen, dtype=jnp.int32),
        jnp.zeros((bp, SUBLANE, GDN_CONV_CH), F32), jnp.zeros((bp, GDN_HEADS, GDN_DK, GDN_DV), F32),
        None, w, CHUNK)
    conv_p = h3_p[:, tlen - (CONV_W - 1):, COL_Q:COL_Q + GDN_CONV_CH]
    kv_p = []
    for gi, (win, _) in enumerate(SWA_GROUPS):
        keep = min(win, tlen)
        kk = k_p[:, tlen - keep:, gi * SWA_W:(gi + 1) * SWA_W].reshape(bp, keep, SWA_HEADS, SWA_DH)
        vv = h3_p[:, tlen - keep:, COL_SV + gi * SWA_W:COL_SV + (gi + 1) * SWA_W].reshape(bp, keep, SWA_HEADS, SWA_DH)
        kv_p.append(jnp.stack([kk, vv], axis=2))

    tail8 = jnp.concatenate([jnp.zeros((bs, SUBLANE - (CONV_W - 1), GDN_CONV_CH), F32), state_conv], axis=1)
    y_s, gdn_s, h3_s, k_s = _layer(
        x_sample, PAST_LEN + jnp.arange(ts, dtype=jnp.int32), tail8, state_gdn, caches, w, SUBLANE)
    conv_s = jnp.concatenate([state_conv, h3_s[:, :, COL_Q:COL_Q + GDN_CONV_CH]], axis=1)[:, ts:]
    kv_s = []
    for gi, cache in enumerate(caches):
        kk = k_s[:, :, gi * SWA_W:(gi + 1) * SWA_W].reshape(bs, ts, SWA_HEADS, SWA_DH)
        vv = h3_s[:, :, COL_SV + gi * SWA_W:COL_SV + (gi + 1) * SWA_W].reshape(bs, ts, SWA_HEADS, SWA_DH)
        new = jnp.stack([kk, vv], axis=2).astype(cache.dtype)
        kv_s.append(jnp.concatenate([cache, new], axis=1)[:, ts:])

    return (y_p, y_s, gdn_p.astype(x_prompt.dtype), conv_p, kv_p[0], kv_p[1], kv_p[2],
            gdn_s.astype(state_gdn.dtype), conv_s, kv_s[0], kv_s[1], kv_s[2])
```

```python
import functools
import math

import jax
import jax.numpy as jnp
import numpy as np
from jax import lax
from jax.experimental import pallas as pl
from jax.experimental.pallas import tpu as pltpu

F32 = jnp.float32
BF16 = jnp.bfloat16
HIGHEST = lax.Precision.HIGHEST

LANE = 128
SUBLANE = 8
VMEM_LIMIT = 56 * 1024 * 1024

D_MODEL = 2048
PAST_LEN = 16384
EPS = 1e-6
GDN_HEADS = 8
GDN_DK = 128
GDN_DV = 128
GDN_QK = GDN_HEADS * GDN_DK
GDN_VW = GDN_HEADS * GDN_DV
GDN_CONV_CH = 2 * GDN_QK + GDN_VW
CONV_W = 4
CHUNK = 64
SWA_GROUPS = ((128, 1), (512, 4), (2048, 16))
N_SWA = 3
SWA_HEADS = 4
SWA_DH = 128
SWA_W = SWA_HEADS * SWA_DH
SWA_QKV = N_SWA * SWA_W
ROT_DIM = SWA_DH // 4
ROPE_THETA = 500000.0
PEER_HEADS = 8
PEER_NKEYS = 128
PEER_DKEY = 256
PEER_TOPK = 16
PEER_SEL = PEER_HEADS * PEER_TOPK

COL_GA = 0
COL_GB = D_MODEL
COL_Q = 2 * D_MODEL
COL_K = COL_Q + GDN_QK
COL_V = COL_K + GDN_QK
COL_Z = COL_V + GDN_VW
COL_SQ = COL_Z + GDN_VW
COL_SK = COL_SQ + SWA_QKV
COL_SV = COL_SK + SWA_QKV
N_MAIN = COL_SV + SWA_QKV
IN_AB = 2 * GDN_QK + 2 * GDN_VW
IN_SQ = IN_AB + 2 * GDN_HEADS
IN_GA = IN_SQ + 3 * SWA_QKV
NEG_BIG = -1e30


def _cparams(*sem):
    return pltpu.CompilerParams(dimension_semantics=sem, vmem_limit_bytes=VMEM_LIMIT)


def _sigmoid(x):
    return 1.0 / (1.0 + jnp.exp(-x))


def _hdot(a, b):
    return jnp.dot(a, b, precision=HIGHEST, preferred_element_type=F32)


def _hdot_nt(a, b):
    return lax.dot_general(a, b, (((1,), (1,)), ((), ())), precision=HIGHEST, preferred_element_type=F32)


def _hdot_tn(a, b):
    return lax.dot_general(a, b, (((0,), (0,)), ((), ())), precision=HIGHEST, preferred_element_type=F32)


def _dot_nt(a, b):
    return lax.dot_general(a, b, (((1,), (1,)), ((), ())), preferred_element_type=F32)


def _rmsnorm_kernel(x_ref, w_ref, o_ref):
    x = x_ref[...]
    y = x * lax.rsqrt(jnp.mean(x * x, axis=-1, keepdims=True) + EPS) * w_ref[...]
    o_ref[...] = y.astype(o_ref.dtype)


def _rmsnorm(x, w, tm, out_dtype):
    m, d = x.shape
    return pl.pallas_call(
        _rmsnorm_kernel,
        grid=(m // tm,),
        in_specs=[pl.BlockSpec((tm, d), lambda i: (i, 0)), pl.BlockSpec((1, d), lambda i: (0, 0))],
        out_specs=pl.BlockSpec((tm, d), lambda i: (i, 0)),
        out_shape=jax.ShapeDtypeStruct((m, d), out_dtype),
        compiler_params=_cparams("parallel"),
        name="rmsnorm",
    )(x, w.reshape(1, d))


def _mm_kernel(x_ref, w_ref, o_ref):
    o_ref[...] = jnp.dot(x_ref[...].astype(BF16), w_ref[...], preferred_element_type=F32)


def _matmul(x, w, tm, tn, name):
    m, k = x.shape
    n = w.shape[1]
    return pl.pallas_call(
        _mm_kernel,
        grid=(n // tn, m // tm),
        in_specs=[pl.BlockSpec((tm, k), lambda j, i: (i, 0)), pl.BlockSpec((k, tn), lambda j, i: (0, j))],
        out_specs=pl.BlockSpec((tm, tn), lambda j, i: (i, j)),
        out_shape=jax.ShapeDtypeStruct((m, n), F32),
        compiler_params=_cparams("parallel", "parallel"),
        name=name,
    )(x, w)


def _rope_kernel(q_ref, k_ref, cos_ref, sin_ref, qo_ref, ko_ref):
    cosf = cos_ref[...]
    sinf = sin_ref[...]
    lane = lax.broadcasted_iota(jnp.int32, cosf.shape, 1)
    first = lane < ROT_DIM // 2
    for src, dst in ((q_ref, qo_ref), (k_ref, ko_ref)):
        for hh in range(SWA_HEADS):
            sl = slice(hh * SWA_DH, (hh + 1) * SWA_DH)
            x = src[:, sl]
            partner = jnp.where(first, pltpu.roll(x, SWA_DH - ROT_DIM // 2, axis=1), pltpu.roll(x, ROT_DIM // 2, axis=1))
            dst[:, sl] = x * cosf + partner * sinf


def _rope(h_main, cosf, sinf, tm):
    m = h_main.shape[0]
    nt = cosf.shape[0] // tm
    qb, kb = COL_SQ // SWA_W, COL_SK // SWA_W
    return pl.pallas_call(
        _rope_kernel,
        grid=(m // tm, N_SWA),
        in_specs=[
            pl.BlockSpec((tm, SWA_W), lambda i, g: (i, qb + g)),
            pl.BlockSpec((tm, SWA_W), lambda i, g: (i, kb + g)),
            pl.BlockSpec((tm, SWA_DH), lambda i, g: (i % nt, 0)),
            pl.BlockSpec((tm, SWA_DH), lambda i, g: (i % nt, 0)),
        ],
        out_specs=[pl.BlockSpec((tm, SWA_W), lambda i, g: (i, g)), pl.BlockSpec((tm, SWA_W), lambda i, g: (i, g))],
        out_shape=[jax.ShapeDtypeStruct((m, SWA_QKV), F32), jax.ShapeDtypeStruct((m, SWA_QKV), F32)],
        compiler_params=_cparams("parallel", "parallel"),
        name="rope",
    )(h_main, h_main, cosf, sinf)


def _rope_split_kernel(*refs, dils, tm):
    ins, cos_ref, sin_ref = refs[:3 * N_SWA], refs[3 * N_SWA], refs[3 * N_SWA + 1]
    outs, slab = refs[3 * N_SWA + 2:6 * N_SWA + 2], refs[6 * N_SWA + 2]
    cosf = cos_ref[...]
    sinf = sin_ref[...]
    first = lax.broadcasted_iota(jnp.int32, cosf.shape, 1) < ROT_DIM // 2
    for gi, dil in enumerate(dils):
        for kind in range(3):
            src, dst = ins[3 * gi + kind], outs[3 * gi + kind]
            for hh in range(SWA_HEADS):
                sl = slice(hh * SWA_DH, (hh + 1) * SWA_DH)
                x = src[0, :, sl]
                if kind < 2:
                    partner = jnp.where(first, pltpu.roll(x, SWA_DH - ROT_DIM // 2, axis=1),
                                        pltpu.roll(x, ROT_DIM // 2, axis=1))
                    x = x * cosf + partner * sinf
                if dil == 1:
                    dst[0, 0, :, sl] = x
                else:
                    slab[hh] = x
                    for r in range(dil):
                        dst[0, r, :, sl] = slab[hh, pl.ds(r, tm // dil, stride=dil), :]


def _rope_split(h3, cosf, sinf, tm):
    bn, t, _ = h3.shape
    dils = tuple(d for _, d in SWA_GROUPS)
    in_specs, out_specs, out_shape = [], [], []
    for gi, dil in enumerate(dils):
        for off in (COL_SQ, COL_SK, COL_SV):
            cb = off // SWA_W + gi
            in_specs.append(pl.BlockSpec((1, tm, SWA_W), lambda b, i, cb=cb: (b, i, cb)))
            out_specs.append(pl.BlockSpec((1, dil, tm // dil, SWA_W), lambda b, i: (b, 0, i, 0)))
            out_shape.append(jax.ShapeDtypeStruct((bn, dil, t // dil, SWA_W), F32))
    tbl = pl.BlockSpec((tm, SWA_DH), lambda b, i: (i, 0))
    outs = pl.pallas_call(
        functools.partial(_rope_split_kernel, dils=dils, tm=tm),
        grid=(bn, t // tm),
        in_specs=in_specs + [tbl, tbl],
        out_specs=out_specs,
        out_shape=out_shape,
        scratch_shapes=[pltpu.VMEM((SWA_HEADS, tm, SWA_DH), F32)],
        compiler_params=_cparams("parallel", "parallel"),
        name="rope_split",
    )(*([h3] * (3 * N_SWA)), cosf, sinf)
    return [tuple(outs[3 * gi:3 * gi + 3]) for gi in range(N_SWA)]


def _rope_tables(pos):
    half = ROT_DIM // 2
    inv = ROPE_THETA ** (-jnp.arange(half, dtype=F32) * 2.0 / ROT_DIM)
    ang = pos.astype(F32)[:, None] * inv[None, :]
    cos, sin = jnp.cos(ang), jnp.sin(ang)
    n = pos.shape[0]
    cosf = jnp.concatenate([cos, cos, jnp.ones((n, SWA_DH - ROT_DIM), F32)], axis=1)
    sinf = jnp.concatenate([-sin, sin, jnp.zeros((n, SWA_DH - ROT_DIM), F32)], axis=1)
    return cosf, sinf


def _gdn_kernel(alog_ref, dtb_ref, q_ref, k_ref, v_ref, z_ref, ab_ref, cwq_ref, cwk_ref, cwv_ref,
                tq_ref, tk_ref, tv_ref, s0_ref, nw_ref, o_ref, sout_ref,
                s_scr, tailq, tailk, tailv, *, chunk, t_total, hps):
    c = pl.program_id(2)

    @pl.when(c == 0)
    def _():
        s_scr[...] = s0_ref[0]
        tailq[...] = tq_ref[0]
        tailk[...] = tk_ref[0]
        tailv[...] = tv_ref[0]

    C = chunk
    heads = range(hps)
    sls = [slice(hh * LANE, (hh + 1) * LANE) for hh in heads]
    hidx = [pl.program_id(1) * hps + hh for hh in heads]
    row8 = lax.broadcasted_iota(jnp.int32, (SUBLANE, LANE), 0)

    def conv(src_ref, tail_ref, w_ref, sl):
        raw = src_ref[0, :, sl]
        w = w_ref[:, sl]
        t8 = tail_ref[:, sl]
        y = raw * w[CONV_W - 1:CONV_W, :]
        for s in range(1, CONV_W):
            rolled = pltpu.roll(raw, s, axis=0)
            top = jnp.where(row8 < s, pltpu.roll(t8, s, axis=0), rolled[0:SUBLANE])
            sh = top if C == SUBLANE else jnp.concatenate([top, rolled[SUBLANE:]], axis=0)
            y = y + sh * w[CONV_W - 1 - s:CONV_W - s, :]
        tail_ref[:, sl] = raw[C - SUBLANE:C]
        return y * _sigmoid(y)

    def l2n(x):
        return x * lax.rsqrt(jnp.sum(x * x, axis=-1, keepdims=True) + EPS)

    q = [l2n(conv(q_ref, tailq, cwq_ref, sl)) * (GDN_DK ** -0.5) for sl in sls]
    k = [l2n(conv(k_ref, tailk, cwk_ref, sl)) for sl in sls]
    v = [conv(v_ref, tailv, cwv_ref, sl) for sl in sls]

    lane = lax.broadcasted_iota(jnp.int32, (C, LANE), 1)
    rowc = lax.broadcasted_iota(jnp.int32, (C, 1), 0)
    valid = (c * C + rowc) < t_total
    ab = ab_ref[0]
    beta, g = [], []
    for h in hidx:
        a = jnp.sum(jnp.where(lane == h, ab, 0.0), axis=-1, keepdims=True)
        b = jnp.sum(jnp.where(lane == h + GDN_HEADS, ab, 0.0), axis=-1, keepdims=True)
        sp_in = a + dtb_ref[h]
        softplus = jnp.maximum(sp_in, 0.0) + jnp.log(1.0 + jnp.exp(-jnp.abs(sp_in)))
        beta.append(jnp.where(valid, _sigmoid(b), 0.0))
        g.append(jnp.where(valid, -jnp.exp(jnp.full((C, 1), alog_ref[h], F32)) * softplus, 0.0))

    ri = lax.broadcasted_iota(jnp.int32, (C, C), 0)
    ci = lax.broadcasted_iota(jnp.int32, (C, C), 1)
    causal = ri >= ci
    strict = ri > ci
    tril = causal.astype(F32)
    e0 = (lane == 0).astype(F32)
    eye = (ri == ci).astype(F32)
    gc_b = [_hdot(tril, jnp.broadcast_to(gh, (C, LANE))) for gh in g]
    gc_row = [_hdot_nt(e0, gch) for gch in gc_b]
    decay = [jnp.where(causal, jnp.exp(jnp.where(causal, gch[:, :C] - grh, 0.0)), 0.0)
             for gch, grh in zip(gc_b, gc_row)]
    kb = [kh * bh for kh, bh in zip(k, beta)]
    xpow = [-jnp.where(strict, _hdot_nt(kbh, kh) * dh, 0.0) for kbh, kh, dh in zip(kb, k, decay)]
    tinv = [eye + xh for xh in xpow]
    for _ in range(int(math.log2(C)) - 1):
        xpow = [_hdot(xh, xh) for xh in xpow]
        tinv = [th + _hdot(th, xh) for th, xh in zip(tinv, xpow)]
    eg = [jnp.exp(gch) for gch in gc_b]
    value = [_hdot(th, vh * bh) for th, vh, bh in zip(tinv, v, beta)]
    kcd = [_hdot(th, kbh * egh) for th, kbh, egh in zip(tinv, kb, eg)]
    attn = [jnp.where(causal, _hdot_nt(qh, kh) * dh, 0.0) for qh, kh, dh in zip(q, k, decay)]
    glast = [gch[C - 1:C, :] for gch in gc_b]
    kdec = [kh * jnp.exp(glh - gch) for kh, glh, gch in zip(k, glast, gc_b)]

    s = [s_scr[hh] for hh in heads]
    v_new = [vh - _hdot(kh, sh) for vh, kh, sh in zip(value, kcd, s)]
    o = [_hdot(qh * egh, sh) + _hdot(ah, vnh) for qh, egh, sh, ah, vnh in zip(q, eg, s, attn, v_new)]
    for hh in heads:
        s_scr[hh] = s[hh] * jnp.exp(glast[hh]) + _hdot_tn(kdec[hh], v_new[hh])

    nw = nw_ref[...]
    for hh, sl in zip(heads, sls):
        z = z_ref[0, :, sl]
        on = o[hh] * lax.rsqrt(jnp.mean(o[hh] * o[hh], axis=-1, keepdims=True) + EPS) * nw
        o_ref[0, :, sl] = on * (z * _sigmoid(z))

    @pl.when(c == pl.num_programs(2) - 1)
    def _():
        sout_ref[0] = s_scr[...]


GDN_HEADS_PER_STEP = 4


def _gdn(h_main3, ab3, conv_w, tail8, s0, a_log, dt_bias, gdn_norm, chunk, t_total):
    bn, tp, _ = h_main3.shape
    nc = tp // chunk
    hps = GDN_HEADS_PER_STEP
    w = hps * LANE
    hq, hk, hv, hz = COL_Q // w, COL_K // w, COL_V // w, COL_Z // w
    cq, ck, cv = 0, GDN_QK // w, 2 * GDN_QK // w
    col = lambda off: pl.BlockSpec((1, chunk, w), lambda b, h, c: (b, c, off + h))
    cw = lambda off: pl.BlockSpec((CONV_W, w), lambda b, h, c: (0, off + h))
    tl = lambda off: pl.BlockSpec((1, SUBLANE, w), lambda b, h, c: (b, 0, off + h))
    smem = pl.BlockSpec(memory_space=pltpu.SMEM)
    kern = functools.partial(_gdn_kernel, chunk=chunk, t_total=t_total, hps=hps)
    return pl.pallas_call(
        kern,
        grid=(bn, GDN_HEADS // hps, nc),
        in_specs=[smem, smem, col(hq), col(hk), col(hv), col(hz),
                  pl.BlockSpec((1, chunk, LANE), lambda b, h, c: (b, c, 0)),
                  cw(cq), cw(ck), cw(cv), tl(cq), tl(ck), tl(cv),
                  pl.BlockSpec((1, hps, GDN_DK, GDN_DV), lambda b, h, c: (b, h, 0, 0)),
                  pl.BlockSpec((1, GDN_DV), lambda b, h, c: (0, 0))],
        out_specs=[pl.BlockSpec((1, chunk, w), lambda b, h, c: (b, c, h)),
                   pl.BlockSpec((1, hps, GDN_DK, GDN_DV), lambda b, h, c: (b, h, 0, 0))],
        out_shape=[jax.ShapeDtypeStruct((bn, tp, GDN_VW), F32),
                   jax.ShapeDtypeStruct((bn, GDN_HEADS, GDN_DK, GDN_DV), F32)],
        scratch_shapes=[pltpu.VMEM((hps, GDN_DK, GDN_DV), F32)] + [pltpu.VMEM((SUBLANE, w), F32)] * 3,
        compiler_params=_cparams("parallel", "parallel", "arbitrary"),
        name="gdn",
    )(a_log, dt_bias, h_main3, h_main3, h_main3, h_main3, ab3, conv_w, conv_w, conv_w,
      tail8, tail8, tail8, s0, gdn_norm.reshape(1, GDN_DV))


def _swa_kernel(q_ref, kp_ref, kc_ref, vp_ref, vc_ref, o_ref, lse_ref):
    qi = pl.program_id(2)
    n = q_ref.shape[2]
    ri = lax.broadcasted_iota(jnp.int32, (n, n), 0)
    ci = lax.broadcasted_iota(jnp.int32, (n, n), 1)
    mask_prev = jnp.logical_and(ci >= ri, qi > 0)
    mask_cur = ci <= ri
    scale = SWA_DH ** -0.5
    for hh in range(SWA_HEADS):
        sl = slice(hh * SWA_DH, (hh + 1) * SWA_DH)
        q = q_ref[0, 0, :, sl].astype(BF16)
        sp = jnp.where(mask_prev, _dot_nt(q, kp_ref[0, 0, :, sl].astype(BF16)) * scale, NEG_BIG)
        sc = jnp.where(mask_cur, _dot_nt(q, kc_ref[0, 0, :, sl].astype(BF16)) * scale, NEG_BIG)
        m = jnp.maximum(jnp.max(sp, axis=-1, keepdims=True), jnp.max(sc, axis=-1, keepdims=True))
        pp = jnp.exp(sp - m)
        pc = jnp.exp(sc - m)
        ssum = jnp.sum(pp, axis=-1, keepdims=True) + jnp.sum(pc, axis=-1, keepdims=True)
        acc = (jnp.dot(pp.astype(BF16), vp_ref[0, 0, :, sl].astype(BF16), preferred_element_type=F32)
               + jnp.dot(pc.astype(BF16), vc_ref[0, 0, :, sl].astype(BF16), preferred_element_type=F32))
        o_ref[0, 0, :, sl] = acc / ssum
        lse_ref[0, 0, :, sl] = jnp.broadcast_to(m + jnp.log(ssum), (n, SWA_DH))


def _swa_prompt_group(q, k, v, gi, qblk):
    bn, dil, tl, _ = q.shape
    cur = lambda b, r, i: (b, r, i, 0)
    prev = lambda b, r, i: (b, r, jnp.maximum(i - 1, 0), 0)
    blk = (1, 1, qblk, SWA_W)
    return pl.pallas_call(
        _swa_kernel,
        grid=(bn, dil, tl // qblk),
        in_specs=[pl.BlockSpec(blk, cur), pl.BlockSpec(blk, prev), pl.BlockSpec(blk, cur),
                  pl.BlockSpec(blk, prev), pl.BlockSpec(blk, cur)],
        out_specs=[pl.BlockSpec(blk, cur)] * 2,
        out_shape=[jax.ShapeDtypeStruct(q.shape, F32)] * 2,
        compiler_params=_cparams("parallel", "parallel", "arbitrary"),
        name=f"swa_prompt_g{gi}",
    )(q, k, k, v, v)


def _swa_sample_kernel(q_ref, kn_ref, vn_ref, c0_ref, c1_ref, c2_ref, *out_refs):
    scale = SWA_DH ** -0.5
    for gi, c_ref in enumerate((c0_ref, c1_ref, c2_ref)):
        o_ref, lse_ref = out_refs[2 * gi], out_refs[2 * gi + 1]
        for hh in range(SWA_HEADS):
            sl = slice(gi * SWA_W + hh * SWA_DH, gi * SWA_W + (hh + 1) * SWA_DH)
            q = q_ref[0, :, sl]
            kn = kn_ref[0, :, sl]
            vn = vn_ref[0, :, sl]
            kc = c_ref[0, :, hh * SWA_DH:(hh + 1) * SWA_DH]
            vc = c_ref[0, :, SWA_W + hh * SWA_DH:SWA_W + (hh + 1) * SWA_DH]
            s = jnp.sum(kc * q, axis=-1, keepdims=True) * scale
            sn = jnp.sum(kn * q, axis=-1, keepdims=True) * scale
            m = jnp.maximum(jnp.max(s, axis=0, keepdims=True), sn)
            p = jnp.exp(s - m)
            pn = jnp.exp(sn - m)
            den = jnp.sum(p, axis=0, keepdims=True) + pn
            osl = slice(hh * SWA_DH, (hh + 1) * SWA_DH)
            o_ref[0, :, osl] = (jnp.sum(p * vc, axis=0, keepdims=True) + pn * vn) / den
            lse_ref[0, :, osl] = jnp.broadcast_to(m + jnp.log(den), (1, SWA_DH))


def _swa_sample(q_r, k_r, v_new, caches):
    bn = q_r.shape[0]
    nkeys = SWA_GROUPS[0][0] // SWA_GROUPS[0][1]
    views = []
    for (win, dil), cache in zip(SWA_GROUPS, caches):
        assert cache.shape[1] == win and win // dil == nkeys
        views.append(cache.reshape(bn, win // dil, dil * 2 * SWA_W))
    row = pl.BlockSpec((1, 1, SWA_QKV), lambda b: (b, 0, 0))
    cspec = pl.BlockSpec((1, nkeys, 2 * SWA_W), lambda b: (b, 0, 0))
    ospec = pl.BlockSpec((1, 1, SWA_W), lambda b: (b, 0, 0))
    outs = pl.pallas_call(
        _swa_sample_kernel,
        grid=(bn,),
        in_specs=[row, row, row, cspec, cspec, cspec],
        out_specs=[ospec] * (2 * N_SWA),
        out_shape=[jax.ShapeDtypeStruct((bn, 1, SWA_W), F32)] * (2 * N_SWA),
        compiler_params=_cparams("parallel"),
        name="swa_sample",
    )(q_r.reshape(bn, 1, SWA_QKV), k_r.reshape(bn, 1, SWA_QKV), v_new.reshape(bn, 1, SWA_QKV), *views)
    return [(outs[2 * g].reshape(bn, SWA_W), outs[2 * g + 1].reshape(bn, SWA_W)) for g in range(N_SWA)]


def _mix_kernel(oa_ref, o0_ref, l0_ref, o1_ref, l1_ref, o2_ref, l2_ref, ga_ref, gb_ref, x_ref,
                wa_ref, wb_ref, wo_ref, nw_ref, x1_ref, xn_ref, slab, *, dils, tm):
    def token_major(ref, dil):
        if dil == 1:
            return ref[0, 0]
        cols = []
        for hh in range(SWA_HEADS):
            for r in range(dil):
                slab[hh, pl.ds(r, tm // dil, stride=dil), :] = ref[0, r, :, hh * SWA_DH:(hh + 1) * SWA_DH]
            cols.append(slab[hh])
        return jnp.concatenate(cols, axis=1)

    os = [token_major(r, d) for r, d in zip((o0_ref, o1_ref, o2_ref), dils)]
    ls = [token_major(r, d) for r, d in zip((l0_ref, l1_ref, l2_ref), dils)]
    lmax = jnp.maximum(jnp.maximum(ls[0], ls[1]), ls[2])
    es = [jnp.exp(l - lmax) for l in ls]
    ob = (es[0] * os[0] + es[1] * os[1] + es[2] * os[2]) / (es[0] + es[1] + es[2])
    pa = jnp.dot(oa_ref[0].astype(BF16), wa_ref[...], preferred_element_type=F32)
    pb = jnp.dot(ob.astype(BF16), wb_ref[...], preferred_element_type=F32)
    merged = _sigmoid(ga_ref[0]) * pa + _sigmoid(gb_ref[0]) * pb
    x1 = x_ref[0] + jnp.dot(merged.astype(BF16), wo_ref[...], preferred_element_type=F32)
    x1_ref[0] = x1
    xn_ref[0] = x1 * lax.rsqrt(jnp.mean(x1 * x1, axis=-1, keepdims=True) + EPS) * nw_ref[...]


def _mix(o_a3, parts, dils, h3, x3, wa, wb, wo, norm_ffn, tm):
    bn, t, _ = x3.shape
    rowblk = lambda w, cb=0: pl.BlockSpec((1, tm, w), lambda b, i: (b, i, cb))
    full = lambda a: pl.BlockSpec(a.shape, lambda b, i: (0, 0))
    nw = norm_ffn.reshape(1, D_MODEL)
    part_specs, flat = [], []
    for (o, lse), dil in zip(parts, dils):
        spec = pl.BlockSpec((1, dil, tm // dil, SWA_W), lambda b, i: (b, 0, i, 0))
        part_specs += [spec, spec]
        flat += [o, lse]
    return pl.pallas_call(
        functools.partial(_mix_kernel, dils=dils, tm=tm),
        grid=(bn, t // tm),
        in_specs=[rowblk(GDN_VW)] + part_specs
                 + [rowblk(D_MODEL, COL_GA // D_MODEL), rowblk(D_MODEL, COL_GB // D_MODEL),
                    rowblk(D_MODEL), full(wa), full(wb), full(wo), full(nw)],
        out_specs=[rowblk(D_MODEL), rowblk(D_MODEL)],
        out_shape=[jax.ShapeDtypeStruct((bn, t, D_MODEL), F32)] * 2,
        scratch_shapes=[pltpu.VMEM((SWA_HEADS, tm, SWA_DH), F32)],
        compiler_params=_cparams("parallel", "parallel"),
        name="mix_out",
    )(o_a3, *flat, h3, h3, x3, wa, wb, wo, nw)


def _topk_rows(vals, k, payload=None):
    n, t = vals.shape
    idx = lax.broadcasted_iota(jnp.int32, (n, t), 0)
    out_v, out_i = [], []
    for _ in range(k):
        m = jnp.max(vals, axis=0, keepdims=True)
        am = jnp.min(jnp.where(vals == m, idx, n), axis=0, keepdims=True)
        hit = idx == am
        out_v.append(m)
        out_i.append(am if payload is None else jnp.sum(jnp.where(hit, payload, 0), axis=0, keepdims=True))
        vals = jnp.where(hit, -jnp.inf, vals)
    return jnp.concatenate(out_v, axis=0), jnp.concatenate(out_i, axis=0)


def _route_kernel(q_ref, keys_ref, eid_ref, gate_ref):
    half = PEER_DKEY // 2
    eids, gates = [], []
    for h in range(PEER_HEADS):
        tops = []
        for p in range(2):
            qs = q_ref[:, (2 * h + p) * half:(2 * h + p + 1) * half].astype(BF16)
            st = _dot_nt(keys_ref[h, p].astype(BF16), qs)
            tops.append(_topk_rows(st, PEER_TOPK))
        (v1, i1), (v2, i2) = tops
        cand = jnp.concatenate([v1[i:i + 1] + v2 for i in range(PEER_TOPK)], axis=0)
        ecand = jnp.concatenate([i1[i:i + 1] * PEER_NKEYS + i2 for i in range(PEER_TOPK)], axis=0)
        sc, e = _topk_rows(cand, PEER_TOPK, payload=ecand)
        ex = jnp.exp(sc - sc[0:1])
        gates.append(ex / jnp.sum(ex, axis=0, keepdims=True))
        eids.append(e)
    eid_ref[...] = jnp.concatenate(eids, axis=0).T
    gate_ref[...] = jnp.concatenate(gates, axis=0)


def _route(qp, sub_keys, tt):
    m = qp.shape[0]
    return pl.pallas_call(
        _route_kernel,
        grid=(m // tt,),
        in_specs=[pl.BlockSpec((tt, PEER_HEADS * PEER_DKEY), lambda i: (i, 0)),
                  pl.BlockSpec(sub_keys.shape, lambda i: (0, 0, 0, 0))],
        out_specs=[pl.BlockSpec((tt, PEER_SEL), lambda i: (i, 0)), pl.BlockSpec((PEER_SEL, tt), lambda i: (0, i))],
        out_shape=[jax.ShapeDtypeStruct((m, PEER_SEL), jnp.int32), jax.ShapeDtypeStruct((PEER_SEL, m), F32)],
        compiler_params=_cparams("parallel"),
        name="peer_route",
    )(qp, sub_keys)


PEER_ISSUE_UNROLL = 8


def _peer_kernel(eid_ref, gate_ref, xn_ref, x1_ref, nw_ref, tbl_hbm, y_ref, tbuf, sem, *, n_tok):
    def row_copy(t, j, slot):
        e = eid_ref[t, j]
        return pltpu.make_async_copy(tbl_hbm.at[pl.ds(e, 1)], tbuf.at[slot, pl.ds(j, 1)], sem.at[slot])

    def issue(t, slot):
        def body(jj, carry):
            for u in range(PEER_ISSUE_UNROLL):
                row_copy(t, jj * PEER_ISSUE_UNROLL + u, slot).start(priority=u % 2)
            return carry
        lax.fori_loop(0, PEER_SEL // PEER_ISSUE_UNROLL, body, 0)

    def wait(slot):
        pltpu.make_async_copy(tbl_hbm.at[pl.ds(0, PEER_SEL)], tbuf.at[slot], sem.at[slot]).wait()

    lane = lax.broadcasted_iota(jnp.int32, (PEER_SEL, gate_ref.shape[1]), 1)
    nw = nw_ref[...]
    issue(0, 0)

    def token(t, carry):
        slot = t % 2

        @pl.when(t + 1 < n_tok)
        def _():
            issue(t + 1, 1 - slot)

        wait(slot)
        x = xn_ref[pl.ds(t, 1), :]
        act = jnp.sum(tbuf[slot, :, :D_MODEL] * x, axis=-1, keepdims=True)
        gate = jnp.sum(jnp.where(lane == t, gate_ref[...], 0.0), axis=-1, keepdims=True)
        gelu = 0.5 * act * (1.0 + lax.erf(act * (2.0 ** -0.5)))
        w = gate * gelu
        out = jnp.sum(tbuf[slot, :, D_MODEL:] * w, axis=0, keepdims=True)
        x2 = x1_ref[pl.ds(t, 1), :] + out
        y_ref[pl.ds(t, 1), :] = x2 * lax.rsqrt(jnp.mean(x2 * x2, axis=-1, keepdims=True) + EPS) * nw
        return carry

    lax.fori_loop(0, n_tok, token, 0)


def _peer(eid, gates_t, xn2, x1, norm_final, table, tt, n_tok):
    m = xn2.shape[0]
    kern = functools.partial(_peer_kernel, n_tok=n_tok)
    return pl.pallas_call(
        kern,
        grid=(m // n_tok,),
        in_specs=[pl.BlockSpec((tt, PEER_SEL), lambda i: (i, 0), memory_space=pltpu.SMEM),
                  pl.BlockSpec((PEER_SEL, tt), lambda i: (0, i)),
                  pl.BlockSpec((n_tok, D_MODEL), lambda i: (i, 0)),
                  pl.BlockSpec((n_tok, D_MODEL), lambda i: (i, 0)),
                  pl.BlockSpec((1, D_MODEL), lambda i: (0, 0)),
                  pl.BlockSpec(memory_space=pl.ANY)],
        out_specs=pl.BlockSpec((n_tok, D_MODEL), lambda i: (i, 0)),
        out_shape=jax.ShapeDtypeStruct((m, D_MODEL), F32),
        scratch_shapes=[pltpu.VMEM((2, PEER_SEL, 2 * D_MODEL), F32), pltpu.SemaphoreType.DMA((2,))],
        compiler_params=_cparams("arbitrary"),
        name="peer_experts",
    )(eid, gates_t, xn2, x1, norm_final.reshape(1, D_MODEL), table)


def _pick(m, candidates):
    for c in candidates:
        if m % c == 0:
            return c
    return m


def _layer(x, pos, tail8, s0, caches, w, chunk):
    bn, t, _ = x.shape
    m = bn * t
    x2d = x.reshape(m, D_MODEL)
    tm_big = _pick(m, (1024, 512, 256, 128, 64, 32))

    xn = _rmsnorm(x2d, w["norm_mix"], _pick(m, (512, 256, 128, 64, 32)), BF16)
    h_main = _matmul(xn, w["w_main"], tm_big, 640, "proj_in")
    h_ab = _matmul(xn, w["w_ab"], tm_big, LANE, "proj_ab")

    tp = -(-t // chunk) * chunk
    h3 = h_main.reshape(bn, t, N_MAIN)
    ab3 = h_ab.reshape(bn, t, LANE)
    h3p = h3 if tp == t else jnp.pad(h3, ((0, 0), (0, tp - t), (0, 0)))
    ab3p = ab3 if tp == t else jnp.pad(ab3, ((0, 0), (0, tp - t), (0, 0)))
    o_a, s_new = _gdn(h3p, ab3p, w["conv_w"], tail8, s0, w["a_log"], w["dt_bias"], w["gdn_norm"], chunk, t)
    o_a = o_a[:, :t]

    cosf, sinf = _rope_tables(pos)
    kv_new = []
    if caches is None:
        qkv = _rope_split(h3, cosf, sinf, 256)
        parts = [_swa_prompt_group(q, k, v, gi, 128) for gi, (q, k, v) in enumerate(qkv)]
        dils = tuple(d for _, d in SWA_GROUPS)
        for (win, dil), (_, k, v) in zip(SWA_GROUPS, qkv):
            keep = min(win, t)
            tok = lambda a: (a[:, :, (t - keep) // dil:].transpose(0, 2, 1, 3)
                             .reshape(bn, keep, SWA_HEADS, SWA_DH))
            kv_new.append(jnp.stack([tok(k), tok(v)], axis=2))
        mix_shape = (bn, t)
    else:
        cosf, sinf = jnp.tile(cosf, (bn, 1)), jnp.tile(sinf, (bn, 1))
        q_r, k_r = _rope(h_main, cosf, sinf, _pick(m, (512, 256, 128, 64, 32)))
        v_new = h_main[:, COL_SV:COL_SV + SWA_QKV]
        parts = [(o.reshape(1, 1, m, SWA_W), lse.reshape(1, 1, m, SWA_W))
                 for o, lse in _swa_sample(q_r, k_r, v_new, caches)]
        dils = (1,) * N_SWA
        for gi in range(N_SWA):
            kk = k_r[:, gi * SWA_W:(gi + 1) * SWA_W].reshape(bn, t, SWA_HEADS, SWA_DH)
            vv = v_new[:, gi * SWA_W:(gi + 1) * SWA_W].reshape(bn, t, SWA_HEADS, SWA_DH)
            kv_new.append(jnp.stack([kk, vv], axis=2))
        mix_shape = (1, m)

    mb, mt = mix_shape
    x1, xn2 = _mix(o_a.reshape(mb, mt, GDN_VW), parts, dils, h_main.reshape(mb, mt, N_MAIN),
                   x.reshape(mb, mt, D_MODEL), w["w_br_a"], w["w_br_b"], w["w_out"], w["norm_ffn"],
                   _pick(mt, (256, 128, 64, 32)))
    x1, xn2 = x1.reshape(m, D_MODEL), xn2.reshape(m, D_MODEL)

    mp = -(-m // LANE) * LANE
    xq = xn2 if mp == m else jnp.pad(xn2, ((0, mp - m), (0, 0)))
    qp = _matmul(xq, w["w_query"], _pick(mp, (1024, 512, 256, 128)), 512, "peer_query")
    eid, gates_t = _route(qp, w["sub_keys"], LANE)
    n_tok = min(m, LANE)
    y = _peer(eid, gates_t, xn2, x1, w["norm_final"], w["peer_table"], LANE, n_tok)
    return y.reshape(bn, t, D_MODEL), s_new, h3, kv_new


def kernel(x_prompt, x_sample, state_gdn, state_conv, cache_kv_w128, cache_kv_w512, cache_kv_w2048,
           norm_mix, w_in, conv_w, a_log, dt_bias, gdn_norm, w_br_a, w_br_b, w_out, norm_ffn,
           w_query, sub_keys, expert_down, expert_up, norm_final):
    w_main = jnp.concatenate([w_in[:, IN_GA:], w_in[:, :IN_AB], w_in[:, IN_SQ:IN_GA]], axis=1).astype(BF16)
    w_ab = jnp.pad(w_in[:, IN_AB:IN_SQ], ((0, 0), (0, LANE - 2 * GDN_HEADS))).astype(BF16)
    w = dict(norm_mix=norm_mix, w_main=w_main, w_ab=w_ab, conv_w=conv_w, a_log=a_log, dt_bias=dt_bias,
             gdn_norm=gdn_norm, w_br_a=w_br_a.astype(BF16), w_br_b=w_br_b.astype(BF16), w_out=w_out.astype(BF16),
             norm_ffn=norm_ffn, w_query=w_query.astype(BF16), sub_keys=sub_keys,
             peer_table=jnp.concatenate([expert_down, expert_up], axis=1), norm_final=norm_final)

    bp, tlen = x_prompt.shape[:2]
    bs, ts = x_sample.shape[:2]
    caches = (cache_kv_w128, cache_kv_w512, cache_kv_w2048)

    tail8 = jnp.concatenate([jnp.zeros((bs, SUBLANE - (CONV_W - 1), GDN_CONV_CH), F32), state_conv], axis=1)
    y_s, gdn_s, h3_s, kv_new = _layer(
        x_sample, PAST_LEN + jnp.arange(ts, dtype=jnp.int32), tail8, state_gdn, caches, w, SUBLANE)
    conv_s = jnp.concatenate([state_conv, h3_s[:, :, COL_Q:COL_Q + GDN_CONV_CH]], axis=1)[:, ts:]
    kv_s = [jnp.concatenate([cache, new.astype(cache.dtype)], axis=1)[:, ts:] for cache, new in zip(caches, kv_new)]

    y_p, gdn_p, h3_p, kv_p = _layer(
        x_prompt, jnp.arange(tlen, dtype=jnp.int32),
        jnp.zeros((bp, SUBLANE, GDN_CONV_CH), F32), jnp.zeros((bp, GDN_HEADS, GDN_DK, GDN_DV), F32),
        None, w, CHUNK)
    conv_p = h3_p[:, tlen - (CONV_W - 1):, COL_Q:COL_Q + GDN_CONV_CH]

    return (y_p, y_s, gdn_p.astype(x_prompt.dtype), conv_p, kv_p[0], kv_p[1], kv_p[2],
            gdn_s.astype(state_gdn.dtype), conv_s, kv_s[0], kv_s[1], kv_s[2])
```

```python
import functools
import math

import jax
import jax.numpy as jnp
import numpy as np
from jax import lax
from jax.experimental import pallas as pl
from jax.experimental.pallas import tpu as pltpu

F32 = jnp.float32
BF16 = jnp.bfloat16
HIGHEST = lax.Precision.HIGHEST

LANE = 128
SUBLANE = 8
VMEM_LIMIT = 56 * 1024 * 1024

D_MODEL = 2048
PAST_LEN = 16384
EPS = 1e-6
GDN_HEADS = 8
GDN_DK = 128
GDN_DV = 128
GDN_QK = GDN_HEADS * GDN_DK
GDN_VW = GDN_HEADS * GDN_DV
GDN_CONV_CH = 2 * GDN_QK + GDN_VW
CONV_W = 4
CHUNK = 64
SWA_GROUPS = ((128, 1), (512, 4), (2048, 16))
N_SWA = 3
SWA_HEADS = 4
SWA_DH = 128
SWA_W = SWA_HEADS * SWA_DH
SWA_QKV = N_SWA * SWA_W
ROT_DIM = SWA_DH // 4
ROPE_THETA = 500000.0
PEER_HEADS = 8
PEER_NKEYS = 128
PEER_DKEY = 256
PEER_TOPK = 16
PEER_SEL = PEER_HEADS * PEER_TOPK

COL_GA = 0
COL_GB = D_MODEL
COL_Q = 2 * D_MODEL
COL_K = COL_Q + GDN_QK
COL_V = COL_K + GDN_QK
COL_Z = COL_V + GDN_VW
COL_SQ = COL_Z + GDN_VW
COL_SK = COL_SQ + SWA_QKV
COL_SV = COL_SK + SWA_QKV
N_MAIN = COL_SV + SWA_QKV
IN_AB = 2 * GDN_QK + 2 * GDN_VW
IN_SQ = IN_AB + 2 * GDN_HEADS
IN_GA = IN_SQ + 3 * SWA_QKV
NEG_BIG = -1e30


def _cparams(*sem):
    return pltpu.CompilerParams(dimension_semantics=sem, vmem_limit_bytes=VMEM_LIMIT)


def _sigmoid(x):
    return 1.0 / (1.0 + jnp.exp(-x))


def _hdot(a, b):
    return jnp.dot(a, b, precision=HIGHEST, preferred_element_type=F32)


def _hdot_nt(a, b):
    return lax.dot_general(a, b, (((1,), (1,)), ((), ())), precision=HIGHEST, preferred_element_type=F32)


def _hdot_tn(a, b):
    return lax.dot_general(a, b, (((0,), (0,)), ((), ())), precision=HIGHEST, preferred_element_type=F32)


def _dot_nt(a, b):
    return lax.dot_general(a, b, (((1,), (1,)), ((), ())), preferred_element_type=F32)


def _rmsnorm_kernel(x_ref, w_ref, o_ref):
    x = x_ref[...]
    y = x * lax.rsqrt(jnp.mean(x * x, axis=-1, keepdims=True) + EPS) * w_ref[...]
    o_ref[...] = y.astype(o_ref.dtype)


def _rmsnorm(x, w, tm, out_dtype):
    m, d = x.shape
    return pl.pallas_call(
        _rmsnorm_kernel,
        grid=(m // tm,),
        in_specs=[pl.BlockSpec((tm, d), lambda i: (i, 0)), pl.BlockSpec((1, d), lambda i: (0, 0))],
        out_specs=pl.BlockSpec((tm, d), lambda i: (i, 0)),
        out_shape=jax.ShapeDtypeStruct((m, d), out_dtype),
        compiler_params=_cparams("parallel"),
        name="rmsnorm",
    )(x, w.reshape(1, d))


def _mm_kernel(x_ref, w_ref, o_ref):
    o_ref[...] = jnp.dot(x_ref[...].astype(BF16), w_ref[...], preferred_element_type=F32)


def _matmul(x, w, tm, tn, name):
    m, k = x.shape
    n = w.shape[1]
    return pl.pallas_call(
        _mm_kernel,
        grid=(n // tn, m // tm),
        in_specs=[pl.BlockSpec((tm, k), lambda j, i: (i, 0)), pl.BlockSpec((k, tn), lambda j, i: (0, j))],
        out_specs=pl.BlockSpec((tm, tn), lambda j, i: (i, j)),
        out_shape=jax.ShapeDtypeStruct((m, n), F32),
        compiler_params=_cparams("parallel", "parallel"),
        name=name,
    )(x, w)


def _rope_kernel(q_ref, k_ref, cos_ref, sin_ref, qo_ref, ko_ref):
    cosf = cos_ref[...]
    sinf = sin_ref[...]
    lane = lax.broadcasted_iota(jnp.int32, cosf.shape, 1)
    first = lane < ROT_DIM // 2
    for src, dst in ((q_ref, qo_ref), (k_ref, ko_ref)):
        for hh in range(SWA_HEADS):
            sl = slice(hh * SWA_DH, (hh + 1) * SWA_DH)
            x = src[:, sl]
            partner = jnp.where(first, pltpu.roll(x, SWA_DH - ROT_DIM // 2, axis=1), pltpu.roll(x, ROT_DIM // 2, axis=1))
            dst[:, sl] = x * cosf + partner * sinf


def _rope(h_main, cosf, sinf, tm):
    m = h_main.shape[0]
    nt = cosf.shape[0] // tm
    qb, kb = COL_SQ // SWA_W, COL_SK // SWA_W
    return pl.pallas_call(
        _rope_kernel,
        grid=(m // tm, N_SWA),
        in_specs=[
            pl.BlockSpec((tm, SWA_W), lambda i, g: (i, qb + g)),
            pl.BlockSpec((tm, SWA_W), lambda i, g: (i, kb + g)),
            pl.BlockSpec((tm, SWA_DH), lambda i, g: (i % nt, 0)),
            pl.BlockSpec((tm, SWA_DH), lambda i, g: (i % nt, 0)),
        ],
        out_specs=[pl.BlockSpec((tm, SWA_W), lambda i, g: (i, g)), pl.BlockSpec((tm, SWA_W), lambda i, g: (i, g))],
        out_shape=[jax.ShapeDtypeStruct((m, SWA_QKV), F32), jax.ShapeDtypeStruct((m, SWA_QKV), F32)],
        compiler_params=_cparams("parallel", "parallel"),
        name="rope",
    )(h_main, h_main, cosf, sinf)


def _rope_split_kernel(*refs, dils, tm):
    ins, cos_ref, sin_ref = refs[:3 * N_SWA], refs[3 * N_SWA], refs[3 * N_SWA + 1]
    outs, slab = refs[3 * N_SWA + 2:6 * N_SWA + 2], refs[6 * N_SWA + 2]
    cosf = cos_ref[...]
    sinf = sin_ref[...]
    first = lax.broadcasted_iota(jnp.int32, cosf.shape, 1) < ROT_DIM // 2
    for gi, dil in enumerate(dils):
        for kind in range(3):
            src, dst = ins[3 * gi + kind], outs[3 * gi + kind]
            for hh in range(SWA_HEADS):
                sl = slice(hh * SWA_DH, (hh + 1) * SWA_DH)
                x = src[0, :, sl]
                if kind < 2:
                    partner = jnp.where(first, pltpu.roll(x, SWA_DH - ROT_DIM // 2, axis=1),
                                        pltpu.roll(x, ROT_DIM // 2, axis=1))
                    x = x * cosf + partner * sinf
                if dil == 1:
                    dst[0, 0, :, sl] = x
                else:
                    slab[hh] = x
                    for r in range(dil):
                        dst[0, r, :, sl] = slab[hh, pl.ds(r, tm // dil, stride=dil), :]


def _rope_split(h3, cosf, sinf, tm):
    bn, t, _ = h3.shape
    dils = tuple(d for _, d in SWA_GROUPS)
    in_specs, out_specs, out_shape = [], [], []
    for gi, dil in enumerate(dils):
        for off in (COL_SQ, COL_SK, COL_SV):
            cb = off // SWA_W + gi
            in_specs.append(pl.BlockSpec((1, tm, SWA_W), lambda b, i, cb=cb: (b, i, cb)))
            out_specs.append(pl.BlockSpec((1, dil, tm // dil, SWA_W), lambda b, i: (b, 0, i, 0)))
            out_shape.append(jax.ShapeDtypeStruct((bn, dil, t // dil, SWA_W), F32))
    tbl = pl.BlockSpec((tm, SWA_DH), lambda b, i: (i, 0))
    outs = pl.pallas_call(
        functools.partial(_rope_split_kernel, dils=dils, tm=tm),
        grid=(bn, t // tm),
        in_specs=in_specs + [tbl, tbl],
        out_specs=out_specs,
        out_shape=out_shape,
        scratch_shapes=[pltpu.VMEM((SWA_HEADS, tm, SWA_DH), F32)],
        compiler_params=_cparams("parallel", "parallel"),
        name="rope_split",
    )(*([h3] * (3 * N_SWA)), cosf, sinf)
    return [tuple(outs[3 * gi:3 * gi + 3]) for gi in range(N_SWA)]


def _rope_tables(pos):
    half = ROT_DIM // 2
    inv = ROPE_THETA ** (-jnp.arange(half, dtype=F32) * 2.0 / ROT_DIM)
    ang = pos.astype(F32)[:, None] * inv[None, :]
    cos, sin = jnp.cos(ang), jnp.sin(ang)
    n = pos.shape[0]
    cosf = jnp.concatenate([cos, cos, jnp.ones((n, SWA_DH - ROT_DIM), F32)], axis=1)
    sinf = jnp.concatenate([-sin, sin, jnp.zeros((n, SWA_DH - ROT_DIM), F32)], axis=1)
    return cosf, sinf


def _gdn_kernel(alog_ref, dtb_ref, q_ref, k_ref, v_ref, z_ref, ab_ref, cwq_ref, cwk_ref, cwv_ref,
                tq_ref, tk_ref, tv_ref, s0_ref, nw_ref, o_ref, sout_ref,
                s_scr, tailq, tailk, tailv, *, chunk, t_total, hps):
    c = pl.program_id(2)

    @pl.when(c == 0)
    def _():
        s_scr[...] = s0_ref[0]
        tailq[...] = tq_ref[0]
        tailk[...] = tk_ref[0]
        tailv[...] = tv_ref[0]

    C = chunk
    heads = range(hps)
    sls = [slice(hh * LANE, (hh + 1) * LANE) for hh in heads]
    hidx = [pl.program_id(1) * hps + hh for hh in heads]
    row8 = lax.broadcasted_iota(jnp.int32, (SUBLANE, LANE), 0)

    def conv(src_ref, tail_ref, w_ref, sl):
        raw = src_ref[0, :, sl]
        w = w_ref[:, sl]
        t8 = tail_ref[:, sl]
        y = raw * w[CONV_W - 1:CONV_W, :]
        for s in range(1, CONV_W):
            rolled = pltpu.roll(raw, s, axis=0)
            top = jnp.where(row8 < s, pltpu.roll(t8, s, axis=0), rolled[0:SUBLANE])
            sh = top if C == SUBLANE else jnp.concatenate([top, rolled[SUBLANE:]], axis=0)
            y = y + sh * w[CONV_W - 1 - s:CONV_W - s, :]
        tail_ref[:, sl] = raw[C - SUBLANE:C]
        return y * _sigmoid(y)

    def l2n(x):
        return x * lax.rsqrt(jnp.sum(x * x, axis=-1, keepdims=True) + EPS)

    q = [l2n(conv(q_ref, tailq, cwq_ref, sl)) * (GDN_DK ** -0.5) for sl in sls]
    k = [l2n(conv(k_ref, tailk, cwk_ref, sl)) for sl in sls]
    v = [conv(v_ref, tailv, cwv_ref, sl) for sl in sls]

    lane = lax.broadcasted_iota(jnp.int32, (C, LANE), 1)
    rowc = lax.broadcasted_iota(jnp.int32, (C, 1), 0)
    valid = (c * C + rowc) < t_total
    ab = ab_ref[0]
    beta, g = [], []
    for h in hidx:
        a = jnp.sum(jnp.where(lane == h, ab, 0.0), axis=-1, keepdims=True)
        b = jnp.sum(jnp.where(lane == h + GDN_HEADS, ab, 0.0), axis=-1, keepdims=True)
        sp_in = a + dtb_ref[h]
        softplus = jnp.maximum(sp_in, 0.0) + jnp.log(1.0 + jnp.exp(-jnp.abs(sp_in)))
        beta.append(jnp.where(valid, _sigmoid(b), 0.0))
        g.append(jnp.where(valid, -jnp.exp(jnp.full((C, 1), alog_ref[h], F32)) * softplus, 0.0))

    ri = lax.broadcasted_iota(jnp.int32, (C, C), 0)
    ci = lax.broadcasted_iota(jnp.int32, (C, C), 1)
    causal = ri >= ci
    strict = ri > ci
    tril = causal.astype(F32)
    e0 = (lane == 0).astype(F32)
    eye = (ri == ci).astype(F32)
    gc_b = [_hdot(tril, jnp.broadcast_to(gh, (C, LANE))) for gh in g]
    gc_row = [_hdot_nt(e0, gch) for gch in gc_b]
    decay = [jnp.where(causal, jnp.exp(jnp.where(causal, gch[:, :C] - grh, 0.0)), 0.0)
             for gch, grh in zip(gc_b, gc_row)]
    kb = [kh * bh for kh, bh in zip(k, beta)]
    xpow = [-jnp.where(strict, _hdot_nt(kbh, kh) * dh, 0.0) for kbh, kh, dh in zip(kb, k, decay)]
    tinv = [eye + xh for xh in xpow]
    for _ in range(int(math.log2(C)) - 1):
        xpow = [_hdot(xh, xh) for xh in xpow]
        tinv = [th + _hdot(th, xh) for th, xh in zip(tinv, xpow)]
    eg = [jnp.exp(gch) for gch in gc_b]
    value = [_hdot(th, vh * bh) for th, vh, bh in zip(tinv, v, beta)]
    kcd = [_hdot(th, kbh * egh) for th, kbh, egh in zip(tinv, kb, eg)]
    attn = [jnp.where(causal, _hdot_nt(qh, kh) * dh, 0.0) for qh, kh, dh in zip(q, k, decay)]
    glast = [gch[C - 1:C, :] for gch in gc_b]
    kdec = [kh * jnp.exp(glh - gch) for kh, glh, gch in zip(k, glast, gc_b)]

    s = [s_scr[hh] for hh in heads]
    v_new = [vh - _hdot(kh, sh) for vh, kh, sh in zip(value, kcd, s)]
    o = [_hdot(qh * egh, sh) + _hdot(ah, vnh) for qh, egh, sh, ah, vnh in zip(q, eg, s, attn, v_new)]
    for hh in heads:
        s_scr[hh] = s[hh] * jnp.exp(glast[hh]) + _hdot_tn(kdec[hh], v_new[hh])

    nw = nw_ref[...]
    for hh, sl in zip(heads, sls):
        z = z_ref[0, :, sl]
        on = o[hh] * lax.rsqrt(jnp.mean(o[hh] * o[hh], axis=-1, keepdims=True) + EPS) * nw
        o_ref[0, :, sl] = on * (z * _sigmoid(z))

    @pl.when(c == pl.num_programs(2) - 1)
    def _():
        sout_ref[0] = s_scr[...]


GDN_HEADS_PER_STEP = 4


def _gdn(h_main3, ab3, conv_w, tail8, s0, a_log, dt_bias, gdn_norm, chunk, t_total):
    bn, tp, _ = h_main3.shape
    nc = tp // chunk
    hps = GDN_HEADS_PER_STEP
    w = hps * LANE
    hq, hk, hv, hz = COL_Q // w, COL_K // w, COL_V // w, COL_Z // w
    cq, ck, cv = 0, GDN_QK // w, 2 * GDN_QK // w
    col = lambda off: pl.BlockSpec((1, chunk, w), lambda b, h, c: (b, c, off + h))
    cw = lambda off: pl.BlockSpec((CONV_W, w), lambda b, h, c: (0, off + h))
    tl = lambda off: pl.BlockSpec((1, SUBLANE, w), lambda b, h, c: (b, 0, off + h))
    smem = pl.BlockSpec(memory_space=pltpu.SMEM)
    kern = functools.partial(_gdn_kernel, chunk=chunk, t_total=t_total, hps=hps)
    return pl.pallas_call(
        kern,
        grid=(bn, GDN_HEADS // hps, nc),
        in_specs=[smem, smem, col(hq), col(hk), col(hv), col(hz),
                  pl.BlockSpec((1, chunk, LANE), lambda b, h, c: (b, c, 0)),
                  cw(cq), cw(ck), cw(cv), tl(cq), tl(ck), tl(cv),
                  pl.BlockSpec((1, hps, GDN_DK, GDN_DV), lambda b, h, c: (b, h, 0, 0)),
                  pl.BlockSpec((1, GDN_DV), lambda b, h, c: (0, 0))],
        out_specs=[pl.BlockSpec((1, chunk, w), lambda b, h, c: (b, c, h)),
                   pl.BlockSpec((1, hps, GDN_DK, GDN_DV), lambda b, h, c: (b, h, 0, 0))],
        out_shape=[jax.ShapeDtypeStruct((bn, tp, GDN_VW), F32),
                   jax.ShapeDtypeStruct((bn, GDN_HEADS, GDN_DK, GDN_DV), F32)],
        scratch_shapes=[pltpu.VMEM((hps, GDN_DK, GDN_DV), F32)] + [pltpu.VMEM((SUBLANE, w), F32)] * 3,
        compiler_params=_cparams("parallel", "parallel", "arbitrary"),
        name="gdn",
    )(a_log, dt_bias, h_main3, h_main3, h_main3, h_main3, ab3, conv_w, conv_w, conv_w,
      tail8, tail8, tail8, s0, gdn_norm.reshape(1, GDN_DV))


def _swa_kernel(q_ref, kp_ref, kc_ref, vp_ref, vc_ref, o_ref, lse_ref):
    qi = pl.program_id(2)
    n = q_ref.shape[2]
    ri = lax.broadcasted_iota(jnp.int32, (n, n), 0)
    ci = lax.broadcasted_iota(jnp.int32, (n, n), 1)
    mask_prev = jnp.logical_and(ci >= ri, qi > 0)
    mask_cur = ci <= ri
    scale = SWA_DH ** -0.5
    for hh in range(SWA_HEADS):
        sl = slice(hh * SWA_DH, (hh + 1) * SWA_DH)
        q = q_ref[0, 0, :, sl].astype(BF16)
        sp = jnp.where(mask_prev, _dot_nt(q, kp_ref[0, 0, :, sl].astype(BF16)) * scale, NEG_BIG)
        sc = jnp.where(mask_cur, _dot_nt(q, kc_ref[0, 0, :, sl].astype(BF16)) * scale, NEG_BIG)
        m = jnp.maximum(jnp.max(sp, axis=-1, keepdims=True), jnp.max(sc, axis=-1, keepdims=True))
        pp = jnp.exp(sp - m)
        pc = jnp.exp(sc - m)
        ssum = jnp.sum(pp, axis=-1, keepdims=True) + jnp.sum(pc, axis=-1, keepdims=True)
        acc = (jnp.dot(pp.astype(BF16), vp_ref[0, 0, :, sl].astype(BF16), preferred_element_type=F32)
               + jnp.dot(pc.astype(BF16), vc_ref[0, 0, :, sl].astype(BF16), preferred_element_type=F32))
        o_ref[0, 0, :, sl] = acc / ssum
        lse_ref[0, 0, :, sl] = jnp.broadcast_to(m + jnp.log(ssum), (n, SWA_DH))


def _swa_prompt_group(q, k, v, gi, qblk):
    bn, dil, tl, _ = q.shape
    cur = lambda b, r, i: (b, r, i, 0)
    prev = lambda b, r, i: (b, r, jnp.maximum(i - 1, 0), 0)
    blk = (1, 1, qblk, SWA_W)
    return pl.pallas_call(
        _swa_kernel,
        grid=(bn, dil, tl // qblk),
        in_specs=[pl.BlockSpec(blk, cur), pl.BlockSpec(blk, prev), pl.BlockSpec(blk, cur),
                  pl.BlockSpec(blk, prev), pl.BlockSpec(blk, cur)],
        out_specs=[pl.BlockSpec(blk, cur)] * 2,
        out_shape=[jax.ShapeDtypeStruct(q.shape, F32)] * 2,
        compiler_params=_cparams("parallel", "parallel", "arbitrary"),
        name=f"swa_prompt_g{gi}",
    )(q, k, k, v, v)


def _swa_sample_kernel(q_ref, kn_ref, vn_ref, c0_ref, c1_ref, c2_ref, *out_refs):
    scale = SWA_DH ** -0.5
    for gi, c_ref in enumerate((c0_ref, c1_ref, c2_ref)):
        o_ref, lse_ref = out_refs[2 * gi], out_refs[2 * gi + 1]
        for hh in range(SWA_HEADS):
            sl = slice(gi * SWA_W + hh * SWA_DH, gi * SWA_W + (hh + 1) * SWA_DH)
            q = q_ref[0, :, sl]
            kn = kn_ref[0, :, sl]
            vn = vn_ref[0, :, sl]
            kc = c_ref[0, :, hh * SWA_DH:(hh + 1) * SWA_DH]
            vc = c_ref[0, :, SWA_W + hh * SWA_DH:SWA_W + (hh + 1) * SWA_DH]
            s = jnp.sum(kc * q, axis=-1, keepdims=True) * scale
            sn = jnp.sum(kn * q, axis=-1, keepdims=True) * scale
            m = jnp.maximum(jnp.max(s, axis=0, keepdims=True), sn)
            p = jnp.exp(s - m)
            pn = jnp.exp(sn - m)
            den = jnp.sum(p, axis=0, keepdims=True) + pn
            osl = slice(hh * SWA_DH, (hh + 1) * SWA_DH)
            o_ref[0, :, osl] = (jnp.sum(p * vc, axis=0, keepdims=True) + pn * vn) / den
            lse_ref[0, :, osl] = jnp.broadcast_to(m + jnp.log(den), (1, SWA_DH))


def _swa_sample(q_r, k_r, v_new, caches):
    bn = q_r.shape[0]
    nkeys = SWA_GROUPS[0][0] // SWA_GROUPS[0][1]
    views = []
    for (win, dil), cache in zip(SWA_GROUPS, caches):
        assert cache.shape[1] == win and win // dil == nkeys
        views.append(cache.reshape(bn, win // dil, dil * 2 * SWA_W))
    row = pl.BlockSpec((1, 1, SWA_QKV), lambda b: (b, 0, 0))
    cspec = pl.BlockSpec((1, nkeys, 2 * SWA_W), lambda b: (b, 0, 0))
    ospec = pl.BlockSpec((1, 1, SWA_W), lambda b: (b, 0, 0))
    outs = pl.pallas_call(
        _swa_sample_kernel,
        grid=(bn,),
        in_specs=[row, row, row, cspec, cspec, cspec],
        out_specs=[ospec] * (2 * N_SWA),
        out_shape=[jax.ShapeDtypeStruct((bn, 1, SWA_W), F32)] * (2 * N_SWA),
        compiler_params=_cparams("parallel"),
        name="swa_sample",
    )(q_r.reshape(bn, 1, SWA_QKV), k_r.reshape(bn, 1, SWA_QKV), v_new.reshape(bn, 1, SWA_QKV), *views)
    return [(outs[2 * g].reshape(bn, SWA_W), outs[2 * g + 1].reshape(bn, SWA_W)) for g in range(N_SWA)]


def _mix_kernel(oa_ref, o0_ref, l0_ref, o1_ref, l1_ref, o2_ref, l2_ref, ga_ref, gb_ref, x_ref,
                wa_ref, wb_ref, wo_ref, nw_ref, x1_ref, xn_ref, slab, *, dils, tm):
    def token_major(ref, dil):
        if dil == 1:
            return ref[0, 0]
        cols = []
        for hh in range(SWA_HEADS):
            for r in range(dil):
                slab[hh, pl.ds(r, tm // dil, stride=dil), :] = ref[0, r, :, hh * SWA_DH:(hh + 1) * SWA_DH]
            cols.append(slab[hh])
        return jnp.concatenate(cols, axis=1)

    os = [token_major(r, d) for r, d in zip((o0_ref, o1_ref, o2_ref), dils)]
    ls = [token_major(r, d) for r, d in zip((l0_ref, l1_ref, l2_ref), dils)]
    lmax = jnp.maximum(jnp.maximum(ls[0], ls[1]), ls[2])
    es = [jnp.exp(l - lmax) for l in ls]
    ob = (es[0] * os[0] + es[1] * os[1] + es[2] * os[2]) / (es[0] + es[1] + es[2])
    pa = jnp.dot(oa_ref[0].astype(BF16), wa_ref[...], preferred_element_type=F32)
    pb = jnp.dot(ob.astype(BF16), wb_ref[...], preferred_element_type=F32)
    merged = _sigmoid(ga_ref[0]) * pa + _sigmoid(gb_ref[0]) * pb
    x1 = x_ref[0] + jnp.dot(merged.astype(BF16), wo_ref[...], preferred_element_type=F32)
    x1_ref[0] = x1
    xn_ref[0] = x1 * lax.rsqrt(jnp.mean(x1 * x1, axis=-1, keepdims=True) + EPS) * nw_ref[...]


def _mix(o_a3, parts, dils, h3, x3, wa, wb, wo, norm_ffn, tm):
    bn, t, _ = x3.shape
    rowblk = lambda w, cb=0: pl.BlockSpec((1, tm, w), lambda b, i: (b, i, cb))
    full = lambda a: pl.BlockSpec(a.shape, lambda b, i: (0, 0))
    nw = norm_ffn.reshape(1, D_MODEL)
    part_specs, flat = [], []
    for (o, lse), dil in zip(parts, dils):
        spec = pl.BlockSpec((1, dil, tm // dil, SWA_W), lambda b, i: (b, 0, i, 0))
        part_specs += [spec, spec]
        flat += [o, lse]
    return pl.pallas_call(
        functools.partial(_mix_kernel, dils=dils, tm=tm),
        grid=(bn, t // tm),
        in_specs=[rowblk(GDN_VW)] + part_specs
                 + [rowblk(D_MODEL, COL_GA // D_MODEL), rowblk(D_MODEL, COL_GB // D_MODEL),
                    rowblk(D_MODEL), full(wa), full(wb), full(wo), full(nw)],
        out_specs=[rowblk(D_MODEL), rowblk(D_MODEL)],
        out_shape=[jax.ShapeDtypeStruct((bn, t, D_MODEL), F32)] * 2,
        scratch_shapes=[pltpu.VMEM((SWA_HEADS, tm, SWA_DH), F32)],
        compiler_params=_cparams("parallel", "parallel"),
        name="mix_out",
    )(o_a3, *flat, h3, h3, x3, wa, wb, wo, nw)


def _topk_rows(vals, k, payload=None):
    n, t = vals.shape
    idx = lax.broadcasted_iota(jnp.int32, (n, t), 0)
    out_v, out_i = [], []
    for _ in range(k):
        m = jnp.max(vals, axis=0, keepdims=True)
        am = jnp.min(jnp.where(vals == m, idx, n), axis=0, keepdims=True)
        hit = idx == am
        out_v.append(m)
        out_i.append(am if payload is None else jnp.sum(jnp.where(hit, payload, 0), axis=0, keepdims=True))
        vals = jnp.where(hit, -jnp.inf, vals)
    return jnp.concatenate(out_v, axis=0), jnp.concatenate(out_i, axis=0)


def _route_kernel(q_ref, keys_ref, eid_ref, gate_ref):
    half = PEER_DKEY // 2
    eids, gates = [], []
    for h in range(PEER_HEADS):
        tops = []
        for p in range(2):
            qs = q_ref[:, (2 * h + p) * half:(2 * h + p + 1) * half].astype(BF16)
            st = _dot_nt(keys_ref[h, p].astype(BF16), qs)
            tops.append(_topk_rows(st, PEER_TOPK))
        (v1, i1), (v2, i2) = tops
        cand = jnp.concatenate([v1[i:i + 1] + v2 for i in range(PEER_TOPK)], axis=0)
        ecand = jnp.concatenate([i1[i:i + 1] * PEER_NKEYS + i2 for i in range(PEER_TOPK)], axis=0)
        sc, e = _topk_rows(cand, PEER_TOPK, payload=ecand)
        ex = jnp.exp(sc - sc[0:1])
        gates.append(ex / jnp.sum(ex, axis=0, keepdims=True))
        eids.append(e)
    eid_ref[...] = jnp.concatenate(eids, axis=0).T
    gate_ref[...] = jnp.concatenate(gates, axis=0)


def _route(qp, sub_keys, tt):
    m = qp.shape[0]
    return pl.pallas_call(
        _route_kernel,
        grid=(m // tt,),
        in_specs=[pl.BlockSpec((tt, PEER_HEADS * PEER_DKEY), lambda i: (i, 0)),
                  pl.BlockSpec(sub_keys.shape, lambda i: (0, 0, 0, 0))],
        out_specs=[pl.BlockSpec((tt, PEER_SEL), lambda i: (i, 0)), pl.BlockSpec((PEER_SEL, tt), lambda i: (0, i))],
        out_shape=[jax.ShapeDtypeStruct((m, PEER_SEL), jnp.int32), jax.ShapeDtypeStruct((PEER_SEL, m), F32)],
        compiler_params=_cparams("parallel"),
        name="peer_route",
    )(qp, sub_keys)


PEER_ISSUE_UNROLL = 8


PEER_CHUNKS = D_MODEL // LANE


def _pack_peer_table(expert_down, expert_up):
    hi = lax.bitcast_convert_type(expert_down.astype(BF16), jnp.uint16).astype(jnp.uint32) << 16
    lo = lax.bitcast_convert_type(expert_up.astype(BF16), jnp.uint16).astype(jnp.uint32)
    return (hi | lo).reshape(expert_down.shape[0], PEER_CHUNKS, LANE)


def _peer_kernel(eid_ref, gate_ref, xn_ref, x1_ref, nw_ref, tbl_hbm, y_ref, tbuf, sem, *, n_tok):
    def issue(t, slot, rows):
        for j in rows:
            pltpu.make_async_copy(tbl_hbm.at[eid_ref[t, j]], tbuf.at[slot, :, j, :], sem.at[slot]).start(priority=j % 2)

    def wait(slot):
        pltpu.make_async_copy(tbuf.at[slot], tbuf.at[slot], sem.at[slot]).wait()

    lane = lax.broadcasted_iota(jnp.int32, (PEER_SEL, gate_ref.shape[1]), 1)
    hi_mask = jnp.uint32(0xFFFF0000)
    per_chunk = PEER_SEL // (2 * PEER_CHUNKS)

    def compute(t, slot, t_next):
        def issue_part(i):
            if t_next is not None:
                issue(t_next, 1 - slot, range(i * per_chunk, (i + 1) * per_chunk))

        x = xn_ref[pl.ds(t, 1), :]
        acc = jnp.zeros((PEER_SEL, LANE), F32)
        for c in range(PEER_CHUNKS):
            down = lax.bitcast_convert_type(tbuf[slot, c] & hi_mask, F32)
            acc = acc + down * x[:, c * LANE:(c + 1) * LANE]
            issue_part(c)
        act = jnp.sum(acc, axis=-1, keepdims=True)
        gate = jnp.sum(jnp.where(lane == t, gate_ref[...], 0.0), axis=-1, keepdims=True)
        gelu = 0.5 * act * (1.0 + lax.erf(act * (2.0 ** -0.5)))
        w = gate * gelu
        outs = []
        for c in range(PEER_CHUNKS):
            up = lax.bitcast_convert_type(tbuf[slot, c] << 16, F32)
            outs.append(jnp.sum(up * w, axis=0, keepdims=True))
            issue_part(PEER_CHUNKS + c)
        y_ref[pl.ds(t, 1), :] = x1_ref[pl.ds(t, 1), :] + jnp.concatenate(outs, axis=1)

    assert n_tok % 2 == 0
    issue(0, 0, range(PEER_SEL))

    def pair(p, carry):
        t = 2 * p
        wait(0)
        compute(t, 0, t + 1)
        wait(1)
        compute(t + 1, 1, t + 2)
        return carry

    lax.fori_loop(0, n_tok // 2 - 1, pair, 0)
    wait(0)
    compute(n_tok - 2, 0, n_tok - 1)
    wait(1)
    compute(n_tok - 1, 1, None)

    x2 = y_ref[...]
    y_ref[...] = x2 * lax.rsqrt(jnp.mean(x2 * x2, axis=-1, keepdims=True) + EPS) * nw_ref[...]


def _peer(eid, gates_t, xn2, x1, norm_final, table, tt, n_tok):
    m = xn2.shape[0]
    kern = functools.partial(_peer_kernel, n_tok=n_tok)
    return pl.pallas_call(
        kern,
        grid=(m // n_tok,),
        in_specs=[pl.BlockSpec((tt, PEER_SEL), lambda i: (i, 0), memory_space=pltpu.SMEM),
                  pl.BlockSpec((PEER_SEL, tt), lambda i: (0, i)),
                  pl.BlockSpec((n_tok, D_MODEL), lambda i: (i, 0)),
                  pl.BlockSpec((n_tok, D_MODEL), lambda i: (i, 0)),
                  pl.BlockSpec((1, D_MODEL), lambda i: (0, 0)),
                  pl.BlockSpec(memory_space=pl.ANY)],
        out_specs=pl.BlockSpec((n_tok, D_MODEL), lambda i: (i, 0)),
        out_shape=jax.ShapeDtypeStruct((m, D_MODEL), F32),
        scratch_shapes=[pltpu.VMEM((2, PEER_CHUNKS, PEER_SEL, LANE), jnp.uint32), pltpu.SemaphoreType.DMA((2,))],
        compiler_params=_cparams("arbitrary"),
        name="peer_experts",
    )(eid, gates_t, xn2, x1, norm_final.reshape(1, D_MODEL), table)


def _pick(m, candidates):
    for c in candidates:
        if m % c == 0:
            return c
    return m


def _layer(x, pos, tail8, s0, caches, w, chunk):
    bn, t, _ = x.shape
    m = bn * t
    x2d = x.reshape(m, D_MODEL)
    tm_big = _pick(m, (1024, 512, 256, 128, 64, 32))

    xn = _rmsnorm(x2d, w["norm_mix"], _pick(m, (512, 256, 128, 64, 32)), BF16)
    h_main = _matmul(xn, w["w_main"], tm_big, 640, "proj_in")
    h_ab = _matmul(xn, w["w_ab"], tm_big, LANE, "proj_ab")

    tp = -(-t // chunk) * chunk
    h3 = h_main.reshape(bn, t, N_MAIN)
    ab3 = h_ab.reshape(bn, t, LANE)
    h3p = h3 if tp == t else jnp.pad(h3, ((0, 0), (0, tp - t), (0, 0)))
    ab3p = ab3 if tp == t else jnp.pad(ab3, ((0, 0), (0, tp - t), (0, 0)))
    o_a, s_new = _gdn(h3p, ab3p, w["conv_w"], tail8, s0, w["a_log"], w["dt_bias"], w["gdn_norm"], chunk, t)
    o_a = o_a[:, :t]

    cosf, sinf = _rope_tables(pos)
    kv_new = []
    if caches is None:
        qkv = _rope_split(h3, cosf, sinf, 256)
        parts = [_swa_prompt_group(q, k, v, gi, 128) for gi, (q, k, v) in enumerate(qkv)]
        dils = tuple(d for _, d in SWA_GROUPS)
        for (win, dil), (_, k, v) in zip(SWA_GROUPS, qkv):
            keep = min(win, t)
            tok = lambda a: (a[:, :, (t - keep) // dil:].transpose(0, 2, 1, 3)
                             .reshape(bn, keep, SWA_HEADS, SWA_DH))
            kv_new.append(jnp.stack([tok(k), tok(v)], axis=2))
        mix_shape = (bn, t)
    else:
        cosf, sinf = jnp.tile(cosf, (bn, 1)), jnp.tile(sinf, (bn, 1))
        q_r, k_r = _rope(h_main, cosf, sinf, _pick(m, (512, 256, 128, 64, 32)))
        v_new = h_main[:, COL_SV:COL_SV + SWA_QKV]
        parts = [(o.reshape(1, 1, m, SWA_W), lse.reshape(1, 1, m, SWA_W))
                 for o, lse in _swa_sample(q_r, k_r, v_new, caches)]
        dils = (1,) * N_SWA
        for gi in range(N_SWA):
            kk = k_r[:, gi * SWA_W:(gi + 1) * SWA_W].reshape(bn, t, SWA_HEADS, SWA_DH)
            vv = v_new[:, gi * SWA_W:(gi + 1) * SWA_W].reshape(bn, t, SWA_HEADS, SWA_DH)
            kv_new.append(jnp.stack([kk, vv], axis=2))
        mix_shape = (1, m)

    mb, mt = mix_shape
    x1, xn2 = _mix(o_a.reshape(mb, mt, GDN_VW), parts, dils, h_main.reshape(mb, mt, N_MAIN),
                   x.reshape(mb, mt, D_MODEL), w["w_br_a"], w["w_br_b"], w["w_out"], w["norm_ffn"],
                   _pick(mt, (256, 128, 64, 32)))
    x1, xn2 = x1.reshape(m, D_MODEL), xn2.reshape(m, D_MODEL)

    mp = -(-m // LANE) * LANE
    xq = xn2 if mp == m else jnp.pad(xn2, ((0, mp - m), (0, 0)))
    qp = _matmul(xq, w["w_query"], _pick(mp, (1024, 512, 256, 128)), 512, "peer_query")
    eid, gates_t = _route(qp, w["sub_keys"], LANE)
    n_tok = min(m, LANE)
    y = _peer(eid, gates_t, xn2, x1, w["norm_final"], w["peer_table"], LANE, n_tok)
    return y.reshape(bn, t, D_MODEL), s_new, h3, kv_new


def kernel(x_prompt, x_sample, state_gdn, state_conv, cache_kv_w128, cache_kv_w512, cache_kv_w2048,
           norm_mix, w_in, conv_w, a_log, dt_bias, gdn_norm, w_br_a, w_br_b, w_out, norm_ffn,
           w_query, sub_keys, expert_down, expert_up, norm_final):
    w_main = jnp.concatenate([w_in[:, IN_GA:], w_in[:, :IN_AB], w_in[:, IN_SQ:IN_GA]], axis=1).astype(BF16)
    w_ab = jnp.pad(w_in[:, IN_AB:IN_SQ], ((0, 0), (0, LANE - 2 * GDN_HEADS))).astype(BF16)
    w = dict(norm_mix=norm_mix, w_main=w_main, w_ab=w_ab, conv_w=conv_w, a_log=a_log, dt_bias=dt_bias,
             gdn_norm=gdn_norm, w_br_a=w_br_a.astype(BF16), w_br_b=w_br_b.astype(BF16), w_out=w_out.astype(BF16),
             norm_ffn=norm_ffn, w_query=w_query.astype(BF16), sub_keys=sub_keys,
             peer_table=_pack_peer_table(expert_down, expert_up), norm_final=norm_final)

    bp, tlen = x_prompt.shape[:2]
    bs, ts = x_sample.shape[:2]
    caches = (cache_kv_w128, cache_kv_w512, cache_kv_w2048)

    tail8 = jnp.concatenate([jnp.zeros((bs, SUBLANE - (CONV_W - 1), GDN_CONV_CH), F32), state_conv], axis=1)
    y_s, gdn_s, h3_s, kv_new = _layer(
        x_sample, PAST_LEN + jnp.arange(ts, dtype=jnp.int32), tail8, state_gdn, caches, w, SUBLANE)
    conv_s = jnp.concatenate([state_conv, h3_s[:, :, COL_Q:COL_Q + GDN_CONV_CH]], axis=1)[:, ts:]
    kv_s = [jnp.concatenate([cache, new.astype(cache.dtype)], axis=1)[:, ts:] for cache, new in zip(caches, kv_new)]

    y_p, gdn_p, h3_p, kv_p = _layer(
        x_prompt, jnp.arange(tlen, dtype=jnp.int32),
        jnp.zeros((bp, SUBLANE, GDN_CONV_CH), F32), jnp.zeros((bp, GDN_HEADS, GDN_DK, GDN_DV), F32),
        None, w, CHUNK)
    conv_p = h3_p[:, tlen - (CONV_W - 1):, COL_Q:COL_Q + GDN_CONV_CH]

    return (y_p, y_s, gdn_p.astype(x_prompt.dtype), conv_p, kv_p[0], kv_p[1], kv_p[2],
            gdn_s.astype(state_gdn.dtype), conv_s, kv_s[0], kv_s[1], kv_s[2])
```

```python
import functools
import math

import jax
import jax.numpy as jnp
import numpy as np
from jax import lax
from jax.experimental import pallas as pl
from jax.experimental.pallas import tpu as pltpu

F32 = jnp.float32
BF16 = jnp.bfloat16
HIGHEST = lax.Precision.HIGHEST

LANE = 128
SUBLANE = 8
VMEM_LIMIT = 56 * 1024 * 1024

D_MODEL = 2048
PAST_LEN = 16384
EPS = 1e-6
GDN_HEADS = 8
GDN_DK = 128
GDN_DV = 128
GDN_QK = GDN_HEADS * GDN_DK
GDN_VW = GDN_HEADS * GDN_DV
GDN_CONV_CH = 2 * GDN_QK + GDN_VW
CONV_W = 4
CHUNK = 64
SWA_GROUPS = ((128, 1), (512, 4), (2048, 16))
N_SWA = 3
SWA_HEADS = 4
SWA_DH = 128
SWA_W = SWA_HEADS * SWA_DH
SWA_QKV = N_SWA * SWA_W
ROT_DIM = SWA_DH // 4
ROPE_THETA = 500000.0
PEER_HEADS = 8
PEER_NKEYS = 128
PEER_DKEY = 256
PEER_TOPK = 16
PEER_SEL = PEER_HEADS * PEER_TOPK

COL_GA = 0
COL_GB = D_MODEL
COL_Q = 2 * D_MODEL
COL_K = COL_Q + GDN_QK
COL_V = COL_K + GDN_QK
COL_Z = COL_V + GDN_VW
COL_SQ = COL_Z + GDN_VW
COL_SK = COL_SQ + SWA_QKV
COL_SV = COL_SK + SWA_QKV
N_MAIN = COL_SV + SWA_QKV
IN_AB = 2 * GDN_QK + 2 * GDN_VW
IN_SQ = IN_AB + 2 * GDN_HEADS
IN_GA = IN_SQ + 3 * SWA_QKV
NEG_BIG = -1e30


def _cparams(*sem):
    return pltpu.CompilerParams(dimension_semantics=sem, vmem_limit_bytes=VMEM_LIMIT)


def _sigmoid(x):
    return 1.0 / (1.0 + jnp.exp(-x))


def _hdot(a, b):
    return jnp.dot(a, b, precision=HIGHEST, preferred_element_type=F32)


def _hdot_nt(a, b):
    return lax.dot_general(a, b, (((1,), (1,)), ((), ())), precision=HIGHEST, preferred_element_type=F32)


def _hdot_tn(a, b):
    return lax.dot_general(a, b, (((0,), (0,)), ((), ())), precision=HIGHEST, preferred_element_type=F32)


def _dot_nt(a, b):
    return lax.dot_general(a, b, (((1,), (1,)), ((), ())), preferred_element_type=F32)


def _rmsnorm_kernel(x_ref, w_ref, o_ref):
    x = x_ref[...]
    y = x * lax.rsqrt(jnp.mean(x * x, axis=-1, keepdims=True) + EPS) * w_ref[...]
    o_ref[...] = y.astype(o_ref.dtype)


def _rmsnorm(x, w, tm, out_dtype):
    m, d = x.shape
    return pl.pallas_call(
        _rmsnorm_kernel,
        grid=(m // tm,),
        in_specs=[pl.BlockSpec((tm, d), lambda i: (i, 0)), pl.BlockSpec((1, d), lambda i: (0, 0))],
        out_specs=pl.BlockSpec((tm, d), lambda i: (i, 0)),
        out_shape=jax.ShapeDtypeStruct((m, d), out_dtype),
        compiler_params=_cparams("parallel"),
        name="rmsnorm",
    )(x, w.reshape(1, d))


def _mm_kernel(x_ref, w_ref, o_ref):
    o_ref[...] = jnp.dot(x_ref[...].astype(BF16), w_ref[...], preferred_element_type=F32)


def _matmul(x, w, tm, tn, name):
    m, k = x.shape
    n = w.shape[1]
    return pl.pallas_call(
        _mm_kernel,
        grid=(n // tn, m // tm),
        in_specs=[pl.BlockSpec((tm, k), lambda j, i: (i, 0)), pl.BlockSpec((k, tn), lambda j, i: (0, j))],
        out_specs=pl.BlockSpec((tm, tn), lambda j, i: (i, j)),
        out_shape=jax.ShapeDtypeStruct((m, n), F32),
        compiler_params=_cparams("parallel", "parallel"),
        name=name,
    )(x, w)


def _rope_kernel(q_ref, k_ref, cos_ref, sin_ref, qo_ref, ko_ref):
    cosf = cos_ref[...]
    sinf = sin_ref[...]
    lane = lax.broadcasted_iota(jnp.int32, cosf.shape, 1)
    first = lane < ROT_DIM // 2
    for src, dst in ((q_ref, qo_ref), (k_ref, ko_ref)):
        for hh in range(SWA_HEADS):
            sl = slice(hh * SWA_DH, (hh + 1) * SWA_DH)
            x = src[:, sl]
            partner = jnp.where(first, pltpu.roll(x, SWA_DH - ROT_DIM // 2, axis=1), pltpu.roll(x, ROT_DIM // 2, axis=1))
            dst[:, sl] = x * cosf + partner * sinf


def _rope(h_main, cosf, sinf, tm):
    m = h_main.shape[0]
    nt = cosf.shape[0] // tm
    qb, kb = COL_SQ // SWA_W, COL_SK // SWA_W
    return pl.pallas_call(
        _rope_kernel,
        grid=(m // tm, N_SWA),
        in_specs=[
            pl.BlockSpec((tm, SWA_W), lambda i, g: (i, qb + g)),
            pl.BlockSpec((tm, SWA_W), lambda i, g: (i, kb + g)),
            pl.BlockSpec((tm, SWA_DH), lambda i, g: (i % nt, 0)),
            pl.BlockSpec((tm, SWA_DH), lambda i, g: (i % nt, 0)),
        ],
        out_specs=[pl.BlockSpec((tm, SWA_W), lambda i, g: (i, g)), pl.BlockSpec((tm, SWA_W), lambda i, g: (i, g))],
        out_shape=[jax.ShapeDtypeStruct((m, SWA_QKV), F32), jax.ShapeDtypeStruct((m, SWA_QKV), F32)],
        compiler_params=_cparams("parallel", "parallel"),
        name="rope",
    )(h_main, h_main, cosf, sinf)


def _rope_split_kernel(*refs, dils, tm):
    ins, cos_ref, sin_ref = refs[:3 * N_SWA], refs[3 * N_SWA], refs[3 * N_SWA + 1]
    outs, slab = refs[3 * N_SWA + 2:6 * N_SWA + 2], refs[6 * N_SWA + 2]
    cosf = cos_ref[...]
    sinf = sin_ref[...]
    first = lax.broadcasted_iota(jnp.int32, cosf.shape, 1) < ROT_DIM // 2
    for gi, dil in enumerate(dils):
        for kind in range(3):
            src, dst = ins[3 * gi + kind], outs[3 * gi + kind]
            for hh in range(SWA_HEADS):
                sl = slice(hh * SWA_DH, (hh + 1) * SWA_DH)
                x = src[0, :, sl]
                if kind < 2:
                    partner = jnp.where(first, pltpu.roll(x, SWA_DH - ROT_DIM // 2, axis=1),
                                        pltpu.roll(x, ROT_DIM // 2, axis=1))
                    x = x * cosf + partner * sinf
                if dil == 1:
                    dst[0, 0, :, sl] = x
                else:
                    slab[hh] = x
                    for r in range(dil):
                        dst[0, r, :, sl] = slab[hh, pl.ds(r, tm // dil, stride=dil), :]


def _rope_split(h3, cosf, sinf, tm):
    bn, t, _ = h3.shape
    dils = tuple(d for _, d in SWA_GROUPS)
    in_specs, out_specs, out_shape = [], [], []
    for gi, dil in enumerate(dils):
        for off in (COL_SQ, COL_SK, COL_SV):
            cb = off // SWA_W + gi
            in_specs.append(pl.BlockSpec((1, tm, SWA_W), lambda b, i, cb=cb: (b, i, cb)))
            out_specs.append(pl.BlockSpec((1, dil, tm // dil, SWA_W), lambda b, i: (b, 0, i, 0)))
            out_shape.append(jax.ShapeDtypeStruct((bn, dil, t // dil, SWA_W), F32))
    tbl = pl.BlockSpec((tm, SWA_DH), lambda b, i: (i, 0))
    outs = pl.pallas_call(
        functools.partial(_rope_split_kernel, dils=dils, tm=tm),
        grid=(bn, t // tm),
        in_specs=in_specs + [tbl, tbl],
        out_specs=out_specs,
        out_shape=out_shape,
        scratch_shapes=[pltpu.VMEM((SWA_HEADS, tm, SWA_DH), F32)],
        compiler_params=_cparams("parallel", "parallel"),
        name="rope_split",
    )(*([h3] * (3 * N_SWA)), cosf, sinf)
    return [tuple(outs[3 * gi:3 * gi + 3]) for gi in range(N_SWA)]


def _rope_tables(pos):
    half = ROT_DIM // 2
    inv = ROPE_THETA ** (-jnp.arange(half, dtype=F32) * 2.0 / ROT_DIM)
    ang = pos.astype(F32)[:, None] * inv[None, :]
    cos, sin = jnp.cos(ang), jnp.sin(ang)
    n = pos.shape[0]
    cosf = jnp.concatenate([cos, cos, jnp.ones((n, SWA_DH - ROT_DIM), F32)], axis=1)
    sinf = jnp.concatenate([-sin, sin, jnp.zeros((n, SWA_DH - ROT_DIM), F32)], axis=1)
    return cosf, sinf


def _gdn_kernel(alog_ref, dtb_ref, q_ref, k_ref, v_ref, z_ref, ab_ref, cwq_ref, cwk_ref, cwv_ref,
                tq_ref, tk_ref, tv_ref, s0_ref, nw_ref, o_ref, sout_ref,
                s_scr, tailq, tailk, tailv, *, chunk, t_total, hps):
    c = pl.program_id(2)

    @pl.when(c == 0)
    def _():
        s_scr[...] = s0_ref[0]
        tailq[...] = tq_ref[0]
        tailk[...] = tk_ref[0]
        tailv[...] = tv_ref[0]

    C = chunk
    heads = range(hps)
    sls = [slice(hh * LANE, (hh + 1) * LANE) for hh in heads]
    hidx = [pl.program_id(1) * hps + hh for hh in heads]
    row8 = lax.broadcasted_iota(jnp.int32, (SUBLANE, LANE), 0)

    def conv(src_ref, tail_ref, w_ref, sl):
        raw = src_ref[0, :, sl]
        w = w_ref[:, sl]
        t8 = tail_ref[:, sl]
        y = raw * w[CONV_W - 1:CONV_W, :]
        for s in range(1, CONV_W):
            rolled = pltpu.roll(raw, s, axis=0)
            top = jnp.where(row8 < s, pltpu.roll(t8, s, axis=0), rolled[0:SUBLANE])
            sh = top if C == SUBLANE else jnp.concatenate([top, rolled[SUBLANE:]], axis=0)
            y = y + sh * w[CONV_W - 1 - s:CONV_W - s, :]
        tail_ref[:, sl] = raw[C - SUBLANE:C]
        return y * _sigmoid(y)

    def l2n(x):
        return x * lax.rsqrt(jnp.sum(x * x, axis=-1, keepdims=True) + EPS)

    q = [l2n(conv(q_ref, tailq, cwq_ref, sl)) * (GDN_DK ** -0.5) for sl in sls]
    k = [l2n(conv(k_ref, tailk, cwk_ref, sl)) for sl in sls]
    v = [conv(v_ref, tailv, cwv_ref, sl) for sl in sls]

    lane = lax.broadcasted_iota(jnp.int32, (C, LANE), 1)
    rowc = lax.broadcasted_iota(jnp.int32, (C, 1), 0)
    valid = (c * C + rowc) < t_total
    ab = ab_ref[0]
    beta, g = [], []
    for h in hidx:
        a = jnp.sum(jnp.where(lane == h, ab, 0.0), axis=-1, keepdims=True)
        b = jnp.sum(jnp.where(lane == h + GDN_HEADS, ab, 0.0), axis=-1, keepdims=True)
        sp_in = a + dtb_ref[h]
        softplus = jnp.maximum(sp_in, 0.0) + jnp.log(1.0 + jnp.exp(-jnp.abs(sp_in)))
        beta.append(jnp.where(valid, _sigmoid(b), 0.0))
        g.append(jnp.where(valid, -jnp.exp(jnp.full((C, 1), alog_ref[h], F32)) * softplus, 0.0))

    ri = lax.broadcasted_iota(jnp.int32, (C, C), 0)
    ci = lax.broadcasted_iota(jnp.int32, (C, C), 1)
    causal = ri >= ci
    strict = ri > ci
    tril = causal.astype(F32)
    e0 = (lane == 0).astype(F32)
    eye = (ri == ci).astype(F32)
    gc_b = [_hdot(tril, jnp.broadcast_to(gh, (C, LANE))) for gh in g]
    gc_row = [_hdot_nt(e0, gch) for gch in gc_b]
    decay = [jnp.where(causal, jnp.exp(jnp.where(causal, gch[:, :C] - grh, 0.0)), 0.0)
             for gch, grh in zip(gc_b, gc_row)]
    kb = [kh * bh for kh, bh in zip(k, beta)]
    xpow = [-jnp.where(strict, _hdot_nt(kbh, kh) * dh, 0.0) for kbh, kh, dh in zip(kb, k, decay)]
    tinv = [eye + xh for xh in xpow]
    for _ in range(int(math.log2(C)) - 1):
        xpow = [_hdot(xh, xh) for xh in xpow]
        tinv = [th + _hdot(th, xh) for th, xh in zip(tinv, xpow)]
    eg = [jnp.exp(gch) for gch in gc_b]
    value = [_hdot(th, vh * bh) for th, vh, bh in zip(tinv, v, beta)]
    kcd = [_hdot(th, kbh * egh) for th, kbh, egh in zip(tinv, kb, eg)]
    attn = [jnp.where(causal, _hdot_nt(qh, kh) * dh, 0.0) for qh, kh, dh in zip(q, k, decay)]
    glast = [gch[C - 1:C, :] for gch in gc_b]
    kdec = [kh * jnp.exp(glh - gch) for kh, glh, gch in zip(k, glast, gc_b)]

    s = [s_scr[hh] for hh in heads]
    v_new = [vh - _hdot(kh, sh) for vh, kh, sh in zip(value, kcd, s)]
    o = [_hdot(qh * egh, sh) + _hdot(ah, vnh) for qh, egh, sh, ah, vnh in zip(q, eg, s, attn, v_new)]
    for hh in heads:
        s_scr[hh] = s[hh] * jnp.exp(glast[hh]) + _hdot_tn(kdec[hh], v_new[hh])

    nw = nw_ref[...]
    for hh, sl in zip(heads, sls):
        z = z_ref[0, :, sl]
        on = o[hh] * lax.rsqrt(jnp.mean(o[hh] * o[hh], axis=-1, keepdims=True) + EPS) * nw
        o_ref[0, :, sl] = on * (z * _sigmoid(z))

    @pl.when(c == pl.num_programs(2) - 1)
    def _():
        sout_ref[0] = s_scr[...]


GDN_HEADS_PER_STEP = 8


def _gdn(h_main3, ab3, conv_w, tail8, s0, a_log, dt_bias, gdn_norm, chunk, t_total):
    bn, tp, _ = h_main3.shape
    nc = tp // chunk
    hps = GDN_HEADS_PER_STEP
    w = hps * LANE
    hq, hk, hv, hz = COL_Q // w, COL_K // w, COL_V // w, COL_Z // w
    cq, ck, cv = 0, GDN_QK // w, 2 * GDN_QK // w
    col = lambda off: pl.BlockSpec((1, chunk, w), lambda b, h, c: (b, c, off + h))
    cw = lambda off: pl.BlockSpec((CONV_W, w), lambda b, h, c: (0, off + h))
    tl = lambda off: pl.BlockSpec((1, SUBLANE, w), lambda b, h, c: (b, 0, off + h))
    smem = pl.BlockSpec(memory_space=pltpu.SMEM)
    kern = functools.partial(_gdn_kernel, chunk=chunk, t_total=t_total, hps=hps)
    return pl.pallas_call(
        kern,
        grid=(bn, GDN_HEADS // hps, nc),
        in_specs=[smem, smem, col(hq), col(hk), col(hv), col(hz),
                  pl.BlockSpec((1, chunk, LANE), lambda b, h, c: (b, c, 0)),
                  cw(cq), cw(ck), cw(cv), tl(cq), tl(ck), tl(cv),
                  pl.BlockSpec((1, hps, GDN_DK, GDN_DV), lambda b, h, c: (b, h, 0, 0)),
                  pl.BlockSpec((1, GDN_DV), lambda b, h, c: (0, 0))],
        out_specs=[pl.BlockSpec((1, chunk, w), lambda b, h, c: (b, c, h)),
                   pl.BlockSpec((1, hps, GDN_DK, GDN_DV), lambda b, h, c: (b, h, 0, 0))],
        out_shape=[jax.ShapeDtypeStruct((bn, tp, GDN_VW), F32),
                   jax.ShapeDtypeStruct((bn, GDN_HEADS, GDN_DK, GDN_DV), F32)],
        scratch_shapes=[pltpu.VMEM((hps, GDN_DK, GDN_DV), F32)] + [pltpu.VMEM((SUBLANE, w), F32)] * 3,
        compiler_params=_cparams("parallel", "parallel", "arbitrary"),
        name="gdn",
    )(a_log, dt_bias, h_main3, h_main3, h_main3, h_main3, ab3, conv_w, conv_w, conv_w,
      tail8, tail8, tail8, s0, gdn_norm.reshape(1, GDN_DV))


def _swa_kernel(q_ref, kp_ref, kc_ref, vp_ref, vc_ref, o_ref, lse_ref):
    qi = pl.program_id(2)
    n = q_ref.shape[2]
    ri = lax.broadcasted_iota(jnp.int32, (n, n), 0)
    ci = lax.broadcasted_iota(jnp.int32, (n, n), 1)
    mask_prev = jnp.logical_and(ci >= ri, qi > 0)
    mask_cur = ci <= ri
    scale = SWA_DH ** -0.5
    for hh in range(SWA_HEADS):
        sl = slice(hh * SWA_DH, (hh + 1) * SWA_DH)
        q = q_ref[0, 0, :, sl].astype(BF16)
        sp = jnp.where(mask_prev, _dot_nt(q, kp_ref[0, 0, :, sl].astype(BF16)) * scale, NEG_BIG)
        sc = jnp.where(mask_cur, _dot_nt(q, kc_ref[0, 0, :, sl].astype(BF16)) * scale, NEG_BIG)
        m = jnp.maximum(jnp.max(sp, axis=-1, keepdims=True), jnp.max(sc, axis=-1, keepdims=True))
        pp = jnp.exp(sp - m)
        pc = jnp.exp(sc - m)
        ssum = jnp.sum(pp, axis=-1, keepdims=True) + jnp.sum(pc, axis=-1, keepdims=True)
        acc = (jnp.dot(pp.astype(BF16), vp_ref[0, 0, :, sl].astype(BF16), preferred_element_type=F32)
               + jnp.dot(pc.astype(BF16), vc_ref[0, 0, :, sl].astype(BF16), preferred_element_type=F32))
        o_ref[0, 0, :, sl] = acc / ssum
        lse_ref[0, 0, :, sl] = jnp.broadcast_to(m + jnp.log(ssum), (n, SWA_DH))


def _swa_prompt_group(q, k, v, gi, qblk):
    bn, dil, tl, _ = q.shape
    cur = lambda b, r, i: (b, r, i, 0)
    prev = lambda b, r, i: (b, r, jnp.maximum(i - 1, 0), 0)
    blk = (1, 1, qblk, SWA_W)
    return pl.pallas_call(
        _swa_kernel,
        grid=(bn, dil, tl // qblk),
        in_specs=[pl.BlockSpec(blk, cur), pl.BlockSpec(blk, prev), pl.BlockSpec(blk, cur),
                  pl.BlockSpec(blk, prev), pl.BlockSpec(blk, cur)],
        out_specs=[pl.BlockSpec(blk, cur)] * 2,
        out_shape=[jax.ShapeDtypeStruct(q.shape, F32)] * 2,
        compiler_params=_cparams("parallel", "parallel", "arbitrary"),
        name=f"swa_prompt_g{gi}",
    )(q, k, k, v, v)


def _swa_sample_kernel(q_ref, kn_ref, vn_ref, c0_ref, c1_ref, c2_ref, *out_refs):
    scale = SWA_DH ** -0.5
    for gi, c_ref in enumerate((c0_ref, c1_ref, c2_ref)):
        o_ref, lse_ref = out_refs[2 * gi], out_refs[2 * gi + 1]
        for hh in range(SWA_HEADS):
            sl = slice(gi * SWA_W + hh * SWA_DH, gi * SWA_W + (hh + 1) * SWA_DH)
            q = q_ref[0, :, sl]
            kn = kn_ref[0, :, sl]
            vn = vn_ref[0, :, sl]
            kc = c_ref[0, :, hh * SWA_DH:(hh + 1) * SWA_DH]
            vc = c_ref[0, :, SWA_W + hh * SWA_DH:SWA_W + (hh + 1) * SWA_DH]
            s = jnp.sum(kc * q, axis=-1, keepdims=True) * scale
            sn = jnp.sum(kn * q, axis=-1, keepdims=True) * scale
            m = jnp.maximum(jnp.max(s, axis=0, keepdims=True), sn)
            p = jnp.exp(s - m)
            pn = jnp.exp(sn - m)
            den = jnp.sum(p, axis=0, keepdims=True) + pn
            osl = slice(hh * SWA_DH, (hh + 1) * SWA_DH)
            o_ref[0, :, osl] = (jnp.sum(p * vc, axis=0, keepdims=True) + pn * vn) / den
            lse_ref[0, :, osl] = jnp.broadcast_to(m + jnp.log(den), (1, SWA_DH))


def _swa_sample(q_r, k_r, v_new, caches):
    bn = q_r.shape[0]
    nkeys = SWA_GROUPS[0][0] // SWA_GROUPS[0][1]
    views = []
    for (win, dil), cache in zip(SWA_GROUPS, caches):
        assert cache.shape[1] == win and win // dil == nkeys
        views.append(cache.reshape(bn, win // dil, dil * 2 * SWA_W))
    row = pl.BlockSpec((1, 1, SWA_QKV), lambda b: (b, 0, 0))
    cspec = pl.BlockSpec((1, nkeys, 2 * SWA_W), lambda b: (b, 0, 0))
    ospec = pl.BlockSpec((1, 1, SWA_W), lambda b: (b, 0, 0))
    outs = pl.pallas_call(
        _swa_sample_kernel,
        grid=(bn,),
        in_specs=[row, row, row, cspec, cspec, cspec],
        out_specs=[ospec] * (2 * N_SWA),
        out_shape=[jax.ShapeDtypeStruct((bn, 1, SWA_W), F32)] * (2 * N_SWA),
        compiler_params=_cparams("parallel"),
        name="swa_sample",
    )(q_r.reshape(bn, 1, SWA_QKV), k_r.reshape(bn, 1, SWA_QKV), v_new.reshape(bn, 1, SWA_QKV), *views)
    return [(outs[2 * g].reshape(bn, SWA_W), outs[2 * g + 1].reshape(bn, SWA_W)) for g in range(N_SWA)]


def _mix_kernel(oa_ref, o0_ref, l0_ref, o1_ref, l1_ref, o2_ref, l2_ref, ga_ref, gb_ref, x_ref,
                wa_ref, wb_ref, wo_ref, nw_ref, x1_ref, xn_ref, slab, *, dils, tm):
    def token_major(ref, dil):
        if dil == 1:
            return ref[0, 0]
        cols = []
        for hh in range(SWA_HEADS):
            for r in range(dil):
                slab[hh, pl.ds(r, tm // dil, stride=dil), :] = ref[0, r, :, hh * SWA_DH:(hh + 1) * SWA_DH]
            cols.append(slab[hh])
        return jnp.concatenate(cols, axis=1)

    os = [token_major(r, d) for r, d in zip((o0_ref, o1_ref, o2_ref), dils)]
    ls = [token_major(r, d) for r, d in zip((l0_ref, l1_ref, l2_ref), dils)]
    lmax = jnp.maximum(jnp.maximum(ls[0], ls[1]), ls[2])
    es = [jnp.exp(l - lmax) for l in ls]
    ob = (es[0] * os[0] + es[1] * os[1] + es[2] * os[2]) / (es[0] + es[1] + es[2])
    pa = jnp.dot(oa_ref[0].astype(BF16), wa_ref[...], preferred_element_type=F32)
    pb = jnp.dot(ob.astype(BF16), wb_ref[...], preferred_element_type=F32)
    merged = _sigmoid(ga_ref[0]) * pa + _sigmoid(gb_ref[0]) * pb
    x1 = x_ref[0] + jnp.dot(merged.astype(BF16), wo_ref[...], preferred_element_type=F32)
    x1_ref[0] = x1
    xn_ref[0] = x1 * lax.rsqrt(jnp.mean(x1 * x1, axis=-1, keepdims=True) + EPS) * nw_ref[...]


def _mix(o_a3, parts, dils, h3, x3, wa, wb, wo, norm_ffn, tm):
    bn, t, _ = x3.shape
    rowblk = lambda w, cb=0: pl.BlockSpec((1, tm, w), lambda b, i: (b, i, cb))
    full = lambda a: pl.BlockSpec(a.shape, lambda b, i: (0, 0))
    nw = norm_ffn.reshape(1, D_MODEL)
    part_specs, flat = [], []
    for (o, lse), dil in zip(parts, dils):
        spec = pl.BlockSpec((1, dil, tm // dil, SWA_W), lambda b, i: (b, 0, i, 0))
        part_specs += [spec, spec]
        flat += [o, lse]
    return pl.pallas_call(
        functools.partial(_mix_kernel, dils=dils, tm=tm),
        grid=(bn, t // tm),
        in_specs=[rowblk(GDN_VW)] + part_specs
                 + [rowblk(D_MODEL, COL_GA // D_MODEL), rowblk(D_MODEL, COL_GB // D_MODEL),
                    rowblk(D_MODEL), full(wa), full(wb), full(wo), full(nw)],
        out_specs=[rowblk(D_MODEL), rowblk(D_MODEL)],
        out_shape=[jax.ShapeDtypeStruct((bn, t, D_MODEL), F32)] * 2,
        scratch_shapes=[pltpu.VMEM((SWA_HEADS, tm, SWA_DH), F32)],
        compiler_params=_cparams("parallel", "parallel"),
        name="mix_out",
    )(o_a3, *flat, h3, h3, x3, wa, wb, wo, nw)


def _topk_rows(vals, k, payload=None):
    n, t = vals.shape
    idx = lax.broadcasted_iota(jnp.int32, (n, t), 0)
    out_v, out_i = [], []
    for _ in range(k):
        m = jnp.max(vals, axis=0, keepdims=True)
        am = jnp.min(jnp.where(vals == m, idx, n), axis=0, keepdims=True)
        hit = idx == am
        out_v.append(m)
        out_i.append(am if payload is None else jnp.sum(jnp.where(hit, payload, 0), axis=0, keepdims=True))
        vals = jnp.where(hit, -jnp.inf, vals)
    return jnp.concatenate(out_v, axis=0), jnp.concatenate(out_i, axis=0)


def _route_kernel(q_ref, keys_ref, eid_ref, gate_ref):
    half = PEER_DKEY // 2
    eids, gates = [], []
    for h in range(PEER_HEADS):
        tops = []
        for p in range(2):
            qs = q_ref[:, (2 * h + p) * half:(2 * h + p + 1) * half].astype(BF16)
            st = _dot_nt(keys_ref[h, p].astype(BF16), qs)
            tops.append(_topk_rows(st, PEER_TOPK))
        (v1, i1), (v2, i2) = tops
        cand = jnp.concatenate([v1[i:i + 1] + v2 for i in range(PEER_TOPK)], axis=0)
        ecand = jnp.concatenate([i1[i:i + 1] * PEER_NKEYS + i2 for i in range(PEER_TOPK)], axis=0)
        sc, e = _topk_rows(cand, PEER_TOPK, payload=ecand)
        ex = jnp.exp(sc - sc[0:1])
        gates.append(ex / jnp.sum(ex, axis=0, keepdims=True))
        eids.append(e)
    eid_ref[...] = jnp.concatenate(eids, axis=0).T
    gate_ref[...] = jnp.concatenate(gates, axis=0)


def _route(qp, sub_keys, tt):
    m = qp.shape[0]
    return pl.pallas_call(
        _route_kernel,
        grid=(m // tt,),
        in_specs=[pl.BlockSpec((tt, PEER_HEADS * PEER_DKEY), lambda i: (i, 0)),
                  pl.BlockSpec(sub_keys.shape, lambda i: (0, 0, 0, 0))],
        out_specs=[pl.BlockSpec((tt, PEER_SEL), lambda i: (i, 0)), pl.BlockSpec((PEER_SEL, tt), lambda i: (0, i))],
        out_shape=[jax.ShapeDtypeStruct((m, PEER_SEL), jnp.int32), jax.ShapeDtypeStruct((PEER_SEL, m), F32)],
        compiler_params=_cparams("parallel"),
        name="peer_route",
    )(qp, sub_keys)


PEER_NBUF = 4


PEER_CHUNKS = D_MODEL // LANE


def _pack_peer_table(expert_down, expert_up):
    hi = lax.bitcast_convert_type(expert_down.astype(BF16), jnp.uint16).astype(jnp.uint32) << 16
    lo = lax.bitcast_convert_type(expert_up.astype(BF16), jnp.uint16).astype(jnp.uint32)
    return (hi | lo).reshape(expert_down.shape[0], PEER_CHUNKS, LANE)


def _peer_kernel(eid_ref, gate_ref, xn_ref, x1_ref, nw_ref, tbl_hbm, y_ref, tbuf, sem, *, n_tok):
    nbuf = PEER_NBUF

    def issue(t, slot, rows):
        for j in rows:
            pltpu.make_async_copy(tbl_hbm.at[eid_ref[t, j]], tbuf.at[slot, :, j, :], sem.at[slot]).start(priority=j % 2)

    def wait(slot):
        pltpu.make_async_copy(tbuf.at[slot], tbuf.at[slot], sem.at[slot]).wait()

    lane = lax.broadcasted_iota(jnp.int32, (PEER_SEL, gate_ref.shape[1]), 1)
    hi_mask = jnp.uint32(0xFFFF0000)
    per_chunk = PEER_SEL // (2 * PEER_CHUNKS)

    def compute(t, slot, t_ahead):
        def issue_part(i):
            if t_ahead is not None:
                issue(t_ahead, (slot - 1) % nbuf, range(i * per_chunk, (i + 1) * per_chunk))

        x = xn_ref[pl.ds(t, 1), :]
        acc = jnp.zeros((PEER_SEL, LANE), F32)
        for c in range(PEER_CHUNKS):
            down = lax.bitcast_convert_type(tbuf[slot, c] & hi_mask, F32)
            acc = acc + down * x[:, c * LANE:(c + 1) * LANE]
            issue_part(c)
        act = jnp.sum(acc, axis=-1, keepdims=True)
        gate = jnp.sum(jnp.where(lane == t, gate_ref[...], 0.0), axis=-1, keepdims=True)
        gelu = 0.5 * act * (1.0 + lax.erf(act * (2.0 ** -0.5)))
        w = gate * gelu
        outs = []
        for c in range(PEER_CHUNKS):
            up = lax.bitcast_convert_type(tbuf[slot, c] << 16, F32)
            outs.append(jnp.sum(up * w, axis=0, keepdims=True))
            issue_part(PEER_CHUNKS + c)
        y_ref[pl.ds(t, 1), :] = x1_ref[pl.ds(t, 1), :] + jnp.concatenate(outs, axis=1)

    assert n_tok % nbuf == 0
    for k in range(nbuf - 1):
        issue(k, k, range(PEER_SEL))

    def group(p, carry):
        for k in range(nbuf):
            t = nbuf * p + k
            wait(k)
            compute(t, k, t + nbuf - 1)
        return carry

    lax.fori_loop(0, n_tok // nbuf - 1, group, 0)
    for k in range(nbuf):
        t = n_tok - nbuf + k
        wait(k)
        compute(t, k, t + nbuf - 1 if k == 0 else None)

    x2 = y_ref[...]
    y_ref[...] = x2 * lax.rsqrt(jnp.mean(x2 * x2, axis=-1, keepdims=True) + EPS) * nw_ref[...]


def _peer(eid, gates_t, xn2, x1, norm_final, table, tt, n_tok):
    m = xn2.shape[0]
    kern = functools.partial(_peer_kernel, n_tok=n_tok)
    return pl.pallas_call(
        kern,
        grid=(m // n_tok,),
        in_specs=[pl.BlockSpec((tt, PEER_SEL), lambda i: (i, 0), memory_space=pltpu.SMEM),
                  pl.BlockSpec((PEER_SEL, tt), lambda i: (0, i)),
                  pl.BlockSpec((n_tok, D_MODEL), lambda i: (i, 0)),
                  pl.BlockSpec((n_tok, D_MODEL), lambda i: (i, 0)),
                  pl.BlockSpec((1, D_MODEL), lambda i: (0, 0)),
                  pl.BlockSpec(memory_space=pl.ANY)],
        out_specs=pl.BlockSpec((n_tok, D_MODEL), lambda i: (i, 0)),
        out_shape=jax.ShapeDtypeStruct((m, D_MODEL), F32),
        scratch_shapes=[pltpu.VMEM((PEER_NBUF, PEER_CHUNKS, PEER_SEL, LANE), jnp.uint32),
                        pltpu.SemaphoreType.DMA((PEER_NBUF,))],
        compiler_params=_cparams("arbitrary"),
        name="peer_experts",
    )(eid, gates_t, xn2, x1, norm_final.reshape(1, D_MODEL), table)


def _pick(m, candidates):
    for c in candidates:
        if m % c == 0:
            return c
    return m


def _layer(x, pos, tail8, s0, caches, w, chunk):
    bn, t, _ = x.shape
    m = bn * t
    x2d = x.reshape(m, D_MODEL)
    tm_big = _pick(m, (1024, 512, 256, 128, 64, 32))

    xn = _rmsnorm(x2d, w["norm_mix"], _pick(m, (512, 256, 128, 64, 32)), BF16)
    h_main = _matmul(xn, w["w_main"], tm_big, 640, "proj_in")
    h_ab = _matmul(xn, w["w_ab"], tm_big, LANE, "proj_ab")

    tp = -(-t // chunk) * chunk
    h3 = h_main.reshape(bn, t, N_MAIN)
    ab3 = h_ab.reshape(bn, t, LANE)
    h3p = h3 if tp == t else jnp.pad(h3, ((0, 0), (0, tp - t), (0, 0)))
    ab3p = ab3 if tp == t else jnp.pad(ab3, ((0, 0), (0, tp - t), (0, 0)))
    o_a, s_new = _gdn(h3p, ab3p, w["conv_w"], tail8, s0, w["a_log"], w["dt_bias"], w["gdn_norm"], chunk, t)
    o_a = o_a[:, :t]

    cosf, sinf = _rope_tables(pos)
    kv_new = []
    if caches is None:
        qkv = _rope_split(h3, cosf, sinf, 256)
        parts = [_swa_prompt_group(q, k, v, gi, 128) for gi, (q, k, v) in enumerate(qkv)]
        dils = tuple(d for _, d in SWA_GROUPS)
        for (win, dil), (_, k, v) in zip(SWA_GROUPS, qkv):
            keep = min(win, t)
            tok = lambda a: (a[:, :, (t - keep) // dil:].transpose(0, 2, 1, 3)
                             .reshape(bn, keep, SWA_HEADS, SWA_DH))
            kv_new.append(jnp.stack([tok(k), tok(v)], axis=2))
        mix_shape = (bn, t)
    else:
        cosf, sinf = jnp.tile(cosf, (bn, 1)), jnp.tile(sinf, (bn, 1))
        q_r, k_r = _rope(h_main, cosf, sinf, _pick(m, (512, 256, 128, 64, 32)))
        v_new = h_main[:, COL_SV:COL_SV + SWA_QKV]
        parts = [(o.reshape(1, 1, m, SWA_W), lse.reshape(1, 1, m, SWA_W))
                 for o, lse in _swa_sample(q_r, k_r, v_new, caches)]
        dils = (1,) * N_SWA
        for gi in range(N_SWA):
            kk = k_r[:, gi * SWA_W:(gi + 1) * SWA_W].reshape(bn, t, SWA_HEADS, SWA_DH)
            vv = v_new[:, gi * SWA_W:(gi + 1) * SWA_W].reshape(bn, t, SWA_HEADS, SWA_DH)
            kv_new.append(jnp.stack([kk, vv], axis=2))
        mix_shape = (1, m)

    mb, mt = mix_shape
    x1, xn2 = _mix(o_a.reshape(mb, mt, GDN_VW), parts, dils, h_main.reshape(mb, mt, N_MAIN),
                   x.reshape(mb, mt, D_MODEL), w["w_br_a"], w["w_br_b"], w["w_out"], w["norm_ffn"],
                   _pick(mt, (256, 128, 64, 32)))
    x1, xn2 = x1.reshape(m, D_MODEL), xn2.reshape(m, D_MODEL)

    mp = -(-m // LANE) * LANE
    xq = xn2 if mp == m else jnp.pad(xn2, ((0, mp - m), (0, 0)))
    qp = _matmul(xq, w["w_query"], _pick(mp, (1024, 512, 256, 128)), 512, "peer_query")
    eid, gates_t = _route(qp, w["sub_keys"], LANE)
    n_tok = min(m, LANE)
    y = _peer(eid, gates_t, xn2, x1, w["norm_final"], w["peer_table"], LANE, n_tok)
    return y.reshape(bn, t, D_MODEL), s_new, h3, kv_new


def kernel(x_prompt, x_sample, state_gdn, state_conv, cache_kv_w128, cache_kv_w512, cache_kv_w2048,
           norm_mix, w_in, conv_w, a_log, dt_bias, gdn_norm, w_br_a, w_br_b, w_out, norm_ffn,
           w_query, sub_keys, expert_down, expert_up, norm_final):
    w_main = jnp.concatenate([w_in[:, IN_GA:], w_in[:, :IN_AB], w_in[:, IN_SQ:IN_GA]], axis=1).astype(BF16)
    w_ab = jnp.pad(w_in[:, IN_AB:IN_SQ], ((0, 0), (0, LANE - 2 * GDN_HEADS))).astype(BF16)
    w = dict(norm_mix=norm_mix, w_main=w_main, w_ab=w_ab, conv_w=conv_w, a_log=a_log, dt_bias=dt_bias,
             gdn_norm=gdn_norm, w_br_a=w_br_a.astype(BF16), w_br_b=w_br_b.astype(BF16), w_out=w_out.astype(BF16),
             norm_ffn=norm_ffn, w_query=w_query.astype(BF16), sub_keys=sub_keys,
             peer_table=_pack_peer_table(expert_down, expert_up), norm_final=norm_final)

    bp, tlen = x_prompt.shape[:2]
    bs, ts = x_sample.shape[:2]
    caches = (cache_kv_w128, cache_kv_w512, cache_kv_w2048)

    tail8 = jnp.concatenate([jnp.zeros((bs, SUBLANE - (CONV_W - 1), GDN_CONV_CH), F32), state_conv], axis=1)
    y_s, gdn_s, h3_s, kv_new = _layer(
        x_sample, PAST_LEN + jnp.arange(ts, dtype=jnp.int32), tail8, state_gdn, caches, w, SUBLANE)
    conv_s = jnp.concatenate([state_conv, h3_s[:, :, COL_Q:COL_Q + GDN_CONV_CH]], axis=1)[:, ts:]
    kv_s = [jnp.concatenate([cache, new.astype(cache.dtype)], axis=1)[:, ts:] for cache, new in zip(caches, kv_new)]

    y_p, gdn_p, h3_p, kv_p = _layer(
        x_prompt, jnp.arange(tlen, dtype=jnp.int32),
        jnp.zeros((bp, SUBLANE, GDN_CONV_CH), F32), jnp.zeros((bp, GDN_HEADS, GDN_DK, GDN_DV), F32),
        None, w, CHUNK)
    conv_p = h3_p[:, tlen - (CONV_W - 1):, COL_Q:COL_Q + GDN_CONV_CH]

    return (y_p, y_s, gdn_p.astype(x_prompt.dtype), conv_p, kv_p[0], kv_p[1], kv_p[2],
            gdn_s.astype(state_gdn.dtype), conv_s, kv_s[0], kv_s[1], kv_s[2])
```

```python
import functools
import math

import jax
import jax.numpy as jnp
import numpy as np
from jax import lax
from jax.experimental import pallas as pl
from jax.experimental.pallas import tpu as pltpu

F32 = jnp.float32
BF16 = jnp.bfloat16
HIGHEST = lax.Precision.HIGHEST

LANE = 128
SUBLANE = 8
VMEM_LIMIT = 56 * 1024 * 1024

D_MODEL = 2048
PAST_LEN = 16384
EPS = 1e-6
GDN_HEADS = 8
GDN_DK = 128
GDN_DV = 128
GDN_QK = GDN_HEADS * GDN_DK
GDN_VW = GDN_HEADS * GDN_DV
GDN_CONV_CH = 2 * GDN_QK + GDN_VW
CONV_W = 4
CHUNK = 64
SWA_GROUPS = ((128, 1), (512, 4), (2048, 16))
N_SWA = 3
SWA_HEADS = 4
SWA_DH = 128
SWA_W = SWA_HEADS * SWA_DH
SWA_QKV = N_SWA * SWA_W
ROT_DIM = SWA_DH // 4
ROPE_THETA = 500000.0
PEER_HEADS = 8
PEER_NKEYS = 128
PEER_DKEY = 256
PEER_TOPK = 16
PEER_SEL = PEER_HEADS * PEER_TOPK

COL_GA = 0
COL_GB = D_MODEL
COL_Q = 2 * D_MODEL
COL_K = COL_Q + GDN_QK
COL_V = COL_K + GDN_QK
COL_Z = COL_V + GDN_VW
COL_SQ = COL_Z + GDN_VW
COL_SK = COL_SQ + SWA_QKV
COL_SV = COL_SK + SWA_QKV
N_MAIN = COL_SV + SWA_QKV
IN_AB = 2 * GDN_QK + 2 * GDN_VW
IN_SQ = IN_AB + 2 * GDN_HEADS
IN_GA = IN_SQ + 3 * SWA_QKV
NEG_BIG = -1e30


def _cparams(*sem):
    return pltpu.CompilerParams(dimension_semantics=sem, vmem_limit_bytes=VMEM_LIMIT)


def _sigmoid(x):
    return 1.0 / (1.0 + jnp.exp(-x))


def _hdot(a, b):
    return jnp.dot(a, b, precision=HIGHEST, preferred_element_type=F32)


def _hdot_nt(a, b):
    return lax.dot_general(a, b, (((1,), (1,)), ((), ())), precision=HIGHEST, preferred_element_type=F32)


def _split_bf16(a):
    hi = a.astype(BF16)
    return hi, (a - hi.astype(F32)).astype(BF16)


def _dot3(a, b, dims=(((1,), (0,)), ((), ()))):
    a_hi, a_lo = _split_bf16(a)
    b_hi, b_lo = _split_bf16(b)
    dot = lambda x, y: lax.dot_general(x, y, dims, preferred_element_type=F32)
    return dot(a_hi, b_hi) + (dot(a_hi, b_lo) + dot(a_lo, b_hi))


def _dot3_nt(a, b):
    return _dot3(a, b, (((1,), (1,)), ((), ())))


def _dot3_tn(a, b):
    return _dot3(a, b, (((0,), (0,)), ((), ())))


def _dot_nt(a, b):
    return lax.dot_general(a, b, (((1,), (1,)), ((), ())), preferred_element_type=F32)


def _rmsnorm_kernel(x_ref, w_ref, o_ref):
    x = x_ref[...]
    y = x * lax.rsqrt(jnp.mean(x * x, axis=-1, keepdims=True) + EPS) * w_ref[...]
    o_ref[...] = y.astype(o_ref.dtype)


def _rmsnorm(x, w, tm, out_dtype):
    m, d = x.shape
    return pl.pallas_call(
        _rmsnorm_kernel,
        grid=(m // tm,),
        in_specs=[pl.BlockSpec((tm, d), lambda i: (i, 0)), pl.BlockSpec((1, d), lambda i: (0, 0))],
        out_specs=pl.BlockSpec((tm, d), lambda i: (i, 0)),
        out_shape=jax.ShapeDtypeStruct((m, d), out_dtype),
        compiler_params=_cparams("parallel"),
        name="rmsnorm",
    )(x, w.reshape(1, d))


def _mm_kernel(x_ref, w_ref, o_ref):
    o_ref[...] = jnp.dot(x_ref[...].astype(BF16), w_ref[...], preferred_element_type=F32)


def _matmul(x, w, tm, tn, name):
    m, k = x.shape
    n = w.shape[1]
    return pl.pallas_call(
        _mm_kernel,
        grid=(n // tn, m // tm),
        in_specs=[pl.BlockSpec((tm, k), lambda j, i: (i, 0)), pl.BlockSpec((k, tn), lambda j, i: (0, j))],
        out_specs=pl.BlockSpec((tm, tn), lambda j, i: (i, j)),
        out_shape=jax.ShapeDtypeStruct((m, n), F32),
        compiler_params=_cparams("parallel", "parallel"),
        name=name,
    )(x, w)


def _rope_kernel(q_ref, k_ref, cos_ref, sin_ref, qo_ref, ko_ref):
    cosf = cos_ref[...]
    sinf = sin_ref[...]
    lane = lax.broadcasted_iota(jnp.int32, cosf.shape, 1)
    first = lane < ROT_DIM // 2
    for src, dst in ((q_ref, qo_ref), (k_ref, ko_ref)):
        for hh in range(SWA_HEADS):
            sl = slice(hh * SWA_DH, (hh + 1) * SWA_DH)
            x = src[:, sl]
            partner = jnp.where(first, pltpu.roll(x, SWA_DH - ROT_DIM // 2, axis=1), pltpu.roll(x, ROT_DIM // 2, axis=1))
            dst[:, sl] = x * cosf + partner * sinf


def _rope(h_main, cosf, sinf, tm):
    m = h_main.shape[0]
    nt = cosf.shape[0] // tm
    qb, kb = COL_SQ // SWA_W, COL_SK // SWA_W
    return pl.pallas_call(
        _rope_kernel,
        grid=(m // tm, N_SWA),
        in_specs=[
            pl.BlockSpec((tm, SWA_W), lambda i, g: (i, qb + g)),
            pl.BlockSpec((tm, SWA_W), lambda i, g: (i, kb + g)),
            pl.BlockSpec((tm, SWA_DH), lambda i, g: (i % nt, 0)),
            pl.BlockSpec((tm, SWA_DH), lambda i, g: (i % nt, 0)),
        ],
        out_specs=[pl.BlockSpec((tm, SWA_W), lambda i, g: (i, g)), pl.BlockSpec((tm, SWA_W), lambda i, g: (i, g))],
        out_shape=[jax.ShapeDtypeStruct((m, SWA_QKV), F32), jax.ShapeDtypeStruct((m, SWA_QKV), F32)],
        compiler_params=_cparams("parallel", "parallel"),
        name="rope",
    )(h_main, h_main, cosf, sinf)


def _rope_split_kernel(*refs, dils, tm):
    ins, cos_ref, sin_ref = refs[:3 * N_SWA], refs[3 * N_SWA], refs[3 * N_SWA + 1]
    outs, slab = refs[3 * N_SWA + 2:6 * N_SWA + 2], refs[6 * N_SWA + 2]
    cosf = cos_ref[...]
    sinf = sin_ref[...]
    first = lax.broadcasted_iota(jnp.int32, cosf.shape, 1) < ROT_DIM // 2
    for gi, dil in enumerate(dils):
        for kind in range(3):
            src, dst = ins[3 * gi + kind], outs[3 * gi + kind]
            for hh in range(SWA_HEADS):
                sl = slice(hh * SWA_DH, (hh + 1) * SWA_DH)
                x = src[0, :, sl]
                if kind < 2:
                    partner = jnp.where(first, pltpu.roll(x, SWA_DH - ROT_DIM // 2, axis=1),
                                        pltpu.roll(x, ROT_DIM // 2, axis=1))
                    x = x * cosf + partner * sinf
                if dil == 1:
                    dst[0, 0, :, sl] = x
                else:
                    slab[hh] = x
                    for r in range(dil):
                        dst[0, r, :, sl] = slab[hh, pl.ds(r, tm // dil, stride=dil), :]


def _rope_split(h3, cosf, sinf, tm):
    bn, t, _ = h3.shape
    dils = tuple(d for _, d in SWA_GROUPS)
    in_specs, out_specs, out_shape = [], [], []
    for gi, dil in enumerate(dils):
        for off in (COL_SQ, COL_SK, COL_SV):
            cb = off // SWA_W + gi
            in_specs.append(pl.BlockSpec((1, tm, SWA_W), lambda b, i, cb=cb: (b, i, cb)))
            out_specs.append(pl.BlockSpec((1, dil, tm // dil, SWA_W), lambda b, i: (b, 0, i, 0)))
            out_shape.append(jax.ShapeDtypeStruct((bn, dil, t // dil, SWA_W), F32))
    tbl = pl.BlockSpec((tm, SWA_DH), lambda b, i: (i, 0))
    outs = pl.pallas_call(
        functools.partial(_rope_split_kernel, dils=dils, tm=tm),
        grid=(bn, t // tm),
        in_specs=in_specs + [tbl, tbl],
        out_specs=out_specs,
        out_shape=out_shape,
        scratch_shapes=[pltpu.VMEM((SWA_HEADS, tm, SWA_DH), F32)],
        compiler_params=_cparams("parallel", "parallel"),
        name="rope_split",
    )(*([h3] * (3 * N_SWA)), cosf, sinf)
    return [tuple(outs[3 * gi:3 * gi + 3]) for gi in range(N_SWA)]


def _rope_tables(pos):
    half = ROT_DIM // 2
    inv = ROPE_THETA ** (-jnp.arange(half, dtype=F32) * 2.0 / ROT_DIM)
    ang = pos.astype(F32)[:, None] * inv[None, :]
    cos, sin = jnp.cos(ang), jnp.sin(ang)
    n = pos.shape[0]
    cosf = jnp.concatenate([cos, cos, jnp.ones((n, SWA_DH - ROT_DIM), F32)], axis=1)
    sinf = jnp.concatenate([-sin, sin, jnp.zeros((n, SWA_DH - ROT_DIM), F32)], axis=1)
    return cosf, sinf


def _gdn_kernel(alog_ref, dtb_ref, q_ref, k_ref, v_ref, z_ref, ab_ref, cwq_ref, cwk_ref, cwv_ref,
                tq_ref, tk_ref, tv_ref, s0_ref, nw_ref, o_ref, sout_ref,
                s_scr, tailq, tailk, tailv, *, chunk, t_total, hps):
    c = pl.program_id(2)

    @pl.when(c == 0)
    def _():
        s_scr[...] = s0_ref[0]
        tailq[...] = tq_ref[0]
        tailk[...] = tk_ref[0]
        tailv[...] = tv_ref[0]

    C = chunk
    heads = range(hps)
    sls = [slice(hh * LANE, (hh + 1) * LANE) for hh in heads]
    hidx = [pl.program_id(1) * hps + hh for hh in heads]
    row8 = lax.broadcasted_iota(jnp.int32, (SUBLANE, LANE), 0)

    def conv(src_ref, tail_ref, w_ref, sl):
        raw = src_ref[0, :, sl]
        w = w_ref[:, sl]
        t8 = tail_ref[:, sl]
        y = raw * w[CONV_W - 1:CONV_W, :]
        for s in range(1, CONV_W):
            rolled = pltpu.roll(raw, s, axis=0)
            top = jnp.where(row8 < s, pltpu.roll(t8, s, axis=0), rolled[0:SUBLANE])
            sh = top if C == SUBLANE else jnp.concatenate([top, rolled[SUBLANE:]], axis=0)
            y = y + sh * w[CONV_W - 1 - s:CONV_W - s, :]
        tail_ref[:, sl] = raw[C - SUBLANE:C]
        return y * _sigmoid(y)

    def l2n(x):
        return x * lax.rsqrt(jnp.sum(x * x, axis=-1, keepdims=True) + EPS)

    q = [l2n(conv(q_ref, tailq, cwq_ref, sl)) * (GDN_DK ** -0.5) for sl in sls]
    k = [l2n(conv(k_ref, tailk, cwk_ref, sl)) for sl in sls]
    v = [conv(v_ref, tailv, cwv_ref, sl) for sl in sls]

    lane = lax.broadcasted_iota(jnp.int32, (C, LANE), 1)
    rowc = lax.broadcasted_iota(jnp.int32, (C, 1), 0)
    valid = (c * C + rowc) < t_total
    ab = ab_ref[0]
    beta, g = [], []
    for h in hidx:
        a = jnp.sum(jnp.where(lane == h, ab, 0.0), axis=-1, keepdims=True)
        b = jnp.sum(jnp.where(lane == h + GDN_HEADS, ab, 0.0), axis=-1, keepdims=True)
        sp_in = a + dtb_ref[h]
        softplus = jnp.maximum(sp_in, 0.0) + jnp.log(1.0 + jnp.exp(-jnp.abs(sp_in)))
        beta.append(jnp.where(valid, _sigmoid(b), 0.0))
        g.append(jnp.where(valid, -jnp.exp(jnp.full((C, 1), alog_ref[h], F32)) * softplus, 0.0))

    ri = lax.broadcasted_iota(jnp.int32, (C, C), 0)
    ci = lax.broadcasted_iota(jnp.int32, (C, C), 1)
    causal = ri >= ci
    strict = ri > ci
    tril = causal.astype(F32)
    e0 = (lane == 0).astype(F32)
    eye = (ri == ci).astype(F32)
    gc_b = [_hdot(tril, jnp.broadcast_to(gh, (C, LANE))) for gh in g]
    gc_row = [_hdot_nt(e0, gch) for gch in gc_b]
    decay = [jnp.where(causal, jnp.exp(jnp.where(causal, gch[:, :C] - grh, 0.0)), 0.0)
             for gch, grh in zip(gc_b, gc_row)]
    kb = [kh * bh for kh, bh in zip(k, beta)]
    xpow = [-jnp.where(strict, _dot3_nt(kbh, kh) * dh, 0.0) for kbh, kh, dh in zip(kb, k, decay)]
    tinv = [eye + xh for xh in xpow]
    for _ in range(int(math.log2(C)) - 1):
        xpow = [_dot3(xh, xh) for xh in xpow]
        tinv = [th + _dot3(th, xh) for th, xh in zip(tinv, xpow)]
    eg = [jnp.exp(gch) for gch in gc_b]
    value = [_dot3(th, vh * bh) for th, vh, bh in zip(tinv, v, beta)]
    kcd = [_dot3(th, kbh * egh) for th, kbh, egh in zip(tinv, kb, eg)]
    attn = [jnp.where(causal, _dot3_nt(qh, kh) * dh, 0.0) for qh, kh, dh in zip(q, k, decay)]
    glast = [gch[C - 1:C, :] for gch in gc_b]
    kdec = [kh * jnp.exp(glh - gch) for kh, glh, gch in zip(k, glast, gc_b)]

    s = [s_scr[hh] for hh in heads]
    v_new = [vh - _dot3(kh, sh) for vh, kh, sh in zip(value, kcd, s)]
    o = [_dot3(qh * egh, sh) + _dot3(ah, vnh) for qh, egh, sh, ah, vnh in zip(q, eg, s, attn, v_new)]
    for hh in heads:
        s_scr[hh] = s[hh] * jnp.exp(glast[hh]) + _dot3_tn(kdec[hh], v_new[hh])

    nw = nw_ref[...]
    for hh, sl in zip(heads, sls):
        z = z_ref[0, :, sl]
        on = o[hh] * lax.rsqrt(jnp.mean(o[hh] * o[hh], axis=-1, keepdims=True) + EPS) * nw
        o_ref[0, :, sl] = on * (z * _sigmoid(z))

    @pl.when(c == pl.num_programs(2) - 1)
    def _():
        sout_ref[0] = s_scr[...]


GDN_HEADS_PER_STEP = 8


def _gdn(h_main3, ab3, conv_w, tail8, s0, a_log, dt_bias, gdn_norm, chunk, t_total):
    bn, tp, _ = h_main3.shape
    nc = tp // chunk
    hps = GDN_HEADS_PER_STEP
    w = hps * LANE
    hq, hk, hv, hz = COL_Q // w, COL_K // w, COL_V // w, COL_Z // w
    cq, ck, cv = 0, GDN_QK // w, 2 * GDN_QK // w
    col = lambda off: pl.BlockSpec((1, chunk, w), lambda b, h, c: (b, c, off + h))
    cw = lambda off: pl.BlockSpec((CONV_W, w), lambda b, h, c: (0, off + h))
    tl = lambda off: pl.BlockSpec((1, SUBLANE, w), lambda b, h, c: (b, 0, off + h))
    smem = pl.BlockSpec(memory_space=pltpu.SMEM)
    kern = functools.partial(_gdn_kernel, chunk=chunk, t_total=t_total, hps=hps)
    return pl.pallas_call(
        kern,
        grid=(bn, GDN_HEADS // hps, nc),
        in_specs=[smem, smem, col(hq), col(hk), col(hv), col(hz),
                  pl.BlockSpec((1, chunk, LANE), lambda b, h, c: (b, c, 0)),
                  cw(cq), cw(ck), cw(cv), tl(cq), tl(ck), tl(cv),
                  pl.BlockSpec((1, hps, GDN_DK, GDN_DV), lambda b, h, c: (b, h, 0, 0)),
                  pl.BlockSpec((1, GDN_DV), lambda b, h, c: (0, 0))],
        out_specs=[pl.BlockSpec((1, chunk, w), lambda b, h, c: (b, c, h)),
                   pl.BlockSpec((1, hps, GDN_DK, GDN_DV), lambda b, h, c: (b, h, 0, 0))],
        out_shape=[jax.ShapeDtypeStruct((bn, tp, GDN_VW), F32),
                   jax.ShapeDtypeStruct((bn, GDN_HEADS, GDN_DK, GDN_DV), F32)],
        scratch_shapes=[pltpu.VMEM((hps, GDN_DK, GDN_DV), F32)] + [pltpu.VMEM((SUBLANE, w), F32)] * 3,
        compiler_params=_cparams("parallel", "parallel", "arbitrary"),
        name="gdn",
    )(a_log, dt_bias, h_main3, h_main3, h_main3, h_main3, ab3, conv_w, conv_w, conv_w,
      tail8, tail8, tail8, s0, gdn_norm.reshape(1, GDN_DV))


def _swa_kernel(q_ref, kp_ref, kc_ref, vp_ref, vc_ref, o_ref, lse_ref):
    qi = pl.program_id(2)
    n = q_ref.shape[2]
    ri = lax.broadcasted_iota(jnp.int32, (n, n), 0)
    ci = lax.broadcasted_iota(jnp.int32, (n, n), 1)
    mask_prev = jnp.logical_and(ci >= ri, qi > 0)
    mask_cur = ci <= ri
    scale = SWA_DH ** -0.5
    for hh in range(SWA_HEADS):
        sl = slice(hh * SWA_DH, (hh + 1) * SWA_DH)
        q = q_ref[0, 0, :, sl].astype(BF16)
        sp = jnp.where(mask_prev, _dot_nt(q, kp_ref[0, 0, :, sl].astype(BF16)) * scale, NEG_BIG)
        sc = jnp.where(mask_cur, _dot_nt(q, kc_ref[0, 0, :, sl].astype(BF16)) * scale, NEG_BIG)
        m = jnp.maximum(jnp.max(sp, axis=-1, keepdims=True), jnp.max(sc, axis=-1, keepdims=True))
        pp = jnp.exp(sp - m)
        pc = jnp.exp(sc - m)
        ssum = jnp.sum(pp, axis=-1, keepdims=True) + jnp.sum(pc, axis=-1, keepdims=True)
        acc = (jnp.dot(pp.astype(BF16), vp_ref[0, 0, :, sl].astype(BF16), preferred_element_type=F32)
               + jnp.dot(pc.astype(BF16), vc_ref[0, 0, :, sl].astype(BF16), preferred_element_type=F32))
        o_ref[0, 0, :, sl] = acc / ssum
        lse_ref[0, 0, :, sl] = jnp.broadcast_to(m + jnp.log(ssum), (n, SWA_DH))


def _swa_prompt_group(q, k, v, gi, qblk):
    bn, dil, tl, _ = q.shape
    cur = lambda b, r, i: (b, r, i, 0)
    prev = lambda b, r, i: (b, r, jnp.maximum(i - 1, 0), 0)
    blk = (1, 1, qblk, SWA_W)
    return pl.pallas_call(
        _swa_kernel,
        grid=(bn, dil, tl // qblk),
        in_specs=[pl.BlockSpec(blk, cur), pl.BlockSpec(blk, prev), pl.BlockSpec(blk, cur),
                  pl.BlockSpec(blk, prev), pl.BlockSpec(blk, cur)],
        out_specs=[pl.BlockSpec(blk, cur)] * 2,
        out_shape=[jax.ShapeDtypeStruct(q.shape, F32)] * 2,
        compiler_params=_cparams("parallel", "parallel", "arbitrary"),
        name=f"swa_prompt_g{gi}",
    )(q, k, k, v, v)


def _swa_sample_kernel(q_ref, kn_ref, vn_ref, c0_ref, c1_ref, c2_ref, *out_refs):
    scale = SWA_DH ** -0.5
    for gi, c_ref in enumerate((c0_ref, c1_ref, c2_ref)):
        o_ref, lse_ref = out_refs[2 * gi], out_refs[2 * gi + 1]
        for hh in range(SWA_HEADS):
            sl = slice(gi * SWA_W + hh * SWA_DH, gi * SWA_W + (hh + 1) * SWA_DH)
            q = q_ref[0, :, sl]
            kn = kn_ref[0, :, sl]
            vn = vn_ref[0, :, sl]
            kc = c_ref[0, :, hh * SWA_DH:(hh + 1) * SWA_DH]
            vc = c_ref[0, :, SWA_W + hh * SWA_DH:SWA_W + (hh + 1) * SWA_DH]
            s = jnp.sum(kc * q, axis=-1, keepdims=True) * scale
            sn = jnp.sum(kn * q, axis=-1, keepdims=True) * scale
            m = jnp.maximum(jnp.max(s, axis=0, keepdims=True), sn)
            p = jnp.exp(s - m)
            pn = jnp.exp(sn - m)
            den = jnp.sum(p, axis=0, keepdims=True) + pn
            osl = slice(hh * SWA_DH, (hh + 1) * SWA_DH)
            o_ref[0, :, osl] = (jnp.sum(p * vc, axis=0, keepdims=True) + pn * vn) / den
            lse_ref[0, :, osl] = jnp.broadcast_to(m + jnp.log(den), (1, SWA_DH))


def _swa_sample(q_r, k_r, v_new, caches):
    bn = q_r.shape[0]
    nkeys = SWA_GROUPS[0][0] // SWA_GROUPS[0][1]
    views = []
    for (win, dil), cache in zip(SWA_GROUPS, caches):
        assert cache.shape[1] == win and win // dil == nkeys
        views.append(cache.reshape(bn, win // dil, dil * 2 * SWA_W))
    row = pl.BlockSpec((1, 1, SWA_QKV), lambda b: (b, 0, 0))
    cspec = pl.BlockSpec((1, nkeys, 2 * SWA_W), lambda b: (b, 0, 0))
    ospec = pl.BlockSpec((1, 1, SWA_W), lambda b: (b, 0, 0))
    outs = pl.pallas_call(
        _swa_sample_kernel,
        grid=(bn,),
        in_specs=[row, row, row, cspec, cspec, cspec],
        out_specs=[ospec] * (2 * N_SWA),
        out_shape=[jax.ShapeDtypeStruct((bn, 1, SWA_W), F32)] * (2 * N_SWA),
        compiler_params=_cparams("parallel"),
        name="swa_sample",
    )(q_r.reshape(bn, 1, SWA_QKV), k_r.reshape(bn, 1, SWA_QKV), v_new.reshape(bn, 1, SWA_QKV), *views)
    return [(outs[2 * g].reshape(bn, SWA_W), outs[2 * g + 1].reshape(bn, SWA_W)) for g in range(N_SWA)]


def _mix_kernel(oa_ref, o0_ref, l0_ref, o1_ref, l1_ref, o2_ref, l2_ref, ga_ref, gb_ref, x_ref,
                wa_ref, wb_ref, wo_ref, nw_ref, x1_ref, xn_ref, slab, *, dils, tm):
    def token_major(ref, dil):
        if dil == 1:
            return ref[0, 0]
        cols = []
        for hh in range(SWA_HEADS):
            for r in range(dil):
                slab[hh, pl.ds(r, tm // dil, stride=dil), :] = ref[0, r, :, hh * SWA_DH:(hh + 1) * SWA_DH]
            cols.append(slab[hh])
        return jnp.concatenate(cols, axis=1)

    os = [token_major(r, d) for r, d in zip((o0_ref, o1_ref, o2_ref), dils)]
    ls = [token_major(r, d) for r, d in zip((l0_ref, l1_ref, l2_ref), dils)]
    lmax = jnp.maximum(jnp.maximum(ls[0], ls[1]), ls[2])
    es = [jnp.exp(l - lmax) for l in ls]
    ob = (es[0] * os[0] + es[1] * os[1] + es[2] * os[2]) / (es[0] + es[1] + es[2])
    pa = jnp.dot(oa_ref[0].astype(BF16), wa_ref[...], preferred_element_type=F32)
    pb = jnp.dot(ob.astype(BF16), wb_ref[...], preferred_element_type=F32)
    merged = _sigmoid(ga_ref[0]) * pa + _sigmoid(gb_ref[0]) * pb
    x1 = x_ref[0] + jnp.dot(merged.astype(BF16), wo_ref[...], preferred_element_type=F32)
    x1_ref[0] = x1
    xn_ref[0] = x1 * lax.rsqrt(jnp.mean(x1 * x1, axis=-1, keepdims=True) + EPS) * nw_ref[...]


def _mix(o_a3, parts, dils, h3, x3, wa, wb, wo, norm_ffn, tm):
    bn, t, _ = x3.shape
    rowblk = lambda w, cb=0: pl.BlockSpec((1, tm, w), lambda b, i: (b, i, cb))
    full = lambda a: pl.BlockSpec(a.shape, lambda b, i: (0, 0))
    nw = norm_ffn.reshape(1, D_MODEL)
    part_specs, flat = [], []
    for (o, lse), dil in zip(parts, dils):
        spec = pl.BlockSpec((1, dil, tm // dil, SWA_W), lambda b, i: (b, 0, i, 0))
        part_specs += [spec, spec]
        flat += [o, lse]
    return pl.pallas_call(
        functools.partial(_mix_kernel, dils=dils, tm=tm),
        grid=(bn, t // tm),
        in_specs=[rowblk(GDN_VW)] + part_specs
                 + [rowblk(D_MODEL, COL_GA // D_MODEL), rowblk(D_MODEL, COL_GB // D_MODEL),
                    rowblk(D_MODEL), full(wa), full(wb), full(wo), full(nw)],
        out_specs=[rowblk(D_MODEL), rowblk(D_MODEL)],
        out_shape=[jax.ShapeDtypeStruct((bn, t, D_MODEL), F32)] * 2,
        scratch_shapes=[pltpu.VMEM((SWA_HEADS, tm, SWA_DH), F32)],
        compiler_params=_cparams("parallel", "parallel"),
        name="mix_out",
    )(o_a3, *flat, h3, h3, x3, wa, wb, wo, nw)


def _topk_rows(vals, k, payload=None):
    n, t = vals.shape
    idx = lax.broadcasted_iota(jnp.int32, (n, t), 0)
    out_v, out_i = [], []
    for _ in range(k):
        m = jnp.max(vals, axis=0, keepdims=True)
        am = jnp.min(jnp.where(vals == m, idx, n), axis=0, keepdims=True)
        hit = idx == am
        out_v.append(m)
        out_i.append(am if payload is None else jnp.sum(jnp.where(hit, payload, 0), axis=0, keepdims=True))
        vals = jnp.where(hit, -jnp.inf, vals)
    return jnp.concatenate(out_v, axis=0), jnp.concatenate(out_i, axis=0)


def _route_kernel(q_ref, keys_ref, eid_ref, gate_ref):
    half = PEER_DKEY // 2
    eids, gates = [], []
    for h in range(PEER_HEADS):
        tops = []
        for p in range(2):
            qs = q_ref[:, (2 * h + p) * half:(2 * h + p + 1) * half].astype(BF16)
            st = _dot_nt(keys_ref[h, p].astype(BF16), qs)
            tops.append(_topk_rows(st, PEER_TOPK))
        (v1, i1), (v2, i2) = tops
        sub = SUBLANE
        assert PEER_TOPK == 2 * sub
        rows = [(slice(0, 1), slice(0, PEER_TOPK))]
        rows += [(slice(i, i + 1), slice(0, sub)) for i in range(1, sub)]
        rows += [(slice(sub, PEER_TOPK), slice(0, 1))]
        cand = jnp.concatenate([v1[a] + v2[b] for a, b in rows], axis=0)
        ecand = jnp.concatenate([i1[a] * PEER_NKEYS + i2[b] for a, b in rows], axis=0)
        sc, e = _topk_rows(cand, PEER_TOPK, payload=ecand)
        ex = jnp.exp(sc - sc[0:1])
        gates.append(ex / jnp.sum(ex, axis=0, keepdims=True))
        eids.append(e)
    eid_ref[...] = jnp.concatenate(eids, axis=0).T
    gate_ref[...] = jnp.concatenate(gates, axis=0)


def _route(qp, sub_keys, tt):
    m = qp.shape[0]
    return pl.pallas_call(
        _route_kernel,
        grid=(m // tt,),
        in_specs=[pl.BlockSpec((tt, PEER_HEADS * PEER_DKEY), lambda i: (i, 0)),
                  pl.BlockSpec(sub_keys.shape, lambda i: (0, 0, 0, 0))],
        out_specs=[pl.BlockSpec((tt, PEER_SEL), lambda i: (i, 0)), pl.BlockSpec((PEER_SEL, tt), lambda i: (0, i))],
        out_shape=[jax.ShapeDtypeStruct((m, PEER_SEL), jnp.int32), jax.ShapeDtypeStruct((PEER_SEL, m), F32)],
        compiler_params=_cparams("parallel"),
        name="peer_route",
    )(qp, sub_keys)


PEER_NBUF = 8


PEER_CHUNKS = D_MODEL // LANE


def _pack_kernel(d_ref, u_ref, o_ref):
    hi = lax.bitcast_convert_type(d_ref[...].astype(BF16).astype(F32), jnp.uint32)
    lo = lax.bitcast_convert_type(u_ref[...].astype(BF16).astype(F32), jnp.uint32) >> 16
    word = hi | lo
    for c in range(PEER_CHUNKS):
        o_ref[:, c, :] = word[:, c * LANE:(c + 1) * LANE]


def _pack_peer_table(expert_down, expert_up, tr=256):
    e, d = expert_down.shape
    return pl.pallas_call(
        _pack_kernel,
        grid=(e // tr,),
        in_specs=[pl.BlockSpec((tr, d), lambda i: (i, 0))] * 2,
        out_specs=pl.BlockSpec((tr, PEER_CHUNKS, LANE), lambda i: (i, 0, 0)),
        out_shape=jax.ShapeDtypeStruct((e, PEER_CHUNKS, LANE), jnp.uint32),
        compiler_params=_cparams("parallel"),
        name="peer_pack",
    )(expert_down, expert_up)


def _peer_kernel(eid_ref, gate_ref, xn_ref, x1_ref, nw_ref, tbl_hbm, y_ref, tbuf, sem, *, n_tok):
    nbuf = PEER_NBUF

    def issue(t, slot, rows):
        for j in rows:
            pltpu.make_async_copy(tbl_hbm.at[eid_ref[t, j]], tbuf.at[slot, :, j, :], sem.at[slot]).start(priority=j % 2)

    def wait(slot):
        pltpu.make_async_copy(tbuf.at[slot], tbuf.at[slot], sem.at[slot]).wait()

    lane = lax.broadcasted_iota(jnp.int32, (PEER_SEL, gate_ref.shape[1]), 1)
    hi_mask = jnp.uint32(0xFFFF0000)
    per_chunk = PEER_SEL // (2 * PEER_CHUNKS)

    def compute(t, slot, t_ahead):
        def issue_part(i):
            if t_ahead is not None:
                issue(t_ahead, (slot - 1) % nbuf, range(i * per_chunk, (i + 1) * per_chunk))

        x = xn_ref[pl.ds(t, 1), :]
        acc = jnp.zeros((PEER_SEL, LANE), F32)
        for c in range(PEER_CHUNKS):
            down = lax.bitcast_convert_type(tbuf[slot, c] & hi_mask, F32)
            acc = acc + down * x[:, c * LANE:(c + 1) * LANE]
            issue_part(c)
        act = jnp.sum(acc, axis=-1, keepdims=True)
        gate = jnp.sum(jnp.where(lane == t, gate_ref[...], 0.0), axis=-1, keepdims=True)
        gelu = 0.5 * act * (1.0 + lax.erf(act * (2.0 ** -0.5)))
        w = gate * gelu
        outs = []
        for c in range(PEER_CHUNKS):
            up = lax.bitcast_convert_type(tbuf[slot, c] << 16, F32)
            outs.append(jnp.sum(up * w, axis=0, keepdims=True))
            issue_part(PEER_CHUNKS + c)
        y_ref[pl.ds(t, 1), :] = x1_ref[pl.ds(t, 1), :] + jnp.concatenate(outs, axis=1)

    assert n_tok % nbuf == 0
    for k in range(nbuf - 1):
        issue(k, k, range(PEER_SEL))

    def group(p, carry):
        for k in range(nbuf):
            t = nbuf * p + k
            wait(k)
            compute(t, k, t + nbuf - 1)
        return carry

    lax.fori_loop(0, n_tok // nbuf - 1, group, 0)
    for k in range(nbuf):
        t = n_tok - nbuf + k
        wait(k)
        compute(t, k, t + nbuf - 1 if k == 0 else None)

    x2 = y_ref[...]
    y_ref[...] = x2 * lax.rsqrt(jnp.mean(x2 * x2, axis=-1, keepdims=True) + EPS) * nw_ref[...]


def _peer(eid, gates_t, xn2, x1, norm_final, table, tt, n_tok):
    m = xn2.shape[0]
    kern = functools.partial(_peer_kernel, n_tok=n_tok)
    return pl.pallas_call(
        kern,
        grid=(m // n_tok,),
        in_specs=[pl.BlockSpec((tt, PEER_SEL), lambda i: (i, 0), memory_space=pltpu.SMEM),
                  pl.BlockSpec((PEER_SEL, tt), lambda i: (0, i)),
                  pl.BlockSpec((n_tok, D_MODEL), lambda i: (i, 0)),
                  pl.BlockSpec((n_tok, D_MODEL), lambda i: (i, 0)),
                  pl.BlockSpec((1, D_MODEL), lambda i: (0, 0)),
                  pl.BlockSpec(memory_space=pl.ANY)],
        out_specs=pl.BlockSpec((n_tok, D_MODEL), lambda i: (i, 0)),
        out_shape=jax.ShapeDtypeStruct((m, D_MODEL), F32),
        scratch_shapes=[pltpu.VMEM((PEER_NBUF, PEER_CHUNKS, PEER_SEL, LANE), jnp.uint32),
                        pltpu.SemaphoreType.DMA((PEER_NBUF,))],
        compiler_params=_cparams("arbitrary"),
        name="peer_experts",
    )(eid, gates_t, xn2, x1, norm_final.reshape(1, D_MODEL), table)


def _pick(m, candidates):
    for c in candidates:
        if m % c == 0:
            return c
    return m


def _layer(x, pos, tail8, s0, caches, w, chunk):
    bn, t, _ = x.shape
    m = bn * t
    x2d = x.reshape(m, D_MODEL)
    tm_big = _pick(m, (1024, 512, 256, 128, 64, 32))

    xn = _rmsnorm(x2d, w["norm_mix"], _pick(m, (512, 256, 128, 64, 32)), BF16)
    h_main = _matmul(xn, w["w_main"], tm_big, 640, "proj_in")
    h_ab = _matmul(xn, w["w_ab"], tm_big, LANE, "proj_ab")

    tp = -(-t // chunk) * chunk
    h3 = h_main.reshape(bn, t, N_MAIN)
    ab3 = h_ab.reshape(bn, t, LANE)
    h3p = h3 if tp == t else jnp.pad(h3, ((0, 0), (0, tp - t), (0, 0)))
    ab3p = ab3 if tp == t else jnp.pad(ab3, ((0, 0), (0, tp - t), (0, 0)))
    o_a, s_new = _gdn(h3p, ab3p, w["conv_w"], tail8, s0, w["a_log"], w["dt_bias"], w["gdn_norm"], chunk, t)
    o_a = o_a[:, :t]

    cosf, sinf = _rope_tables(pos)
    kv_new = []
    if caches is None:
        qkv = _rope_split(h3, cosf, sinf, 256)
        parts = [_swa_prompt_group(q, k, v, gi, 128) for gi, (q, k, v) in enumerate(qkv)]
        dils = tuple(d for _, d in SWA_GROUPS)
        for (win, dil), (_, k, v) in zip(SWA_GROUPS, qkv):
            keep = min(win, t)
            tok = lambda a: (a[:, :, (t - keep) // dil:].transpose(0, 2, 1, 3)
                             .reshape(bn, keep, SWA_HEADS, SWA_DH))
            kv_new.append(jnp.stack([tok(k), tok(v)], axis=2))
        mix_shape = (bn, t)
    else:
        cosf, sinf = jnp.tile(cosf, (bn, 1)), jnp.tile(sinf, (bn, 1))
        q_r, k_r = _rope(h_main, cosf, sinf, _pick(m, (512, 256, 128, 64, 32)))
        v_new = h_main[:, COL_SV:COL_SV + SWA_QKV]
        parts = [(o.reshape(1, 1, m, SWA_W), lse.reshape(1, 1, m, SWA_W))
                 for o, lse in _swa_sample(q_r, k_r, v_new, caches)]
        dils = (1,) * N_SWA
        for gi in range(N_SWA):
            kk = k_r[:, gi * SWA_W:(gi + 1) * SWA_W].reshape(bn, t, SWA_HEADS, SWA_DH)
            vv = v_new[:, gi * SWA_W:(gi + 1) * SWA_W].reshape(bn, t, SWA_HEADS, SWA_DH)
            kv_new.append(jnp.stack([kk, vv], axis=2))
        mix_shape = (1, m)

    mb, mt = mix_shape
    x1, xn2 = _mix(o_a.reshape(mb, mt, GDN_VW), parts, dils, h_main.reshape(mb, mt, N_MAIN),
                   x.reshape(mb, mt, D_MODEL), w["w_br_a"], w["w_br_b"], w["w_out"], w["norm_ffn"],
                   _pick(mt, (256, 128, 64, 32)))
    x1, xn2 = x1.reshape(m, D_MODEL), xn2.reshape(m, D_MODEL)

    mp = -(-m // LANE) * LANE
    xq = xn2 if mp == m else jnp.pad(xn2, ((0, mp - m), (0, 0)))
    qp = _matmul(xq, w["w_query"], _pick(mp, (1024, 512, 256, 128)), 512, "peer_query")
    eid, gates_t = _route(qp, w["sub_keys"], LANE)
    n_tok = min(m, LANE)
    y = _peer(eid, gates_t, xn2, x1, w["norm_final"], w["peer_table"], LANE, n_tok)
    return y.reshape(bn, t, D_MODEL), s_new, h3, kv_new


def kernel(x_prompt, x_sample, state_gdn, state_conv, cache_kv_w128, cache_kv_w512, cache_kv_w2048,
           norm_mix, w_in, conv_w, a_log, dt_bias, gdn_norm, w_br_a, w_br_b, w_out, norm_ffn,
           w_query, sub_keys, expert_down, expert_up, norm_final):
    w_main = jnp.concatenate([w_in[:, IN_GA:], w_in[:, :IN_AB], w_in[:, IN_SQ:IN_GA]], axis=1).astype(BF16)
    w_ab = jnp.pad(w_in[:, IN_AB:IN_SQ], ((0, 0), (0, LANE - 2 * GDN_HEADS))).astype(BF16)
    w = dict(norm_mix=norm_mix, w_main=w_main, w_ab=w_ab, conv_w=conv_w, a_log=a_log, dt_bias=dt_bias,
             gdn_norm=gdn_norm, w_br_a=w_br_a.astype(BF16), w_br_b=w_br_b.astype(BF16), w_out=w_out.astype(BF16),
             norm_ffn=norm_ffn, w_query=w_query.astype(BF16), sub_keys=sub_keys,
             peer_table=_pack_peer_table(expert_down, expert_up), norm_final=norm_final)

    bp, tlen = x_prompt.shape[:2]
    bs, ts = x_sample.shape[:2]
    caches = (cache_kv_w128, cache_kv_w512, cache_kv_w2048)

    tail8 = jnp.concatenate([jnp.zeros((bs, SUBLANE - (CONV_W - 1), GDN_CONV_CH), F32), state_conv], axis=1)
    y_s, gdn_s, h3_s, kv_new = _layer(
        x_sample, PAST_LEN + jnp.arange(ts, dtype=jnp.int32), tail8, state_gdn, caches, w, SUBLANE)
    conv_s = jnp.concatenate([state_conv, h3_s[:, :, COL_Q:COL_Q + GDN_CONV_CH]], axis=1)[:, ts:]
    kv_s = [jnp.concatenate([cache, new.astype(cache.dtype)], axis=1)[:, ts:] for cache, new in zip(caches, kv_new)]

    y_p, gdn_p, h3_p, kv_p = _layer(
        x_prompt, jnp.arange(tlen, dtype=jnp.int32),
        jnp.zeros((bp, SUBLANE, GDN_CONV_CH), F32), jnp.zeros((bp, GDN_HEADS, GDN_DK, GDN_DV), F32),
        None, w, CHUNK)
    conv_p = h3_p[:, tlen - (CONV_W - 1):, COL_Q:COL_Q + GDN_CONV_CH]

    return (y_p, y_s, gdn_p.astype(x_prompt.dtype), conv_p, kv_p[0], kv_p[1], kv_p[2],
            gdn_s.astype(state_gdn.dtype), conv_s, kv_s[0], kv_s[1], kv_s[2])
```

```python
import functools
import math

import jax
import jax.numpy as jnp
import numpy as np
from jax import lax
from jax.experimental import pallas as pl
from jax.experimental.pallas import tpu as pltpu

F32 = jnp.float32
BF16 = jnp.bfloat16
HIGHEST = lax.Precision.HIGHEST

LANE = 128
SUBLANE = 8
VMEM_LIMIT = 56 * 1024 * 1024

D_MODEL = 2048
PAST_LEN = 16384
EPS = 1e-6
GDN_HEADS = 8
GDN_DK = 128
GDN_DV = 128
GDN_QK = GDN_HEADS * GDN_DK
GDN_VW = GDN_HEADS * GDN_DV
GDN_CONV_CH = 2 * GDN_QK + GDN_VW
CONV_W = 4
CHUNK = 64
SWA_GROUPS = ((128, 1), (512, 4), (2048, 16))
N_SWA = 3
SWA_HEADS = 4
SWA_DH = 128
SWA_W = SWA_HEADS * SWA_DH
SWA_QKV = N_SWA * SWA_W
ROT_DIM = SWA_DH // 4
ROPE_THETA = 500000.0
PEER_HEADS = 8
PEER_NKEYS = 128
PEER_DKEY = 256
PEER_TOPK = 16
PEER_SEL = PEER_HEADS * PEER_TOPK

COL_GA = 0
COL_GB = D_MODEL
COL_Q = 2 * D_MODEL
COL_K = COL_Q + GDN_QK
COL_V = COL_K + GDN_QK
COL_Z = COL_V + GDN_VW
COL_SQ = COL_Z + GDN_VW
COL_SK = COL_SQ + SWA_QKV
COL_SV = COL_SK + SWA_QKV
N_MAIN = COL_SV + SWA_QKV
IN_AB = 2 * GDN_QK + 2 * GDN_VW
IN_SQ = IN_AB + 2 * GDN_HEADS
IN_GA = IN_SQ + 3 * SWA_QKV
NEG_BIG = -1e30


def _cparams(*sem):
    return pltpu.CompilerParams(dimension_semantics=sem, vmem_limit_bytes=VMEM_LIMIT)


def _sigmoid(x):
    return 1.0 / (1.0 + jnp.exp(-x))


def _hdot(a, b):
    return jnp.dot(a, b, precision=HIGHEST, preferred_element_type=F32)


def _hdot_nt(a, b):
    return lax.dot_general(a, b, (((1,), (1,)), ((), ())), precision=HIGHEST, preferred_element_type=F32)


def _split_bf16(a):
    hi = a.astype(BF16)
    return hi, (a - hi.astype(F32)).astype(BF16)


def _dot3(a, b, dims=(((1,), (0,)), ((), ()))):
    a_hi, a_lo = _split_bf16(a)
    b_hi, b_lo = _split_bf16(b)
    dot = lambda x, y: lax.dot_general(x, y, dims, preferred_element_type=F32)
    return dot(a_hi, b_hi) + (dot(a_hi, b_lo) + dot(a_lo, b_hi))


def _dot3_nt(a, b):
    return _dot3(a, b, (((1,), (1,)), ((), ())))


def _dot3_tn(a, b):
    return _dot3(a, b, (((0,), (0,)), ((), ())))


def _dot_nt(a, b):
    return lax.dot_general(a, b, (((1,), (1,)), ((), ())), preferred_element_type=F32)


def _rmsnorm_kernel(x_ref, w_ref, o_ref):
    x = x_ref[...]
    y = x * lax.rsqrt(jnp.mean(x * x, axis=-1, keepdims=True) + EPS) * w_ref[...]
    o_ref[...] = y.astype(o_ref.dtype)


def _rmsnorm(x, w, tm, out_dtype):
    m, d = x.shape
    return pl.pallas_call(
        _rmsnorm_kernel,
        grid=(m // tm,),
        in_specs=[pl.BlockSpec((tm, d), lambda i: (i, 0)), pl.BlockSpec((1, d), lambda i: (0, 0))],
        out_specs=pl.BlockSpec((tm, d), lambda i: (i, 0)),
        out_shape=jax.ShapeDtypeStruct((m, d), out_dtype),
        compiler_params=_cparams("parallel"),
        name="rmsnorm",
    )(x, w.reshape(1, d))


def _mm_kernel(x_ref, w_ref, o_ref):
    o_ref[...] = jnp.dot(x_ref[...].astype(BF16), w_ref[...], preferred_element_type=F32)


def _matmul(x, w, tm, tn, name):
    m, k = x.shape
    n = w.shape[1]
    return pl.pallas_call(
        _mm_kernel,
        grid=(n // tn, m // tm),
        in_specs=[pl.BlockSpec((tm, k), lambda j, i: (i, 0)), pl.BlockSpec((k, tn), lambda j, i: (0, j))],
        out_specs=pl.BlockSpec((tm, tn), lambda j, i: (i, j)),
        out_shape=jax.ShapeDtypeStruct((m, n), F32),
        compiler_params=_cparams("parallel", "parallel"),
        name=name,
    )(x, w)


def _rope_kernel(q_ref, k_ref, cos_ref, sin_ref, qo_ref, ko_ref):
    cosf = cos_ref[...]
    sinf = sin_ref[...]
    lane = lax.broadcasted_iota(jnp.int32, cosf.shape, 1)
    first = lane < ROT_DIM // 2
    for src, dst in ((q_ref, qo_ref), (k_ref, ko_ref)):
        for hh in range(SWA_HEADS):
            sl = slice(hh * SWA_DH, (hh + 1) * SWA_DH)
            x = src[:, sl]
            partner = jnp.where(first, pltpu.roll(x, SWA_DH - ROT_DIM // 2, axis=1), pltpu.roll(x, ROT_DIM // 2, axis=1))
            dst[:, sl] = x * cosf + partner * sinf


def _rope(h_main, cosf, sinf, tm):
    m = h_main.shape[0]
    nt = cosf.shape[0] // tm
    qb, kb = COL_SQ // SWA_W, COL_SK // SWA_W
    return pl.pallas_call(
        _rope_kernel,
        grid=(m // tm, N_SWA),
        in_specs=[
            pl.BlockSpec((tm, SWA_W), lambda i, g: (i, qb + g)),
            pl.BlockSpec((tm, SWA_W), lambda i, g: (i, kb + g)),
            pl.BlockSpec((tm, SWA_DH), lambda i, g: (i % nt, 0)),
            pl.BlockSpec((tm, SWA_DH), lambda i, g: (i % nt, 0)),
        ],
        out_specs=[pl.BlockSpec((tm, SWA_W), lambda i, g: (i, g)), pl.BlockSpec((tm, SWA_W), lambda i, g: (i, g))],
        out_shape=[jax.ShapeDtypeStruct((m, SWA_QKV), F32), jax.ShapeDtypeStruct((m, SWA_QKV), F32)],
        compiler_params=_cparams("parallel", "parallel"),
        name="rope",
    )(h_main, h_main, cosf, sinf)


def _rope_split_kernel(*refs, dils, tm):
    ins, cos_ref, sin_ref = refs[:3 * N_SWA], refs[3 * N_SWA], refs[3 * N_SWA + 1]
    outs, slab = refs[3 * N_SWA + 2:6 * N_SWA + 2], refs[6 * N_SWA + 2]
    cosf = cos_ref[...]
    sinf = sin_ref[...]
    first = lax.broadcasted_iota(jnp.int32, cosf.shape, 1) < ROT_DIM // 2
    for gi, dil in enumerate(dils):
        for kind in range(3):
            src, dst = ins[3 * gi + kind], outs[3 * gi + kind]
            for hh in range(SWA_HEADS):
                sl = slice(hh * SWA_DH, (hh + 1) * SWA_DH)
                x = src[0, :, sl]
                if kind < 2:
                    partner = jnp.where(first, pltpu.roll(x, SWA_DH - ROT_DIM // 2, axis=1),
                                        pltpu.roll(x, ROT_DIM // 2, axis=1))
                    x = x * cosf + partner * sinf
                if dil == 1:
                    dst[0, 0, :, sl] = x
                else:
                    slab[hh] = x
                    for r in range(dil):
                        dst[0, r, :, sl] = slab[hh, pl.ds(r, tm // dil, stride=dil), :]


def _rope_split(h3, cosf, sinf, tm):
    bn, t, _ = h3.shape
    dils = tuple(d for _, d in SWA_GROUPS)
    in_specs, out_specs, out_shape = [], [], []
    for gi, dil in enumerate(dils):
        for off in (COL_SQ, COL_SK, COL_SV):
            cb = off // SWA_W + gi
            in_specs.append(pl.BlockSpec((1, tm, SWA_W), lambda b, i, cb=cb: (b, i, cb)))
            out_specs.append(pl.BlockSpec((1, dil, tm // dil, SWA_W), lambda b, i: (b, 0, i, 0)))
            out_shape.append(jax.ShapeDtypeStruct((bn, dil, t // dil, SWA_W), F32))
    tbl = pl.BlockSpec((tm, SWA_DH), lambda b, i: (i, 0))
    outs = pl.pallas_call(
        functools.partial(_rope_split_kernel, dils=dils, tm=tm),
        grid=(bn, t // tm),
        in_specs=in_specs + [tbl, tbl],
        out_specs=out_specs,
        out_shape=out_shape,
        scratch_shapes=[pltpu.VMEM((SWA_HEADS, tm, SWA_DH), F32)],
        compiler_params=_cparams("parallel", "parallel"),
        name="rope_split",
    )(*([h3] * (3 * N_SWA)), cosf, sinf)
    return [tuple(outs[3 * gi:3 * gi + 3]) for gi in range(N_SWA)]


def _rope_tables(pos):
    half = ROT_DIM // 2
    inv = ROPE_THETA ** (-jnp.arange(half, dtype=F32) * 2.0 / ROT_DIM)
    ang = pos.astype(F32)[:, None] * inv[None, :]
    cos, sin = jnp.cos(ang), jnp.sin(ang)
    n = pos.shape[0]
    cosf = jnp.concatenate([cos, cos, jnp.ones((n, SWA_DH - ROT_DIM), F32)], axis=1)
    sinf = jnp.concatenate([-sin, sin, jnp.zeros((n, SWA_DH - ROT_DIM), F32)], axis=1)
    return cosf, sinf


def _gdn_kernel(alog_ref, dtb_ref, q_ref, k_ref, v_ref, z_ref, ab_ref, cwq_ref, cwk_ref, cwv_ref,
                tq_ref, tk_ref, tv_ref, s0_ref, nw_ref, o_ref, sout_ref,
                s_scr, tailq, tailk, tailv, *, chunk, t_total, hps):
    c = pl.program_id(2)

    @pl.when(c == 0)
    def _():
        s_scr[...] = s0_ref[0]
        tailq[...] = tq_ref[0]
        tailk[...] = tk_ref[0]
        tailv[...] = tv_ref[0]

    C = chunk
    heads = range(hps)
    sls = [slice(hh * LANE, (hh + 1) * LANE) for hh in heads]
    hidx = [pl.program_id(1) * hps + hh for hh in heads]
    row8 = lax.broadcasted_iota(jnp.int32, (SUBLANE, LANE), 0)

    def conv(src_ref, tail_ref, w_ref, sl):
        raw = src_ref[0, :, sl]
        w = w_ref[:, sl]
        t8 = tail_ref[:, sl]
        y = raw * w[CONV_W - 1:CONV_W, :]
        for s in range(1, CONV_W):
            rolled = pltpu.roll(raw, s, axis=0)
            top = jnp.where(row8 < s, pltpu.roll(t8, s, axis=0), rolled[0:SUBLANE])
            sh = top if C == SUBLANE else jnp.concatenate([top, rolled[SUBLANE:]], axis=0)
            y = y + sh * w[CONV_W - 1 - s:CONV_W - s, :]
        tail_ref[:, sl] = raw[C - SUBLANE:C]
        return y * _sigmoid(y)

    def l2n(x):
        return x * lax.rsqrt(jnp.sum(x * x, axis=-1, keepdims=True) + EPS)

    q = [l2n(conv(q_ref, tailq, cwq_ref, sl)) * (GDN_DK ** -0.5) for sl in sls]
    k = [l2n(conv(k_ref, tailk, cwk_ref, sl)) for sl in sls]
    v = [conv(v_ref, tailv, cwv_ref, sl) for sl in sls]

    lane = lax.broadcasted_iota(jnp.int32, (C, LANE), 1)
    rowc = lax.broadcasted_iota(jnp.int32, (C, 1), 0)
    valid = (c * C + rowc) < t_total
    ab = ab_ref[0]
    beta, g = [], []
    for h in hidx:
        a = jnp.sum(jnp.where(lane == h, ab, 0.0), axis=-1, keepdims=True)
        b = jnp.sum(jnp.where(lane == h + GDN_HEADS, ab, 0.0), axis=-1, keepdims=True)
        sp_in = a + dtb_ref[h]
        softplus = jnp.maximum(sp_in, 0.0) + jnp.log(1.0 + jnp.exp(-jnp.abs(sp_in)))
        beta.append(jnp.where(valid, _sigmoid(b), 0.0))
        g.append(jnp.where(valid, -jnp.exp(jnp.full((C, 1), alog_ref[h], F32)) * softplus, 0.0))

    ri = lax.broadcasted_iota(jnp.int32, (C, C), 0)
    ci = lax.broadcasted_iota(jnp.int32, (C, C), 1)
    causal = ri >= ci
    strict = ri > ci
    tril = causal.astype(F32)
    e0 = (lane == 0).astype(F32)
    eye = (ri == ci).astype(F32)
    gc_b = [_hdot(tril, jnp.broadcast_to(gh, (C, LANE))) for gh in g]
    gc_row = [_hdot_nt(e0, gch) for gch in gc_b]
    decay = [jnp.where(causal, jnp.exp(jnp.where(causal, gch[:, :C] - grh, 0.0)), 0.0)
             for gch, grh in zip(gc_b, gc_row)]
    kb = [kh * bh for kh, bh in zip(k, beta)]
    xpow = [-jnp.where(strict, _dot3_nt(kbh, kh) * dh, 0.0) for kbh, kh, dh in zip(kb, k, decay)]
    tinv = [eye + xh for xh in xpow]
    for _ in range(int(math.log2(C)) - 1):
        xpow = [_dot3(xh, xh) for xh in xpow]
        tinv = [th + _dot3(th, xh) for th, xh in zip(tinv, xpow)]
    eg = [jnp.exp(gch) for gch in gc_b]
    value = [_dot3(th, vh * bh) for th, vh, bh in zip(tinv, v, beta)]
    kcd = [_dot3(th, kbh * egh) for th, kbh, egh in zip(tinv, kb, eg)]
    attn = [jnp.where(causal, _dot3_nt(qh, kh) * dh, 0.0) for qh, kh, dh in zip(q, k, decay)]
    glast = [gch[C - 1:C, :] for gch in gc_b]
    kdec = [kh * jnp.exp(glh - gch) for kh, glh, gch in zip(k, glast, gc_b)]

    s = [s_scr[hh] for hh in heads]
    v_new = [vh - _dot3(kh, sh) for vh, kh, sh in zip(value, kcd, s)]
    o = [_dot3(qh * egh, sh) + _dot3(ah, vnh) for qh, egh, sh, ah, vnh in zip(q, eg, s, attn, v_new)]
    for hh in heads:
        s_scr[hh] = s[hh] * jnp.exp(glast[hh]) + _dot3_tn(kdec[hh], v_new[hh])

    nw = nw_ref[...]
    for hh, sl in zip(heads, sls):
        z = z_ref[0, :, sl]
        on = o[hh] * lax.rsqrt(jnp.mean(o[hh] * o[hh], axis=-1, keepdims=True) + EPS) * nw
        o_ref[0, :, sl] = on * (z * _sigmoid(z))

    @pl.when(c == pl.num_programs(2) - 1)
    def _():
        sout_ref[0] = s_scr[...]


GDN_HEADS_PER_STEP = 8


def _gdn(h_main3, ab3, conv_w, tail8, s0, a_log, dt_bias, gdn_norm, chunk, t_total):
    bn, tp, _ = h_main3.shape
    nc = tp // chunk
    hps = GDN_HEADS_PER_STEP
    w = hps * LANE
    hq, hk, hv, hz = COL_Q // w, COL_K // w, COL_V // w, COL_Z // w
    cq, ck, cv = 0, GDN_QK // w, 2 * GDN_QK // w
    col = lambda off: pl.BlockSpec((1, chunk, w), lambda b, h, c: (b, c, off + h))
    cw = lambda off: pl.BlockSpec((CONV_W, w), lambda b, h, c: (0, off + h))
    tl = lambda off: pl.BlockSpec((1, SUBLANE, w), lambda b, h, c: (b, 0, off + h))
    smem = pl.BlockSpec(memory_space=pltpu.SMEM)
    kern = functools.partial(_gdn_kernel, chunk=chunk, t_total=t_total, hps=hps)
    return pl.pallas_call(
        kern,
        grid=(bn, GDN_HEADS // hps, nc),
        in_specs=[smem, smem, col(hq), col(hk), col(hv), col(hz),
                  pl.BlockSpec((1, chunk, LANE), lambda b, h, c: (b, c, 0)),
                  cw(cq), cw(ck), cw(cv), tl(cq), tl(ck), tl(cv),
                  pl.BlockSpec((1, hps, GDN_DK, GDN_DV), lambda b, h, c: (b, h, 0, 0)),
                  pl.BlockSpec((1, GDN_DV), lambda b, h, c: (0, 0))],
        out_specs=[pl.BlockSpec((1, chunk, w), lambda b, h, c: (b, c, h)),
                   pl.BlockSpec((1, hps, GDN_DK, GDN_DV), lambda b, h, c: (b, h, 0, 0))],
        out_shape=[jax.ShapeDtypeStruct((bn, tp, GDN_VW), F32),
                   jax.ShapeDtypeStruct((bn, GDN_HEADS, GDN_DK, GDN_DV), F32)],
        scratch_shapes=[pltpu.VMEM((hps, GDN_DK, GDN_DV), F32)] + [pltpu.VMEM((SUBLANE, w), F32)] * 3,
        compiler_params=_cparams("parallel", "parallel", "arbitrary"),
        name="gdn",
    )(a_log, dt_bias, h_main3, h_main3, h_main3, h_main3, ab3, conv_w, conv_w, conv_w,
      tail8, tail8, tail8, s0, gdn_norm.reshape(1, GDN_DV))


SWA_WIN_BLK = 128


def _swa_kernel(q_ref, kp_ref, kc_ref, vp_ref, vc_ref, o_ref, lse_ref, *, nsub):
    qi = pl.program_id(2)
    n = SWA_WIN_BLK
    ri = lax.broadcasted_iota(jnp.int32, (n, n), 0)
    ci = lax.broadcasted_iota(jnp.int32, (n, n), 1)
    mask_cur = ci <= ri
    scale = SWA_DH ** -0.5
    for sb in range(nsub):
        rows = slice(sb * n, (sb + 1) * n)
        prows = slice((sb - 1) * n, sb * n)
        mask_prev = jnp.logical_and(ci >= ri, qi > 0) if sb == 0 else ci >= ri
        for hh in range(SWA_HEADS):
            sl = slice(hh * SWA_DH, (hh + 1) * SWA_DH)
            kp = kp_ref[0, 0, :, sl] if sb == 0 else kc_ref[0, 0, prows, sl]
            vp = vp_ref[0, 0, :, sl] if sb == 0 else vc_ref[0, 0, prows, sl]
            q = q_ref[0, 0, rows, sl].astype(BF16)
            sp = jnp.where(mask_prev, _dot_nt(q, kp.astype(BF16)) * scale, NEG_BIG)
            sc = jnp.where(mask_cur, _dot_nt(q, kc_ref[0, 0, rows, sl].astype(BF16)) * scale, NEG_BIG)
            m = jnp.maximum(jnp.max(sp, axis=-1, keepdims=True), jnp.max(sc, axis=-1, keepdims=True))
            pp = jnp.exp(sp - m)
            pc = jnp.exp(sc - m)
            ssum = jnp.sum(pp, axis=-1, keepdims=True) + jnp.sum(pc, axis=-1, keepdims=True)
            acc = (jnp.dot(pp.astype(BF16), vp.astype(BF16), preferred_element_type=F32)
                   + jnp.dot(pc.astype(BF16), vc_ref[0, 0, rows, sl].astype(BF16), preferred_element_type=F32))
            o_ref[0, 0, rows, sl] = acc / ssum
            lse_ref[0, 0, rows, sl] = jnp.broadcast_to(m + jnp.log(ssum), (n, SWA_DH))


def _swa_prompt_group(q, k, v, gi, nsub):
    bn, dil, tl, _ = q.shape
    qblk = nsub * SWA_WIN_BLK
    cur = lambda b, r, i: (b, r, i, 0)
    prev = lambda b, r, i: (b, r, jnp.maximum(i * nsub - 1, 0), 0)
    blk = (1, 1, qblk, SWA_W)
    pblk = (1, 1, SWA_WIN_BLK, SWA_W)
    return pl.pallas_call(
        functools.partial(_swa_kernel, nsub=nsub),
        grid=(bn, dil, tl // qblk),
        in_specs=[pl.BlockSpec(blk, cur), pl.BlockSpec(pblk, prev), pl.BlockSpec(blk, cur),
                  pl.BlockSpec(pblk, prev), pl.BlockSpec(blk, cur)],
        out_specs=[pl.BlockSpec(blk, cur)] * 2,
        out_shape=[jax.ShapeDtypeStruct(q.shape, F32)] * 2,
        compiler_params=_cparams("parallel", "parallel", "arbitrary"),
        name=f"swa_prompt_g{gi}",
    )(q, k, k, v, v)


def _swa_sample_kernel(q_ref, kn_ref, vn_ref, c0_ref, c1_ref, c2_ref, *out_refs):
    scale = SWA_DH ** -0.5
    for gi, c_ref in enumerate((c0_ref, c1_ref, c2_ref)):
        o_ref, lse_ref = out_refs[2 * gi], out_refs[2 * gi + 1]
        q = q_ref[0, gi]
        kn = kn_ref[0, gi]
        vn = vn_ref[0, gi]
        kc = c_ref[0, :, 0, 0]
        vc = c_ref[0, :, 0, 1]
        s = jnp.sum(kc * q[None], axis=-1, keepdims=True) * scale
        sn = jnp.sum(kn * q, axis=-1, keepdims=True) * scale
        m = jnp.maximum(jnp.max(s, axis=0), sn)
        p = jnp.exp(s - m[None])
        pn = jnp.exp(sn - m)
        den = jnp.sum(p, axis=0) + pn
        o_ref[0] = (jnp.sum(p * vc, axis=0) + pn * vn) / den
        lse_ref[0] = jnp.broadcast_to(m + jnp.log(den), (SWA_HEADS, SWA_DH))


def _swa_sample(q_r, k_r, v_new, caches):
    bn = q_r.shape[0]
    nkeys = SWA_GROUPS[0][0] // SWA_GROUPS[0][1]
    views = []
    for (win, dil), cache in zip(SWA_GROUPS, caches):
        assert cache.shape[1] == win and win // dil == nkeys
        views.append(cache.reshape(bn, nkeys, dil, 2, SWA_HEADS, SWA_DH))
    heads = lambda a: a.reshape(bn, N_SWA, SWA_HEADS, SWA_DH)
    row = pl.BlockSpec((1, N_SWA, SWA_HEADS, SWA_DH), lambda b: (b, 0, 0, 0))
    cspec = pl.BlockSpec((1, nkeys, 1, 2, SWA_HEADS, SWA_DH), lambda b: (b, 0, 0, 0, 0, 0))
    ospec = pl.BlockSpec((1, SWA_HEADS, SWA_DH), lambda b: (b, 0, 0))
    outs = pl.pallas_call(
        _swa_sample_kernel,
        grid=(bn,),
        in_specs=[row, row, row, cspec, cspec, cspec],
        out_specs=[ospec] * (2 * N_SWA),
        out_shape=[jax.ShapeDtypeStruct((bn, SWA_HEADS, SWA_DH), F32)] * (2 * N_SWA),
        compiler_params=_cparams("parallel"),
        name="swa_sample",
    )(heads(q_r), heads(k_r), heads(v_new), *views)
    return [(outs[2 * g].reshape(bn, SWA_W), outs[2 * g + 1].reshape(bn, SWA_W)) for g in range(N_SWA)]


def _mix_kernel(oa_ref, o0_ref, l0_ref, o1_ref, l1_ref, o2_ref, l2_ref, ga_ref, gb_ref, x_ref,
                wa_ref, wb_ref, wo_ref, nw_ref, x1_ref, xn_ref, slab, *, dils, tm):
    def token_major(ref, dil):
        if dil == 1:
            return ref[0, 0]
        cols = []
        for hh in range(SWA_HEADS):
            for r in range(dil):
                slab[hh, pl.ds(r, tm // dil, stride=dil), :] = ref[0, r, :, hh * SWA_DH:(hh + 1) * SWA_DH]
            cols.append(slab[hh])
        return jnp.concatenate(cols, axis=1)

    os = [token_major(r, d) for r, d in zip((o0_ref, o1_ref, o2_ref), dils)]
    ls = [token_major(r, d) for r, d in zip((l0_ref, l1_ref, l2_ref), dils)]
    lmax = jnp.maximum(jnp.maximum(ls[0], ls[1]), ls[2])
    es = [jnp.exp(l - lmax) for l in ls]
    ob = (es[0] * os[0] + es[1] * os[1] + es[2] * os[2]) / (es[0] + es[1] + es[2])
    pa = jnp.dot(oa_ref[0].astype(BF16), wa_ref[...], preferred_element_type=F32)
    pb = jnp.dot(ob.astype(BF16), wb_ref[...], preferred_element_type=F32)
    merged = _sigmoid(ga_ref[0]) * pa + _sigmoid(gb_ref[0]) * pb
    x1 = x_ref[0] + jnp.dot(merged.astype(BF16), wo_ref[...], preferred_element_type=F32)
    x1_ref[0] = x1
    xn_ref[0] = x1 * lax.rsqrt(jnp.mean(x1 * x1, axis=-1, keepdims=True) + EPS) * nw_ref[...]


def _mix(o_a3, parts, dils, h3, x3, wa, wb, wo, norm_ffn, tm):
    bn, t, _ = x3.shape
    rowblk = lambda w, cb=0: pl.BlockSpec((1, tm, w), lambda b, i: (b, i, cb))
    full = lambda a: pl.BlockSpec(a.shape, lambda b, i: (0, 0))
    nw = norm_ffn.reshape(1, D_MODEL)
    part_specs, flat = [], []
    for (o, lse), dil in zip(parts, dils):
        spec = pl.BlockSpec((1, dil, tm // dil, SWA_W), lambda b, i: (b, 0, i, 0))
        part_specs += [spec, spec]
        flat += [o, lse]
    return pl.pallas_call(
        functools.partial(_mix_kernel, dils=dils, tm=tm),
        grid=(bn, t // tm),
        in_specs=[rowblk(GDN_VW)] + part_specs
                 + [rowblk(D_MODEL, COL_GA // D_MODEL), rowblk(D_MODEL, COL_GB // D_MODEL),
                    rowblk(D_MODEL), full(wa), full(wb), full(wo), full(nw)],
        out_specs=[rowblk(D_MODEL), rowblk(D_MODEL)],
        out_shape=[jax.ShapeDtypeStruct((bn, t, D_MODEL), F32)] * 2,
        scratch_shapes=[pltpu.VMEM((SWA_HEADS, tm, SWA_DH), F32)],
        compiler_params=_cparams("parallel", "parallel"),
        name="mix_out",
    )(o_a3, *flat, h3, h3, x3, wa, wb, wo, nw)


def _topk_rows(vals, k, payload=None):
    n, t = vals.shape
    idx = lax.broadcasted_iota(jnp.int32, (n, t), 0)
    out_v, out_i = [], []
    for _ in range(k):
        m = jnp.max(vals, axis=0, keepdims=True)
        am = jnp.min(jnp.where(vals == m, idx, n), axis=0, keepdims=True)
        hit = idx == am
        out_v.append(m)
        out_i.append(am if payload is None else jnp.sum(jnp.where(hit, payload, 0), axis=0, keepdims=True))
        vals = jnp.where(hit, -jnp.inf, vals)
    return jnp.concatenate(out_v, axis=0), jnp.concatenate(out_i, axis=0)


def _route_kernel(q_ref, keys_ref, eid_ref, gate_ref):
    half = PEER_DKEY // 2
    eids, gates = [], []
    for h in range(PEER_HEADS):
        tops = []
        for p in range(2):
            qs = q_ref[:, (2 * h + p) * half:(2 * h + p + 1) * half].astype(BF16)
            st = _dot_nt(keys_ref[h, p].astype(BF16), qs)
            tops.append(_topk_rows(st, PEER_TOPK))
        (v1, i1), (v2, i2) = tops
        sub = SUBLANE
        assert PEER_TOPK == 2 * sub
        rows = [(slice(0, 1), slice(0, PEER_TOPK))]
        rows += [(slice(i, i + 1), slice(0, sub)) for i in range(1, sub)]
        rows += [(slice(sub, PEER_TOPK), slice(0, 1))]
        cand = jnp.concatenate([v1[a] + v2[b] for a, b in rows], axis=0)
        ecand = jnp.concatenate([i1[a] * PEER_NKEYS + i2[b] for a, b in rows], axis=0)
        sc, e = _topk_rows(cand, PEER_TOPK, payload=ecand)
        ex = jnp.exp(sc - sc[0:1])
        gates.append(ex / jnp.sum(ex, axis=0, keepdims=True))
        eids.append(e)
    eid_ref[...] = jnp.concatenate(eids, axis=0).T
    gate_ref[...] = jnp.concatenate(gates, axis=0)


def _route(qp, sub_keys, tt):
    m = qp.shape[0]
    return pl.pallas_call(
        _route_kernel,
        grid=(m // tt,),
        in_specs=[pl.BlockSpec((tt, PEER_HEADS * PEER_DKEY), lambda i: (i, 0)),
                  pl.BlockSpec(sub_keys.shape, lambda i: (0, 0, 0, 0))],
        out_specs=[pl.BlockSpec((tt, PEER_SEL), lambda i: (i, 0)), pl.BlockSpec((PEER_SEL, tt), lambda i: (0, i))],
        out_shape=[jax.ShapeDtypeStruct((m, PEER_SEL), jnp.int32), jax.ShapeDtypeStruct((PEER_SEL, m), F32)],
        compiler_params=_cparams("parallel"),
        name="peer_route",
    )(qp, sub_keys)


PEER_NBUF = 8


PEER_CHUNKS = D_MODEL // LANE


def _pack_kernel(d_ref, u_ref, o_ref):
    hi = lax.bitcast_convert_type(d_ref[...].astype(BF16).astype(F32), jnp.uint32)
    lo = lax.bitcast_convert_type(u_ref[...].astype(BF16).astype(F32), jnp.uint32) >> 16
    word = hi | lo
    for c in range(PEER_CHUNKS):
        o_ref[:, c, :] = word[:, c * LANE:(c + 1) * LANE]


def _pack_peer_table(expert_down, expert_up, tr=256):
    e, d = expert_down.shape
    return pl.pallas_call(
        _pack_kernel,
        grid=(e // tr,),
        in_specs=[pl.BlockSpec((tr, d), lambda i: (i, 0))] * 2,
        out_specs=pl.BlockSpec((tr, PEER_CHUNKS, LANE), lambda i: (i, 0, 0)),
        out_shape=jax.ShapeDtypeStruct((e, PEER_CHUNKS, LANE), jnp.uint32),
        compiler_params=_cparams("parallel"),
        name="peer_pack",
    )(expert_down, expert_up)


def _peer_kernel(eid_ref, gate_ref, xn_ref, x1_ref, nw_ref, tbl_hbm, y_ref, tbuf, sem, xrow, yrow, *, n_tok):
    nbuf = PEER_NBUF

    def issue(t, slot, rows):
        for j in rows:
            pltpu.make_async_copy(tbl_hbm.at[eid_ref[t, j]], tbuf.at[slot, :, j, :], sem.at[slot]).start(priority=j % 2)

    def wait(slot):
        pltpu.make_async_copy(tbuf.at[slot], tbuf.at[slot], sem.at[slot]).wait()

    lane = lax.broadcasted_iota(jnp.int32, (PEER_SEL, gate_ref.shape[1]), 1)
    hi_mask = jnp.uint32(0xFFFF0000)
    per_chunk = 3
    mid_rows = range(2 * PEER_CHUNKS * per_chunk, PEER_SEL)

    def compute(t, slot, t_ahead):
        def issue_rows(rows):
            if t_ahead is not None:
                issue(t_ahead, (slot - 1) % nbuf, rows)

        xrow[...] = xn_ref[pl.ds(t, 1), :]
        acc = jnp.zeros((PEER_SEL, LANE), F32)
        for c in range(PEER_CHUNKS):
            down = lax.bitcast_convert_type(tbuf[slot, c] & hi_mask, F32)
            acc = acc + down * xrow[:, c * LANE:(c + 1) * LANE]
            issue_rows(range(c * per_chunk, (c + 1) * per_chunk))
        act = jnp.sum(acc, axis=-1, keepdims=True)
        gate = jnp.sum(jnp.where(lane == t, gate_ref[...], 0.0), axis=-1, keepdims=True)
        issue_rows(mid_rows)
        gelu = 0.5 * act * (1.0 + lax.erf(act * (2.0 ** -0.5)))
        w = gate * gelu
        for c in range(PEER_CHUNKS):
            up = lax.bitcast_convert_type(tbuf[slot, c] << 16, F32)
            yrow[:, c * LANE:(c + 1) * LANE] = jnp.sum(up * w, axis=0, keepdims=True)
            issue_rows(range((PEER_CHUNKS + c) * per_chunk, (PEER_CHUNKS + c + 1) * per_chunk))
        y_ref[pl.ds(t, 1), :] = x1_ref[pl.ds(t, 1), :] + yrow[...]

    assert n_tok % nbuf == 0
    for k in range(nbuf - 1):
        issue(k, k, range(PEER_SEL))

    def group(p, carry):
        for k in range(nbuf):
            t = nbuf * p + k
            wait(k)
            compute(t, k, t + nbuf - 1)
        return carry

    lax.fori_loop(0, n_tok // nbuf - 1, group, 0)
    for k in range(nbuf):
        t = n_tok - nbuf + k
        wait(k)
        compute(t, k, t + nbuf - 1 if k == 0 else None)

    x2 = y_ref[...]
    y_ref[...] = x2 * lax.rsqrt(jnp.mean(x2 * x2, axis=-1, keepdims=True) + EPS) * nw_ref[...]


def _peer(eid, gates_t, xn2, x1, norm_final, table, tt, n_tok):
    m = xn2.shape[0]
    kern = functools.partial(_peer_kernel, n_tok=n_tok)
    return pl.pallas_call(
        kern,
        grid=(m // n_tok,),
        in_specs=[pl.BlockSpec((tt, PEER_SEL), lambda i: (i, 0), memory_space=pltpu.SMEM),
                  pl.BlockSpec((PEER_SEL, tt), lambda i: (0, i)),
                  pl.BlockSpec((n_tok, D_MODEL), lambda i: (i, 0)),
                  pl.BlockSpec((n_tok, D_MODEL), lambda i: (i, 0)),
                  pl.BlockSpec((1, D_MODEL), lambda i: (0, 0)),
                  pl.BlockSpec(memory_space=pl.ANY)],
        out_specs=pl.BlockSpec((n_tok, D_MODEL), lambda i: (i, 0)),
        out_shape=jax.ShapeDtypeStruct((m, D_MODEL), F32),
        scratch_shapes=[pltpu.VMEM((PEER_NBUF, PEER_CHUNKS, PEER_SEL, LANE), jnp.uint32),
                        pltpu.SemaphoreType.DMA((PEER_NBUF,)),
                        pltpu.VMEM((1, D_MODEL), F32), pltpu.VMEM((1, D_MODEL), F32)],
        compiler_params=_cparams("arbitrary"),
        name="peer_experts",
    )(eid, gates_t, xn2, x1, norm_final.reshape(1, D_MODEL), table)


def _pick(m, candidates):
    for c in candidates:
        if m % c == 0:
            return c
    return m


def _layer(x, pos, tail8, s0, caches, w, chunk):
    bn, t, _ = x.shape
    m = bn * t
    x2d = x.reshape(m, D_MODEL)
    tm_big = _pick(m, (1024, 512, 256, 128, 64, 32))

    xn = _rmsnorm(x2d, w["norm_mix"], _pick(m, (512, 256, 128, 64, 32)), BF16)
    h_main = _matmul(xn, w["w_main"], tm_big, 640, "proj_in")
    h_ab = _matmul(xn, w["w_ab"], tm_big, LANE, "proj_ab")

    tp = -(-t // chunk) * chunk
    h3 = h_main.reshape(bn, t, N_MAIN)
    ab3 = h_ab.reshape(bn, t, LANE)
    h3p = h3 if tp == t else jnp.pad(h3, ((0, 0), (0, tp - t), (0, 0)))
    ab3p = ab3 if tp == t else jnp.pad(ab3, ((0, 0), (0, tp - t), (0, 0)))
    o_a, s_new = _gdn(h3p, ab3p, w["conv_w"], tail8, s0, w["a_log"], w["dt_bias"], w["gdn_norm"], chunk, t)
    o_a = o_a[:, :t]

    cosf, sinf = _rope_tables(pos)
    kv_new = []
    if caches is None:
        qkv = _rope_split(h3, cosf, sinf, 256)
        parts = [_swa_prompt_group(q, k, v, gi, 2 if q.shape[2] % (2 * SWA_WIN_BLK) == 0 else 1)
                 for gi, (q, k, v) in enumerate(qkv)]
        dils = tuple(d for _, d in SWA_GROUPS)
        for (win, dil), (_, k, v) in zip(SWA_GROUPS, qkv):
            keep = min(win, t)
            tok = lambda a: (a[:, :, (t - keep) // dil:].transpose(0, 2, 1, 3)
                             .reshape(bn, keep, SWA_HEADS, SWA_DH))
            kv_new.append(jnp.stack([tok(k), tok(v)], axis=2))
        mix_shape = (bn, t)
    else:
        cosf, sinf = jnp.tile(cosf, (bn, 1)), jnp.tile(sinf, (bn, 1))
        q_r, k_r = _rope(h_main, cosf, sinf, _pick(m, (512, 256, 128, 64, 32)))
        v_new = h_main[:, COL_SV:COL_SV + SWA_QKV]
        parts = [(o.reshape(1, 1, m, SWA_W), lse.reshape(1, 1, m, SWA_W))
                 for o, lse in _swa_sample(q_r, k_r, v_new, caches)]
        dils = (1,) * N_SWA
        for gi in range(N_SWA):
            kk = k_r[:, gi * SWA_W:(gi + 1) * SWA_W].reshape(bn, t, SWA_HEADS, SWA_DH)
            vv = v_new[:, gi * SWA_W:(gi + 1) * SWA_W].reshape(bn, t, SWA_HEADS, SWA_DH)
            kv_new.append(jnp.stack([kk, vv], axis=2))
        mix_shape = (1, m)

    mb, mt = mix_shape
    x1, xn2 = _mix(o_a.reshape(mb, mt, GDN_VW), parts, dils, h_main.reshape(mb, mt, N_MAIN),
                   x.reshape(mb, mt, D_MODEL), w["w_br_a"], w["w_br_b"], w["w_out"], w["norm_ffn"],
                   _pick(mt, (256, 128, 64, 32)))
    x1, xn2 = x1.reshape(m, D_MODEL), xn2.reshape(m, D_MODEL)

    mp = -(-m // LANE) * LANE
    xq = xn2 if mp == m else jnp.pad(xn2, ((0, mp - m), (0, 0)))
    qp = _matmul(xq, w["w_query"], _pick(mp, (1024, 512, 256, 128)), 512, "peer_query")
    eid, gates_t = _route(qp, w["sub_keys"], LANE)
    n_tok = min(m, LANE)
    y = _peer(eid, gates_t, xn2, x1, w["norm_final"], w["peer_table"], LANE, n_tok)
    return y.reshape(bn, t, D_MODEL), s_new, h3, kv_new


def kernel(x_prompt, x_sample, state_gdn, state_conv, cache_kv_w128, cache_kv_w512, cache_kv_w2048,
           norm_mix, w_in, conv_w, a_log, dt_bias, gdn_norm, w_br_a, w_br_b, w_out, norm_ffn,
           w_query, sub_keys, expert_down, expert_up, norm_final):
    w_main = jnp.concatenate([w_in[:, IN_GA:], w_in[:, :IN_AB], w_in[:, IN_SQ:IN_GA]], axis=1).astype(BF16)
    w_ab = jnp.pad(w_in[:, IN_AB:IN_SQ], ((0, 0), (0, LANE - 2 * GDN_HEADS))).astype(BF16)
    w = dict(norm_mix=norm_mix, w_main=w_main, w_ab=w_ab, conv_w=conv_w, a_log=a_log, dt_bias=dt_bias,
             gdn_norm=gdn_norm, w_br_a=w_br_a.astype(BF16), w_br_b=w_br_b.astype(BF16), w_out=w_out.astype(BF16),
             norm_ffn=norm_ffn, w_query=w_query.astype(BF16), sub_keys=sub_keys,
             peer_table=_pack_peer_table(expert_down, expert_up), norm_final=norm_final)

    bp, tlen = x_prompt.shape[:2]
    bs, ts = x_sample.shape[:2]
    caches = (cache_kv_w128, cache_kv_w512, cache_kv_w2048)

    tail8 = jnp.concatenate([jnp.zeros((bs, SUBLANE - (CONV_W - 1), GDN_CONV_CH), F32), state_conv], axis=1)
    y_s, gdn_s, h3_s, kv_new = _layer(
        x_sample, PAST_LEN + jnp.arange(ts, dtype=jnp.int32), tail8, state_gdn, caches, w, SUBLANE)
    conv_s = jnp.concatenate([state_conv, h3_s[:, :, COL_Q:COL_Q + GDN_CONV_CH]], axis=1)[:, ts:]
    kv_s = [jnp.concatenate([cache, new.astype(cache.dtype)], axis=1)[:, ts:] for cache, new in zip(caches, kv_new)]

    y_p, gdn_p, h3_p, kv_p = _layer(
        x_prompt, jnp.arange(tlen, dtype=jnp.int32),
        jnp.zeros((bp, SUBLANE, GDN_CONV_CH), F32), jnp.zeros((bp, GDN_HEADS, GDN_DK, GDN_DV), F32),
        None, w, CHUNK)
    conv_p = h3_p[:, tlen - (CONV_W - 1):, COL_Q:COL_Q + GDN_CONV_CH]

    return (y_p, y_s, gdn_p.astype(x_prompt.dtype), conv_p, kv_p[0], kv_p[1], kv_p[2],
            gdn_s.astype(state_gdn.dtype), conv_s, kv_s[0], kv_s[1], kv_s[2])
```

```python
import functools
import math

import jax
import jax.numpy as jnp
import numpy as np
from jax import lax
from jax.experimental import pallas as pl
from jax.experimental.pallas import tpu as pltpu

F32 = jnp.float32
BF16 = jnp.bfloat16

LANE = 128
SUBLANE = 8
VMEM_LIMIT = 56 * 1024 * 1024

D_MODEL = 2048
PAST_LEN = 16384
EPS = 1e-6
GDN_HEADS = 8
GDN_DK = 128
GDN_DV = 128
GDN_QK = GDN_HEADS * GDN_DK
GDN_VW = GDN_HEADS * GDN_DV
GDN_CONV_CH = 2 * GDN_QK + GDN_VW
CONV_W = 4
CHUNK = 64
SWA_GROUPS = ((128, 1), (512, 4), (2048, 16))
N_SWA = 3
SWA_HEADS = 4
SWA_DH = 128
SWA_W = SWA_HEADS * SWA_DH
SWA_QKV = N_SWA * SWA_W
ROT_DIM = SWA_DH // 4
ROPE_THETA = 500000.0
PEER_HEADS = 8
PEER_NKEYS = 128
PEER_DKEY = 256
PEER_TOPK = 16
PEER_SEL = PEER_HEADS * PEER_TOPK

COL_GA = 0
COL_GB = D_MODEL
COL_Q = 2 * D_MODEL
COL_K = COL_Q + GDN_QK
COL_V = COL_K + GDN_QK
COL_Z = COL_V + GDN_VW
COL_SQ = COL_Z + GDN_VW
COL_SK = COL_SQ + SWA_QKV
COL_SV = COL_SK + SWA_QKV
N_MAIN = COL_SV + SWA_QKV
IN_AB = 2 * GDN_QK + 2 * GDN_VW
IN_SQ = IN_AB + 2 * GDN_HEADS
IN_GA = IN_SQ + 3 * SWA_QKV
NEG_BIG = -1e30


def _cparams(*sem):
    return pltpu.CompilerParams(dimension_semantics=sem, vmem_limit_bytes=VMEM_LIMIT)


def _sigmoid(x):
    return 1.0 / (1.0 + jnp.exp(-x))


def _dot_sel(sel, b, dims=(((1,), (0,)), ((), ()))):
    b1 = b.astype(BF16)
    r1 = b - b1.astype(F32)
    b2 = r1.astype(BF16)
    b3 = (r1 - b2.astype(F32)).astype(BF16)
    s = sel.astype(BF16)
    dot = lambda y: lax.dot_general(s, y, dims, preferred_element_type=F32)
    return dot(b1) + (dot(b2) + dot(b3))


def _split_bf16(a):
    hi = a.astype(BF16)
    return hi, (a - hi.astype(F32)).astype(BF16)


def _dot3(a, b, dims=(((1,), (0,)), ((), ()))):
    a_hi, a_lo = _split_bf16(a)
    b_hi, b_lo = _split_bf16(b)
    dot = lambda x, y: lax.dot_general(x, y, dims, preferred_element_type=F32)
    return dot(a_hi, b_hi) + (dot(a_hi, b_lo) + dot(a_lo, b_hi))


def _dot3_nt(a, b):
    return _dot3(a, b, (((1,), (1,)), ((), ())))


def _dot3_tn(a, b):
    return _dot3(a, b, (((0,), (0,)), ((), ())))


def _dot_nt(a, b):
    return lax.dot_general(a, b, (((1,), (1,)), ((), ())), preferred_element_type=F32)


def _rmsnorm_kernel(x_ref, w_ref, o_ref):
    x = x_ref[...]
    y = x * lax.rsqrt(jnp.mean(x * x, axis=-1, keepdims=True) + EPS) * w_ref[...]
    o_ref[...] = y.astype(o_ref.dtype)


def _rmsnorm(x, w, tm, out_dtype):
    m, d = x.shape
    return pl.pallas_call(
        _rmsnorm_kernel,
        grid=(m // tm,),
        in_specs=[pl.BlockSpec((tm, d), lambda i: (i, 0)), pl.BlockSpec((1, d), lambda i: (0, 0))],
        out_specs=pl.BlockSpec((tm, d), lambda i: (i, 0)),
        out_shape=jax.ShapeDtypeStruct((m, d), out_dtype),
        compiler_params=_cparams("parallel"),
        name="rmsnorm",
    )(x, w.reshape(1, d))


def _mm_kernel(x_ref, w_ref, o_ref):
    o_ref[...] = jnp.dot(x_ref[...].astype(BF16), w_ref[...], preferred_element_type=F32)


def _matmul(x, w, tm, tn, name):
    m, k = x.shape
    n = w.shape[1]
    return pl.pallas_call(
        _mm_kernel,
        grid=(n // tn, m // tm),
        in_specs=[pl.BlockSpec((tm, k), lambda j, i: (i, 0)), pl.BlockSpec((k, tn), lambda j, i: (0, j))],
        out_specs=pl.BlockSpec((tm, tn), lambda j, i: (i, j)),
        out_shape=jax.ShapeDtypeStruct((m, n), F32),
        compiler_params=_cparams("parallel", "parallel"),
        name=name,
    )(x, w)


def _rope_kernel(q_ref, k_ref, cos_ref, sin_ref, qo_ref, ko_ref):
    cosf = cos_ref[...]
    sinf = sin_ref[...]
    lane = lax.broadcasted_iota(jnp.int32, cosf.shape, 1)
    first = lane < ROT_DIM // 2
    for src, dst in ((q_ref, qo_ref), (k_ref, ko_ref)):
        for hh in range(SWA_HEADS):
            sl = slice(hh * SWA_DH, (hh + 1) * SWA_DH)
            x = src[:, sl]
            partner = jnp.where(first, pltpu.roll(x, SWA_DH - ROT_DIM // 2, axis=1), pltpu.roll(x, ROT_DIM // 2, axis=1))
            dst[:, sl] = x * cosf + partner * sinf


def _rope(h_main, cosf, sinf, tm):
    m = h_main.shape[0]
    nt = cosf.shape[0] // tm
    qb, kb = COL_SQ // SWA_W, COL_SK // SWA_W
    return pl.pallas_call(
        _rope_kernel,
        grid=(m // tm, N_SWA),
        in_specs=[
            pl.BlockSpec((tm, SWA_W), lambda i, g: (i, qb + g)),
            pl.BlockSpec((tm, SWA_W), lambda i, g: (i, kb + g)),
            pl.BlockSpec((tm, SWA_DH), lambda i, g: (i % nt, 0)),
            pl.BlockSpec((tm, SWA_DH), lambda i, g: (i % nt, 0)),
        ],
        out_specs=[pl.BlockSpec((tm, SWA_W), lambda i, g: (i, g)), pl.BlockSpec((tm, SWA_W), lambda i, g: (i, g))],
        out_shape=[jax.ShapeDtypeStruct((m, SWA_QKV), F32), jax.ShapeDtypeStruct((m, SWA_QKV), F32)],
        compiler_params=_cparams("parallel", "parallel"),
        name="rope",
    )(h_main, h_main, cosf, sinf)


def _rope_split_kernel(*refs, dils, tm):
    ins, cos_ref, sin_ref = refs[:3 * N_SWA], refs[3 * N_SWA], refs[3 * N_SWA + 1]
    outs, slab = refs[3 * N_SWA + 2:6 * N_SWA + 2], refs[6 * N_SWA + 2]
    cosf = cos_ref[...]
    sinf = sin_ref[...]
    first = lax.broadcasted_iota(jnp.int32, cosf.shape, 1) < ROT_DIM // 2
    for gi, dil in enumerate(dils):
        for kind in range(3):
            src, dst = ins[3 * gi + kind], outs[3 * gi + kind]
            for hh in range(SWA_HEADS):
                sl = slice(hh * SWA_DH, (hh + 1) * SWA_DH)
                x = src[0, :, sl]
                if kind < 2:
                    partner = jnp.where(first, pltpu.roll(x, SWA_DH - ROT_DIM // 2, axis=1),
                                        pltpu.roll(x, ROT_DIM // 2, axis=1))
                    x = x * cosf + partner * sinf
                if dil == 1:
                    dst[0, 0, :, sl] = x
                else:
                    slab[hh] = x
                    for r in range(dil):
                        dst[0, r, :, sl] = slab[hh, pl.ds(r, tm // dil, stride=dil), :]


def _rope_split(h3, cosf, sinf, tm):
    bn, t, _ = h3.shape
    dils = tuple(d for _, d in SWA_GROUPS)
    in_specs, out_specs, out_shape = [], [], []
    for gi, dil in enumerate(dils):
        for off in (COL_SQ, COL_SK, COL_SV):
            cb = off // SWA_W + gi
            in_specs.append(pl.BlockSpec((1, tm, SWA_W), lambda b, i, cb=cb: (b, i, cb)))
            out_specs.append(pl.BlockSpec((1, dil, tm // dil, SWA_W), lambda b, i: (b, 0, i, 0)))
            out_shape.append(jax.ShapeDtypeStruct((bn, dil, t // dil, SWA_W), F32))
    tbl = pl.BlockSpec((tm, SWA_DH), lambda b, i: (i, 0))
    outs = pl.pallas_call(
        functools.partial(_rope_split_kernel, dils=dils, tm=tm),
        grid=(bn, t // tm),
        in_specs=in_specs + [tbl, tbl],
        out_specs=out_specs,
        out_shape=out_shape,
        scratch_shapes=[pltpu.VMEM((SWA_HEADS, tm, SWA_DH), F32)],
        compiler_params=_cparams("parallel", "parallel"),
        name="rope_split",
    )(*([h3] * (3 * N_SWA)), cosf, sinf)
    return [tuple(outs[3 * gi:3 * gi + 3]) for gi in range(N_SWA)]


def _rope_tables(pos):
    half = ROT_DIM // 2
    inv = ROPE_THETA ** (-jnp.arange(half, dtype=F32) * 2.0 / ROT_DIM)
    ang = pos.astype(F32)[:, None] * inv[None, :]
    cos, sin = jnp.cos(ang), jnp.sin(ang)
    n = pos.shape[0]
    cosf = jnp.concatenate([cos, cos, jnp.ones((n, SWA_DH - ROT_DIM), F32)], axis=1)
    sinf = jnp.concatenate([-sin, sin, jnp.zeros((n, SWA_DH - ROT_DIM), F32)], axis=1)
    return cosf, sinf


def _gdn_kernel(alog_ref, dtb_ref, q_ref, k_ref, v_ref, z_ref, ab_ref, cwq_ref, cwk_ref, cwv_ref,
                tq_ref, tk_ref, tv_ref, s0_ref, nw_ref, o_ref, sout_ref,
                s_scr, tailq, tailk, tailv, *, chunk, t_total, hps):
    c = pl.program_id(2)

    @pl.when(c == 0)
    def _():
        s_scr[...] = s0_ref[0]
        tailq[...] = tq_ref[0]
        tailk[...] = tk_ref[0]
        tailv[...] = tv_ref[0]

    C = chunk
    heads = range(hps)
    sls = [slice(hh * LANE, (hh + 1) * LANE) for hh in heads]
    hidx = [pl.program_id(1) * hps + hh for hh in heads]
    row8 = lax.broadcasted_iota(jnp.int32, (SUBLANE, LANE), 0)

    def conv(src_ref, tail_ref, w_ref, sl):
        raw = src_ref[0, :, sl]
        w = w_ref[:, sl]
        t8 = tail_ref[:, sl]
        y = raw * w[CONV_W - 1:CONV_W, :]
        for s in range(1, CONV_W):
            rolled = pltpu.roll(raw, s, axis=0)
            top = jnp.where(row8 < s, pltpu.roll(t8, s, axis=0), rolled[0:SUBLANE])
            sh = top if C == SUBLANE else jnp.concatenate([top, rolled[SUBLANE:]], axis=0)
            y = y + sh * w[CONV_W - 1 - s:CONV_W - s, :]
        tail_ref[:, sl] = raw[C - SUBLANE:C]
        return y * _sigmoid(y)

    def l2n(x):
        return x * lax.rsqrt(jnp.sum(x * x, axis=-1, keepdims=True) + EPS)

    q = [l2n(conv(q_ref, tailq, cwq_ref, sl)) * (GDN_DK ** -0.5) for sl in sls]
    k = [l2n(conv(k_ref, tailk, cwk_ref, sl)) for sl in sls]
    v = [conv(v_ref, tailv, cwv_ref, sl) for sl in sls]

    lane = lax.broadcasted_iota(jnp.int32, (C, LANE), 1)
    rowc = lax.broadcasted_iota(jnp.int32, (C, 1), 0)
    valid = (c * C + rowc) < t_total
    ab = ab_ref[0]
    beta, g = [], []
    for h in hidx:
        a = jnp.sum(jnp.where(lane == h, ab, 0.0), axis=-1, keepdims=True)
        b = jnp.sum(jnp.where(lane == h + GDN_HEADS, ab, 0.0), axis=-1, keepdims=True)
        sp_in = a + dtb_ref[h]
        softplus = jnp.maximum(sp_in, 0.0) + jnp.log(1.0 + jnp.exp(-jnp.abs(sp_in)))
        beta.append(jnp.where(valid, _sigmoid(b), 0.0))
        g.append(jnp.where(valid, -jnp.exp(jnp.full((C, 1), alog_ref[h], F32)) * softplus, 0.0))

    ri = lax.broadcasted_iota(jnp.int32, (C, C), 0)
    ci = lax.broadcasted_iota(jnp.int32, (C, C), 1)
    causal = ri >= ci
    strict = ri > ci
    tril = causal.astype(F32)
    e0 = (lane == 0).astype(F32)
    eye = (ri == ci).astype(F32)
    gc_b = [_dot_sel(tril, jnp.broadcast_to(gh, (C, LANE))) for gh in g]
    gc_row = [_dot_sel(e0, gch, (((1,), (1,)), ((), ()))) for gch in gc_b]
    decay = [jnp.where(causal, jnp.exp(jnp.where(causal, gch[:, :C] - grh, 0.0)), 0.0)
             for gch, grh in zip(gc_b, gc_row)]
    kb = [kh * bh for kh, bh in zip(k, beta)]
    xpow = [-jnp.where(strict, _dot3_nt(kbh, kh) * dh, 0.0) for kbh, kh, dh in zip(kb, k, decay)]
    tinv = [eye + xh for xh in xpow]
    for _ in range(int(math.log2(C)) - 1):
        xpow = [_dot3(xh, xh) for xh in xpow]
        tinv = [th + _dot3(th, xh) for th, xh in zip(tinv, xpow)]
    eg = [jnp.exp(gch) for gch in gc_b]
    value = [_dot3(th, vh * bh) for th, vh, bh in zip(tinv, v, beta)]
    kcd = [_dot3(th, kbh * egh) for th, kbh, egh in zip(tinv, kb, eg)]
    attn = [jnp.where(causal, _dot3_nt(qh, kh) * dh, 0.0) for qh, kh, dh in zip(q, k, decay)]
    glast = [gch[C - 1:C, :] for gch in gc_b]
    kdec = [kh * jnp.exp(glh - gch) for kh, glh, gch in zip(k, glast, gc_b)]

    s = [s_scr[hh] for hh in heads]
    v_new = [vh - _dot3(kh, sh) for vh, kh, sh in zip(value, kcd, s)]
    o = [_dot3(qh * egh, sh) + _dot3(ah, vnh) for qh, egh, sh, ah, vnh in zip(q, eg, s, attn, v_new)]
    for hh in heads:
        s_scr[hh] = s[hh] * jnp.exp(glast[hh]) + _dot3_tn(kdec[hh], v_new[hh])

    nw = nw_ref[...]
    for hh, sl in zip(heads, sls):
        z = z_ref[0, :, sl]
        on = o[hh] * lax.rsqrt(jnp.mean(o[hh] * o[hh], axis=-1, keepdims=True) + EPS) * nw
        o_ref[0, :, sl] = on * (z * _sigmoid(z))

    @pl.when(c == pl.num_programs(2) - 1)
    def _():
        sout_ref[0] = s_scr[...]


GDN_HEADS_PER_STEP = 8


def _gdn(h_main3, ab3, conv_w, tail8, s0, a_log, dt_bias, gdn_norm, chunk, t_total):
    bn, tp, _ = h_main3.shape
    nc = tp // chunk
    hps = GDN_HEADS_PER_STEP
    w = hps * LANE
    hq, hk, hv, hz = COL_Q // w, COL_K // w, COL_V // w, COL_Z // w
    cq, ck, cv = 0, GDN_QK // w, 2 * GDN_QK // w
    col = lambda off: pl.BlockSpec((1, chunk, w), lambda b, h, c: (b, c, off + h))
    cw = lambda off: pl.BlockSpec((CONV_W, w), lambda b, h, c: (0, off + h))
    tl = lambda off: pl.BlockSpec((1, SUBLANE, w), lambda b, h, c: (b, 0, off + h))
    smem = pl.BlockSpec(memory_space=pltpu.SMEM)
    kern = functools.partial(_gdn_kernel, chunk=chunk, t_total=t_total, hps=hps)
    return pl.pallas_call(
        kern,
        grid=(bn, GDN_HEADS // hps, nc),
        in_specs=[smem, smem, col(hq), col(hk), col(hv), col(hz),
                  pl.BlockSpec((1, chunk, LANE), lambda b, h, c: (b, c, 0)),
                  cw(cq), cw(ck), cw(cv), tl(cq), tl(ck), tl(cv),
                  pl.BlockSpec((1, hps, GDN_DK, GDN_DV), lambda b, h, c: (b, h, 0, 0)),
                  pl.BlockSpec((1, GDN_DV), lambda b, h, c: (0, 0))],
        out_specs=[pl.BlockSpec((1, chunk, w), lambda b, h, c: (b, c, h)),
                   pl.BlockSpec((1, hps, GDN_DK, GDN_DV), lambda b, h, c: (b, h, 0, 0))],
        out_shape=[jax.ShapeDtypeStruct((bn, tp, GDN_VW), F32),
                   jax.ShapeDtypeStruct((bn, GDN_HEADS, GDN_DK, GDN_DV), F32)],
        scratch_shapes=[pltpu.VMEM((hps, GDN_DK, GDN_DV), F32)] + [pltpu.VMEM((SUBLANE, w), F32)] * 3,
        compiler_params=_cparams("parallel", "parallel", "arbitrary"),
        name="gdn",
    )(a_log, dt_bias, h_main3, h_main3, h_main3, h_main3, ab3, conv_w, conv_w, conv_w,
      tail8, tail8, tail8, s0, gdn_norm.reshape(1, GDN_DV))


SWA_WIN_BLK = 128


def _swa_kernel(q_ref, kp_ref, kc_ref, vp_ref, vc_ref, o_ref, lse_ref, *, nsub):
    qi = pl.program_id(2)
    n = SWA_WIN_BLK
    ri = lax.broadcasted_iota(jnp.int32, (n, n), 0)
    ci = lax.broadcasted_iota(jnp.int32, (n, n), 1)
    mask_cur = ci <= ri
    scale = SWA_DH ** -0.5
    for sb in range(nsub):
        rows = slice(sb * n, (sb + 1) * n)
        prows = slice((sb - 1) * n, sb * n)
        mask_prev = jnp.logical_and(ci >= ri, qi > 0) if sb == 0 else ci >= ri
        for hh in range(SWA_HEADS):
            sl = slice(hh * SWA_DH, (hh + 1) * SWA_DH)
            kp = kp_ref[0, 0, :, sl] if sb == 0 else kc_ref[0, 0, prows, sl]
            vp = vp_ref[0, 0, :, sl] if sb == 0 else vc_ref[0, 0, prows, sl]
            q = q_ref[0, 0, rows, sl].astype(BF16)
            sp = jnp.where(mask_prev, _dot_nt(q, kp.astype(BF16)) * scale, NEG_BIG)
            sc = jnp.where(mask_cur, _dot_nt(q, kc_ref[0, 0, rows, sl].astype(BF16)) * scale, NEG_BIG)
            m = jnp.maximum(jnp.max(sp, axis=-1, keepdims=True), jnp.max(sc, axis=-1, keepdims=True))
            pp = jnp.exp(sp - m)
            pc = jnp.exp(sc - m)
            ssum = jnp.sum(pp, axis=-1, keepdims=True) + jnp.sum(pc, axis=-1, keepdims=True)
            acc = (jnp.dot(pp.astype(BF16), vp.astype(BF16), preferred_element_type=F32)
                   + jnp.dot(pc.astype(BF16), vc_ref[0, 0, rows, sl].astype(BF16), preferred_element_type=F32))
            o_ref[0, 0, rows, sl] = acc / ssum
            lse_ref[0, 0, rows, sl] = jnp.broadcast_to(m + jnp.log(ssum), (n, SWA_DH))


def _swa_prompt_group(q, k, v, gi, nsub):
    bn, dil, tl, _ = q.shape
    qblk = nsub * SWA_WIN_BLK
    cur = lambda b, r, i: (b, r, i, 0)
    prev = lambda b, r, i: (b, r, jnp.maximum(i * nsub - 1, 0), 0)
    blk = (1, 1, qblk, SWA_W)
    pblk = (1, 1, SWA_WIN_BLK, SWA_W)
    return pl.pallas_call(
        functools.partial(_swa_kernel, nsub=nsub),
        grid=(bn, dil, tl // qblk),
        in_specs=[pl.BlockSpec(blk, cur), pl.BlockSpec(pblk, prev), pl.BlockSpec(blk, cur),
                  pl.BlockSpec(pblk, prev), pl.BlockSpec(blk, cur)],
        out_specs=[pl.BlockSpec(blk, cur)] * 2,
        out_shape=[jax.ShapeDtypeStruct(q.shape, F32)] * 2,
        compiler_params=_cparams("parallel", "parallel", "arbitrary"),
        name=f"swa_prompt_g{gi}",
    )(q, k, k, v, v)


def _swa_sample_kernel(q_ref, kn_ref, vn_ref, c0_ref, c1_ref, c2_ref, *out_refs):
    scale = SWA_DH ** -0.5
    for gi, c_ref in enumerate((c0_ref, c1_ref, c2_ref)):
        o_ref, lse_ref = out_refs[2 * gi], out_refs[2 * gi + 1]
        q = q_ref[0, gi]
        kn = kn_ref[0, gi]
        vn = vn_ref[0, gi]
        kc = c_ref[0, :, 0, 0]
        vc = c_ref[0, :, 0, 1]
        s = jnp.sum(kc * q[None], axis=-1, keepdims=True) * scale
        sn = jnp.sum(kn * q, axis=-1, keepdims=True) * scale
        m = jnp.maximum(jnp.max(s, axis=0), sn)
        p = jnp.exp(s - m[None])
        pn = jnp.exp(sn - m)
        den = jnp.sum(p, axis=0) + pn
        o_ref[0] = (jnp.sum(p * vc, axis=0) + pn * vn) / den
        lse_ref[0] = jnp.broadcast_to(m + jnp.log(den), (SWA_HEADS, SWA_DH))


def _swa_sample(q_r, k_r, v_new, caches):
    bn = q_r.shape[0]
    nkeys = SWA_GROUPS[0][0] // SWA_GROUPS[0][1]
    views = []
    for (win, dil), cache in zip(SWA_GROUPS, caches):
        assert cache.shape[1] == win and win // dil == nkeys
        views.append(cache.reshape(bn, nkeys, dil, 2, SWA_HEADS, SWA_DH))
    heads = lambda a: a.reshape(bn, N_SWA, SWA_HEADS, SWA_DH)
    row = pl.BlockSpec((1, N_SWA, SWA_HEADS, SWA_DH), lambda b: (b, 0, 0, 0))
    cspec = pl.BlockSpec((1, nkeys, 1, 2, SWA_HEADS, SWA_DH), lambda b: (b, 0, 0, 0, 0, 0))
    ospec = pl.BlockSpec((1, SWA_HEADS, SWA_DH), lambda b: (b, 0, 0))
    outs = pl.pallas_call(
        _swa_sample_kernel,
        grid=(bn,),
        in_specs=[row, row, row, cspec, cspec, cspec],
        out_specs=[ospec] * (2 * N_SWA),
        out_shape=[jax.ShapeDtypeStruct((bn, SWA_HEADS, SWA_DH), F32)] * (2 * N_SWA),
        compiler_params=_cparams("parallel"),
        name="swa_sample",
    )(heads(q_r), heads(k_r), heads(v_new), *views)
    return [(outs[2 * g].reshape(bn, SWA_W), outs[2 * g + 1].reshape(bn, SWA_W)) for g in range(N_SWA)]


def _mix_kernel(oa_ref, o0_ref, l0_ref, o1_ref, l1_ref, o2_ref, l2_ref, ga_ref, gb_ref, x_ref,
                wa_ref, wb_ref, wo_ref, nw_ref, x1_ref, xn_ref, slab, *, dils, tm):
    def token_major(ref, dil):
        if dil == 1:
            return ref[0, 0]
        cols = []
        for hh in range(SWA_HEADS):
            for r in range(dil):
                slab[hh, pl.ds(r, tm // dil, stride=dil), :] = ref[0, r, :, hh * SWA_DH:(hh + 1) * SWA_DH]
            cols.append(slab[hh])
        return jnp.concatenate(cols, axis=1)

    os = [token_major(r, d) for r, d in zip((o0_ref, o1_ref, o2_ref), dils)]
    ls = [token_major(r, d) for r, d in zip((l0_ref, l1_ref, l2_ref), dils)]
    lmax = jnp.maximum(jnp.maximum(ls[0], ls[1]), ls[2])
    es = [jnp.exp(l - lmax) for l in ls]
    ob = (es[0] * os[0] + es[1] * os[1] + es[2] * os[2]) / (es[0] + es[1] + es[2])
    pa = jnp.dot(oa_ref[0].astype(BF16), wa_ref[...], preferred_element_type=F32)
    pb = jnp.dot(ob.astype(BF16), wb_ref[...], preferred_element_type=F32)
    merged = _sigmoid(ga_ref[0]) * pa + _sigmoid(gb_ref[0]) * pb
    x1 = x_ref[0] + jnp.dot(merged.astype(BF16), wo_ref[...], preferred_element_type=F32)
    x1_ref[0] = x1
    xn_ref[0] = x1 * lax.rsqrt(jnp.mean(x1 * x1, axis=-1, keepdims=True) + EPS) * nw_ref[...]


def _mix(o_a3, parts, dils, h3, x3, wa, wb, wo, norm_ffn, tm):
    bn, t, _ = x3.shape
    rowblk = lambda w, cb=0: pl.BlockSpec((1, tm, w), lambda b, i: (b, i, cb))
    full = lambda a: pl.BlockSpec(a.shape, lambda b, i: (0, 0))
    nw = norm_ffn.reshape(1, D_MODEL)
    part_specs, flat = [], []
    for (o, lse), dil in zip(parts, dils):
        spec = pl.BlockSpec((1, dil, tm // dil, SWA_W), lambda b, i: (b, 0, i, 0))
        part_specs += [spec, spec]
        flat += [o, lse]
    return pl.pallas_call(
        functools.partial(_mix_kernel, dils=dils, tm=tm),
        grid=(bn, t // tm),
        in_specs=[rowblk(GDN_VW)] + part_specs
                 + [rowblk(D_MODEL, COL_GA // D_MODEL), rowblk(D_MODEL, COL_GB // D_MODEL),
                    rowblk(D_MODEL), full(wa), full(wb), full(wo), full(nw)],
        out_specs=[rowblk(D_MODEL), rowblk(D_MODEL)],
        out_shape=[jax.ShapeDtypeStruct((bn, t, D_MODEL), F32)] * 2,
        scratch_shapes=[pltpu.VMEM((SWA_HEADS, tm, SWA_DH), F32)],
        compiler_params=_cparams("parallel", "parallel"),
        name="mix_out",
    )(o_a3, *flat, h3, h3, x3, wa, wb, wo, nw)


def _topk_rows(vals, k, payload=None):
    n, t = vals.shape
    idx = lax.broadcasted_iota(jnp.int32, (n, t), 0)
    out_v, out_i = [], []
    for _ in range(k):
        m = jnp.max(vals, axis=0, keepdims=True)
        am = jnp.min(jnp.where(vals == m, idx, n), axis=0, keepdims=True)
        hit = idx == am
        out_v.append(m)
        out_i.append(am if payload is None else jnp.sum(jnp.where(hit, payload, 0), axis=0, keepdims=True))
        vals = jnp.where(hit, -jnp.inf, vals)
    return jnp.concatenate(out_v, axis=0), jnp.concatenate(out_i, axis=0)


def _route_kernel(q_ref, keys_ref, eid_ref, gate_ref):
    half = PEER_DKEY // 2
    eids, gates = [], []
    for h in range(PEER_HEADS):
        tops = []
        for p in range(2):
            qs = q_ref[:, (2 * h + p) * half:(2 * h + p + 1) * half].astype(BF16)
            st = _dot_nt(keys_ref[h, p].astype(BF16), qs)
            tops.append(_topk_rows(st, PEER_TOPK))
        (v1, i1), (v2, i2) = tops
        sub = SUBLANE
        assert PEER_TOPK == 2 * sub
        rows = [(slice(0, 1), slice(0, PEER_TOPK))]
        rows += [(slice(i, i + 1), slice(0, sub)) for i in range(1, sub)]
        rows += [(slice(sub, PEER_TOPK), slice(0, 1))]
        cand = jnp.concatenate([v1[a] + v2[b] for a, b in rows], axis=0)
        ecand = jnp.concatenate([i1[a] * PEER_NKEYS + i2[b] for a, b in rows], axis=0)
        sc, e = _topk_rows(cand, PEER_TOPK, payload=ecand)
        ex = jnp.exp(sc - sc[0:1])
        gates.append(ex / jnp.sum(ex, axis=0, keepdims=True))
        eids.append(e)
    eid_ref[...] = jnp.concatenate(eids, axis=0).T
    gate_ref[...] = jnp.concatenate(gates, axis=0)


def _route(qp, sub_keys, tt):
    m = qp.shape[0]
    return pl.pallas_call(
        _route_kernel,
        grid=(m // tt,),
        in_specs=[pl.BlockSpec((tt, PEER_HEADS * PEER_DKEY), lambda i: (i, 0)),
                  pl.BlockSpec(sub_keys.shape, lambda i: (0, 0, 0, 0))],
        out_specs=[pl.BlockSpec((tt, PEER_SEL), lambda i: (i, 0)), pl.BlockSpec((PEER_SEL, tt), lambda i: (0, i))],
        out_shape=[jax.ShapeDtypeStruct((m, PEER_SEL), jnp.int32), jax.ShapeDtypeStruct((PEER_SEL, m), F32)],
        compiler_params=_cparams("parallel"),
        name="peer_route",
    )(qp, sub_keys)


PEER_NBUF = 8


PEER_CHUNKS = D_MODEL // LANE


def _pack_kernel(d_ref, u_ref, o_ref):
    hi = lax.bitcast_convert_type(d_ref[...].astype(BF16).astype(F32), jnp.uint32)
    lo = lax.bitcast_convert_type(u_ref[...].astype(BF16).astype(F32), jnp.uint32) >> 16
    word = hi | lo
    for c in range(PEER_CHUNKS):
        o_ref[:, c, :] = word[:, c * LANE:(c + 1) * LANE]


def _pack_peer_table(expert_down, expert_up, tr=256):
    e, d = expert_down.shape
    return pl.pallas_call(
        _pack_kernel,
        grid=(e // tr,),
        in_specs=[pl.BlockSpec((tr, d), lambda i: (i, 0))] * 2,
        out_specs=pl.BlockSpec((tr, PEER_CHUNKS, LANE), lambda i: (i, 0, 0)),
        out_shape=jax.ShapeDtypeStruct((e, PEER_CHUNKS, LANE), jnp.uint32),
        compiler_params=_cparams("parallel"),
        name="peer_pack",
    )(expert_down, expert_up)


def _peer_kernel(eid_ref, gate_ref, xn_ref, x1_ref, nw_ref, tbl_hbm, y_ref, *scratch, n_tok):
    nbuf = PEER_NBUF
    tbufs, (sem, xrow, yrow) = scratch[:nbuf], scratch[nbuf:]

    def issue(t, slot, rows):
        for j in rows:
            pltpu.make_async_copy(tbl_hbm.at[eid_ref[t, j]], tbufs[slot].at[:, j, :], sem.at[slot]).start(priority=j % 2)

    def wait(slot):
        pltpu.make_async_copy(tbufs[slot], tbufs[slot], sem.at[slot]).wait()

    lane = lax.broadcasted_iota(jnp.int32, (PEER_SEL, gate_ref.shape[1]), 1)
    hi_mask = jnp.uint32(0xFFFF0000)
    per_chunk = 3
    mid_rows = range(2 * PEER_CHUNKS * per_chunk, PEER_SEL)

    def compute(t, slot, t_ahead):
        def issue_rows(rows):
            if t_ahead is not None:
                issue(t_ahead, (slot - 1) % nbuf, rows)

        xrow[...] = xn_ref[pl.ds(t, 1), :]
        acc = jnp.zeros((PEER_SEL, LANE), F32)
        for c in range(PEER_CHUNKS):
            down = lax.bitcast_convert_type(tbufs[slot][c] & hi_mask, F32)
            acc = acc + down * xrow[:, c * LANE:(c + 1) * LANE]
            issue_rows(range(c * per_chunk, (c + 1) * per_chunk))
        act = jnp.sum(acc, axis=-1, keepdims=True)
        gate = jnp.sum(jnp.where(lane == t, gate_ref[...], 0.0), axis=-1, keepdims=True)
        issue_rows(mid_rows)
        gelu = 0.5 * act * (1.0 + lax.erf(act * (2.0 ** -0.5)))
        w = gate * gelu
        for c in range(PEER_CHUNKS):
            up = lax.bitcast_convert_type(tbufs[slot][c] << 16, F32)
            yrow[:, c * LANE:(c + 1) * LANE] = jnp.sum(up * w, axis=0, keepdims=True)
            issue_rows(range((PEER_CHUNKS + c) * per_chunk, (PEER_CHUNKS + c + 1) * per_chunk))
        y_ref[pl.ds(t, 1), :] = x1_ref[pl.ds(t, 1), :] + yrow[...]

    assert n_tok % nbuf == 0
    for k in range(nbuf - 1):
        issue(k, k, range(PEER_SEL))

    def group(p, carry):
        for k in range(nbuf):
            t = nbuf * p + k
            wait(k)
            compute(t, k, t + nbuf - 1)
        return carry

    lax.fori_loop(0, n_tok // nbuf - 1, group, 0)
    for k in range(nbuf):
        t = n_tok - nbuf + k
        wait(k)
        compute(t, k, t + nbuf - 1 if k == 0 else None)

    x2 = y_ref[...]
    y_ref[...] = x2 * lax.rsqrt(jnp.mean(x2 * x2, axis=-1, keepdims=True) + EPS) * nw_ref[...]


def _peer(eid, gates_t, xn2, x1, norm_final, table, tt, n_tok):
    m = xn2.shape[0]
    kern = functools.partial(_peer_kernel, n_tok=n_tok)
    return pl.pallas_call(
        kern,
        grid=(m // n_tok,),
        in_specs=[pl.BlockSpec((tt, PEER_SEL), lambda i: (i, 0), memory_space=pltpu.SMEM),
                  pl.BlockSpec((PEER_SEL, tt), lambda i: (0, i)),
                  pl.BlockSpec((n_tok, D_MODEL), lambda i: (i, 0)),
                  pl.BlockSpec((n_tok, D_MODEL), lambda i: (i, 0)),
                  pl.BlockSpec((1, D_MODEL), lambda i: (0, 0)),
                  pl.BlockSpec(memory_space=pl.ANY)],
        out_specs=pl.BlockSpec((n_tok, D_MODEL), lambda i: (i, 0)),
        out_shape=jax.ShapeDtypeStruct((m, D_MODEL), F32),
        scratch_shapes=[pltpu.VMEM((PEER_CHUNKS, PEER_SEL, LANE), jnp.uint32)] * PEER_NBUF
                       + [pltpu.SemaphoreType.DMA((PEER_NBUF,)),
                          pltpu.VMEM((1, D_MODEL), F32), pltpu.VMEM((1, D_MODEL), F32)],
        compiler_params=_cparams("arbitrary"),
        name="peer_experts",
    )(eid, gates_t, xn2, x1, norm_final.reshape(1, D_MODEL), table)


def _pick(m, candidates):
    for c in candidates:
        if m % c == 0:
            return c
    return m


def _layer(x, pos, tail8, s0, caches, w, chunk):
    bn, t, _ = x.shape
    m = bn * t
    x2d = x.reshape(m, D_MODEL)
    tm_big = _pick(m, (1024, 512, 256, 128, 64, 32))

    xn = _rmsnorm(x2d, w["norm_mix"], _pick(m, (512, 256, 128, 64, 32)), BF16)
    h_main = _matmul(xn, w["w_main"], tm_big, 640, "proj_in")
    h_ab = _matmul(xn, w["w_ab"], tm_big, LANE, "proj_ab")

    tp = -(-t // chunk) * chunk
    h3 = h_main.reshape(bn, t, N_MAIN)
    ab3 = h_ab.reshape(bn, t, LANE)
    h3p = h3 if tp == t else jnp.pad(h3, ((0, 0), (0, tp - t), (0, 0)))
    ab3p = ab3 if tp == t else jnp.pad(ab3, ((0, 0), (0, tp - t), (0, 0)))
    o_a, s_new = _gdn(h3p, ab3p, w["conv_w"], tail8, s0, w["a_log"], w["dt_bias"], w["gdn_norm"], chunk, t)
    o_a = o_a[:, :t]

    cosf, sinf = _rope_tables(pos)
    kv_new = []
    if caches is None:
        qkv = _rope_split(h3, cosf, sinf, 256)
        parts = [_swa_prompt_group(q, k, v, gi, 2 if q.shape[2] % (2 * SWA_WIN_BLK) == 0 else 1)
                 for gi, (q, k, v) in enumerate(qkv)]
        dils = tuple(d for _, d in SWA_GROUPS)
        for (win, dil), (_, k, v) in zip(SWA_GROUPS, qkv):
            keep = min(win, t)
            tok = lambda a: (a[:, :, (t - keep) // dil:].transpose(0, 2, 1, 3)
                             .reshape(bn, keep, SWA_HEADS, SWA_DH))
            kv_new.append(jnp.stack([tok(k), tok(v)], axis=2))
        mix_shape = (bn, t)
    else:
        cosf, sinf = jnp.tile(cosf, (bn, 1)), jnp.tile(sinf, (bn, 1))
        q_r, k_r = _rope(h_main, cosf, sinf, _pick(m, (512, 256, 128, 64, 32)))
        v_new = h_main[:, COL_SV:COL_SV + SWA_QKV]
        parts = [(o.reshape(1, 1, m, SWA_W), lse.reshape(1, 1, m, SWA_W))
                 for o, lse in _swa_sample(q_r, k_r, v_new, caches)]
        dils = (1,) * N_SWA
        for gi in range(N_SWA):
            kk = k_r[:, gi * SWA_W:(gi + 1) * SWA_W].reshape(bn, t, SWA_HEADS, SWA_DH)
            vv = v_new[:, gi * SWA_W:(gi + 1) * SWA_W].reshape(bn, t, SWA_HEADS, SWA_DH)
            kv_new.append(jnp.stack([kk, vv], axis=2))
        mix_shape = (1, m)

    mb, mt = mix_shape
    x1, xn2 = _mix(o_a.reshape(mb, mt, GDN_VW), parts, dils, h_main.reshape(mb, mt, N_MAIN),
                   x.reshape(mb, mt, D_MODEL), w["w_br_a"], w["w_br_b"], w["w_out"], w["norm_ffn"],
                   _pick(mt, (256, 128, 64, 32)))
    x1, xn2 = x1.reshape(m, D_MODEL), xn2.reshape(m, D_MODEL)

    mp = -(-m // LANE) * LANE
    xq = xn2 if mp == m else jnp.pad(xn2, ((0, mp - m), (0, 0)))
    qp = _matmul(xq, w["w_query"], _pick(mp, (1024, 512, 256, 128)), 512, "peer_query")
    eid, gates_t = _route(qp, w["sub_keys"], LANE)
    n_tok = min(m, LANE)
    y = _peer(eid, gates_t, xn2, x1, w["norm_final"], w["peer_table"], LANE, n_tok)
    return y.reshape(bn, t, D_MODEL), s_new, h3, kv_new


def kernel(x_prompt, x_sample, state_gdn, state_conv, cache_kv_w128, cache_kv_w512, cache_kv_w2048,
           norm_mix, w_in, conv_w, a_log, dt_bias, gdn_norm, w_br_a, w_br_b, w_out, norm_ffn,
           w_query, sub_keys, expert_down, expert_up, norm_final):
    w_main = jnp.concatenate([w_in[:, IN_GA:], w_in[:, :IN_AB], w_in[:, IN_SQ:IN_GA]], axis=1).astype(BF16)
    w_ab = jnp.pad(w_in[:, IN_AB:IN_SQ], ((0, 0), (0, LANE - 2 * GDN_HEADS))).astype(BF16)
    w = dict(norm_mix=norm_mix, w_main=w_main, w_ab=w_ab, conv_w=conv_w, a_log=a_log, dt_bias=dt_bias,
             gdn_norm=gdn_norm, w_br_a=w_br_a.astype(BF16), w_br_b=w_br_b.astype(BF16), w_out=w_out.astype(BF16),
             norm_ffn=norm_ffn, w_query=w_query.astype(BF16), sub_keys=sub_keys,
             peer_table=_pack_peer_table(expert_down, expert_up), norm_final=norm_final)

    bp, tlen = x_prompt.shape[:2]
    bs, ts = x_sample.shape[:2]
    caches = (cache_kv_w128, cache_kv_w512, cache_kv_w2048)

    tail8 = jnp.concatenate([jnp.zeros((bs, SUBLANE - (CONV_W - 1), GDN_CONV_CH), F32), state_conv], axis=1)
    y_s, gdn_s, h3_s, kv_new = _layer(
        x_sample, PAST_LEN + jnp.arange(ts, dtype=jnp.int32), tail8, state_gdn, caches, w, SUBLANE)
    conv_s = jnp.concatenate([state_conv, h3_s[:, :, COL_Q:COL_Q + GDN_CONV_CH]], axis=1)[:, ts:]
    kv_s = [jnp.concatenate([cache, new.astype(cache.dtype)], axis=1)[:, ts:] for cache, new in zip(caches, kv_new)]

    y_p, gdn_p, h3_p, kv_p = _layer(
        x_prompt, jnp.arange(tlen, dtype=jnp.int32),
        jnp.zeros((bp, SUBLANE, GDN_CONV_CH), F32), jnp.zeros((bp, GDN_HEADS, GDN_DK, GDN_DV), F32),
        None, w, CHUNK)
    conv_p = h3_p[:, tlen - (CONV_W - 1):, COL_Q:COL_Q + GDN_CONV_CH]

    return (y_p, y_s, gdn_p.astype(x_prompt.dtype), conv_p, kv_p[0], kv_p[1], kv_p[2],
            gdn_s.astype(state_gdn.dtype), conv_s, kv_s[0], kv_s[1], kv_s[2])
```

```python
import functools
import math

import jax
import jax.numpy as jnp
import numpy as np
from jax import lax
from jax.experimental import pallas as pl
from jax.experimental.pallas import tpu as pltpu

F32 = jnp.float32
BF16 = jnp.bfloat16

LANE = 128
SUBLANE = 8
VMEM_LIMIT = 56 * 1024 * 1024

D_MODEL = 2048
PAST_LEN = 16384
EPS = 1e-6
GDN_HEADS = 8
GDN_DK = 128
GDN_DV = 128
GDN_QK = GDN_HEADS * GDN_DK
GDN_VW = GDN_HEADS * GDN_DV
GDN_CONV_CH = 2 * GDN_QK + GDN_VW
CONV_W = 4
CHUNK = 64
SWA_GROUPS = ((128, 1), (512, 4), (2048, 16))
N_SWA = 3
SWA_HEADS = 4
SWA_DH = 128
SWA_W = SWA_HEADS * SWA_DH
SWA_QKV = N_SWA * SWA_W
ROT_DIM = SWA_DH // 4
ROPE_THETA = 500000.0
PEER_HEADS = 8
PEER_NKEYS = 128
PEER_DKEY = 256
PEER_TOPK = 16
PEER_SEL = PEER_HEADS * PEER_TOPK

COL_GA = 0
COL_GB = D_MODEL
COL_Q = 2 * D_MODEL
COL_K = COL_Q + GDN_QK
COL_V = COL_K + GDN_QK
COL_Z = COL_V + GDN_VW
COL_SQ = COL_Z + GDN_VW
COL_SK = COL_SQ + SWA_QKV
COL_SV = COL_SK + SWA_QKV
N_MAIN = COL_SV + SWA_QKV
IN_AB = 2 * GDN_QK + 2 * GDN_VW
IN_SQ = IN_AB + 2 * GDN_HEADS
IN_GA = IN_SQ + 3 * SWA_QKV
NEG_BIG = -1e30


def _cparams(*sem):
    return pltpu.CompilerParams(dimension_semantics=sem, vmem_limit_bytes=VMEM_LIMIT)


def _sigmoid(x):
    return 1.0 / (1.0 + jnp.exp(-x))


def _dot_sel(sel, b, dims=(((1,), (0,)), ((), ()))):
    b1 = b.astype(BF16)
    r1 = b - b1.astype(F32)
    b2 = r1.astype(BF16)
    b3 = (r1 - b2.astype(F32)).astype(BF16)
    s = sel.astype(BF16)
    dot = lambda y: lax.dot_general(s, y, dims, preferred_element_type=F32)
    return dot(b1) + (dot(b2) + dot(b3))


def _split_bf16(a):
    hi = a.astype(BF16)
    return hi, (a - hi.astype(F32)).astype(BF16)


def _dot3(a, b, dims=(((1,), (0,)), ((), ()))):
    a_hi, a_lo = _split_bf16(a)
    b_hi, b_lo = _split_bf16(b)
    dot = lambda x, y: lax.dot_general(x, y, dims, preferred_element_type=F32)
    return dot(a_hi, b_hi) + (dot(a_hi, b_lo) + dot(a_lo, b_hi))


def _dot3_nt(a, b):
    return _dot3(a, b, (((1,), (1,)), ((), ())))


def _dot3_tn(a, b):
    return _dot3(a, b, (((0,), (0,)), ((), ())))


def _dot_nt(a, b):
    return lax.dot_general(a, b, (((1,), (1,)), ((), ())), preferred_element_type=F32)


def _rmsnorm_kernel(x_ref, w_ref, o_ref):
    x = x_ref[...]
    y = x * lax.rsqrt(jnp.mean(x * x, axis=-1, keepdims=True) + EPS) * w_ref[...]
    o_ref[...] = y.astype(o_ref.dtype)


def _rmsnorm(x, w, tm, out_dtype):
    m, d = x.shape
    return pl.pallas_call(
        _rmsnorm_kernel,
        grid=(m // tm,),
        in_specs=[pl.BlockSpec((tm, d), lambda i: (i, 0)), pl.BlockSpec((1, d), lambda i: (0, 0))],
        out_specs=pl.BlockSpec((tm, d), lambda i: (i, 0)),
        out_shape=jax.ShapeDtypeStruct((m, d), out_dtype),
        compiler_params=_cparams("parallel"),
        name="rmsnorm",
    )(x, w.reshape(1, d))


def _mm_kernel(x_ref, w_ref, o_ref):
    o_ref[...] = jnp.dot(x_ref[...].astype(BF16), w_ref[...], preferred_element_type=F32)


def _matmul(x, w, tm, tn, name):
    m, k = x.shape
    n = w.shape[1]
    return pl.pallas_call(
        _mm_kernel,
        grid=(n // tn, m // tm),
        in_specs=[pl.BlockSpec((tm, k), lambda j, i: (i, 0)), pl.BlockSpec((k, tn), lambda j, i: (0, j))],
        out_specs=pl.BlockSpec((tm, tn), lambda j, i: (i, j)),
        out_shape=jax.ShapeDtypeStruct((m, n), F32),
        compiler_params=_cparams("parallel", "parallel"),
        name=name,
    )(x, w)


def _rope_kernel(q_ref, k_ref, cos_ref, sin_ref, qo_ref, ko_ref):
    cosf = cos_ref[...]
    sinf = sin_ref[...]
    lane = lax.broadcasted_iota(jnp.int32, cosf.shape, 1)
    first = lane < ROT_DIM // 2
    for src, dst in ((q_ref, qo_ref), (k_ref, ko_ref)):
        for hh in range(SWA_HEADS):
            sl = slice(hh * SWA_DH, (hh + 1) * SWA_DH)
            x = src[:, sl]
            partner = jnp.where(first, pltpu.roll(x, SWA_DH - ROT_DIM // 2, axis=1), pltpu.roll(x, ROT_DIM // 2, axis=1))
            dst[:, sl] = x * cosf + partner * sinf


def _rope(h_main, cosf, sinf, tm):
    m = h_main.shape[0]
    nt = cosf.shape[0] // tm
    qb, kb = COL_SQ // SWA_W, COL_SK // SWA_W
    return pl.pallas_call(
        _rope_kernel,
        grid=(m // tm, N_SWA),
        in_specs=[
            pl.BlockSpec((tm, SWA_W), lambda i, g: (i, qb + g)),
            pl.BlockSpec((tm, SWA_W), lambda i, g: (i, kb + g)),
            pl.BlockSpec((tm, SWA_DH), lambda i, g: (i % nt, 0)),
            pl.BlockSpec((tm, SWA_DH), lambda i, g: (i % nt, 0)),
        ],
        out_specs=[pl.BlockSpec((tm, SWA_W), lambda i, g: (i, g)), pl.BlockSpec((tm, SWA_W), lambda i, g: (i, g))],
        out_shape=[jax.ShapeDtypeStruct((m, SWA_QKV), F32), jax.ShapeDtypeStruct((m, SWA_QKV), F32)],
        compiler_params=_cparams("parallel", "parallel"),
        name="rope",
    )(h_main, h_main, cosf, sinf)


def _rope_split_kernel(*refs, dils, tm):
    ins, cos_ref, sin_ref = refs[:3 * N_SWA], refs[3 * N_SWA], refs[3 * N_SWA + 1]
    outs, slab = refs[3 * N_SWA + 2:6 * N_SWA + 2], refs[6 * N_SWA + 2]
    cosf = cos_ref[...]
    sinf = sin_ref[...]
    first = lax.broadcasted_iota(jnp.int32, cosf.shape, 1) < ROT_DIM // 2
    for gi, dil in enumerate(dils):
        for kind in range(3):
            src, dst = ins[3 * gi + kind], outs[3 * gi + kind]
            for hh in range(SWA_HEADS):
                sl = slice(hh * SWA_DH, (hh + 1) * SWA_DH)
                x = src[0, :, sl]
                if kind < 2:
                    partner = jnp.where(first, pltpu.roll(x, SWA_DH - ROT_DIM // 2, axis=1),
                                        pltpu.roll(x, ROT_DIM // 2, axis=1))
                    x = x * cosf + partner * sinf
                if dil == 1:
                    dst[0, 0, :, sl] = x
                else:
                    slab[hh] = x
                    for r in range(dil):
                        dst[0, r, :, sl] = slab[hh, pl.ds(r, tm // dil, stride=dil), :]


def _rope_split(h3, cosf, sinf, tm):
    bn, t, _ = h3.shape
    dils = tuple(d for _, d in SWA_GROUPS)
    in_specs, out_specs, out_shape = [], [], []
    for gi, dil in enumerate(dils):
        for off in (COL_SQ, COL_SK, COL_SV):
            cb = off // SWA_W + gi
            in_specs.append(pl.BlockSpec((1, tm, SWA_W), lambda b, i, cb=cb: (b, i, cb)))
            out_specs.append(pl.BlockSpec((1, dil, tm // dil, SWA_W), lambda b, i: (b, 0, i, 0)))
            out_shape.append(jax.ShapeDtypeStruct((bn, dil, t // dil, SWA_W), F32))
    tbl = pl.BlockSpec((tm, SWA_DH), lambda b, i: (i, 0))
    outs = pl.pallas_call(
        functools.partial(_rope_split_kernel, dils=dils, tm=tm),
        grid=(bn, t // tm),
        in_specs=in_specs + [tbl, tbl],
        out_specs=out_specs,
        out_shape=out_shape,
        scratch_shapes=[pltpu.VMEM((SWA_HEADS, tm, SWA_DH), F32)],
        compiler_params=_cparams("parallel", "parallel"),
        name="rope_split",
    )(*([h3] * (3 * N_SWA)), cosf, sinf)
    return [tuple(outs[3 * gi:3 * gi + 3]) for gi in range(N_SWA)]


def _rope_tables(pos):
    half = ROT_DIM // 2
    inv = ROPE_THETA ** (-jnp.arange(half, dtype=F32) * 2.0 / ROT_DIM)
    ang = pos.astype(F32)[:, None] * inv[None, :]
    cos, sin = jnp.cos(ang), jnp.sin(ang)
    n = pos.shape[0]
    cosf = jnp.concatenate([cos, cos, jnp.ones((n, SWA_DH - ROT_DIM), F32)], axis=1)
    sinf = jnp.concatenate([-sin, sin, jnp.zeros((n, SWA_DH - ROT_DIM), F32)], axis=1)
    return cosf, sinf


def _gdn_kernel(alog_ref, dtb_ref, q_ref, k_ref, v_ref, z_ref, ab_ref, cwq_ref, cwk_ref, cwv_ref,
                tq_ref, tk_ref, tv_ref, s0_ref, nw_ref, o_ref, sout_ref,
                s_scr, tailq, tailk, tailv, *, chunk, t_total, hps):
    c = pl.program_id(2)

    @pl.when(c == 0)
    def _():
        s_scr[...] = s0_ref[0]
        tailq[...] = tq_ref[0]
        tailk[...] = tk_ref[0]
        tailv[...] = tv_ref[0]

    C = chunk
    heads = range(hps)
    sls = [slice(hh * LANE, (hh + 1) * LANE) for hh in heads]
    hidx = [pl.program_id(1) * hps + hh for hh in heads]
    row8 = lax.broadcasted_iota(jnp.int32, (SUBLANE, LANE), 0)

    def conv(src_ref, tail_ref, w_ref, sl):
        raw = src_ref[0, :, sl]
        w = w_ref[:, sl]
        t8 = tail_ref[:, sl]
        y = raw * w[CONV_W - 1:CONV_W, :]
        for s in range(1, CONV_W):
            rolled = pltpu.roll(raw, s, axis=0)
            top = jnp.where(row8 < s, pltpu.roll(t8, s, axis=0), rolled[0:SUBLANE])
            sh = top if C == SUBLANE else jnp.concatenate([top, rolled[SUBLANE:]], axis=0)
            y = y + sh * w[CONV_W - 1 - s:CONV_W - s, :]
        tail_ref[:, sl] = raw[C - SUBLANE:C]
        return y * _sigmoid(y)

    def l2n(x):
        return x * lax.rsqrt(jnp.sum(x * x, axis=-1, keepdims=True) + EPS)

    q = [l2n(conv(q_ref, tailq, cwq_ref, sl)) * (GDN_DK ** -0.5) for sl in sls]
    k = [l2n(conv(k_ref, tailk, cwk_ref, sl)) for sl in sls]
    v = [conv(v_ref, tailv, cwv_ref, sl) for sl in sls]

    lane = lax.broadcasted_iota(jnp.int32, (C, LANE), 1)
    rowc = lax.broadcasted_iota(jnp.int32, (C, 1), 0)
    valid = (c * C + rowc) < t_total
    ab = ab_ref[0]
    beta, g = [], []
    for h in hidx:
        a = jnp.sum(jnp.where(lane == h, ab, 0.0), axis=-1, keepdims=True)
        b = jnp.sum(jnp.where(lane == h + GDN_HEADS, ab, 0.0), axis=-1, keepdims=True)
        sp_in = a + dtb_ref[h]
        softplus = jnp.maximum(sp_in, 0.0) + jnp.log(1.0 + jnp.exp(-jnp.abs(sp_in)))
        beta.append(jnp.where(valid, _sigmoid(b), 0.0))
        g.append(jnp.where(valid, -jnp.exp(jnp.full((C, 1), alog_ref[h], F32)) * softplus, 0.0))

    ri = lax.broadcasted_iota(jnp.int32, (C, C), 0)
    ci = lax.broadcasted_iota(jnp.int32, (C, C), 1)
    causal = ri >= ci
    strict = ri > ci
    tril = causal.astype(F32)
    e0 = (lane == 0).astype(F32)
    eye = (ri == ci).astype(F32)
    gc_b = [_dot_sel(tril, jnp.broadcast_to(gh, (C, LANE))) for gh in g]
    gc_row = [_dot_sel(e0, gch, (((1,), (1,)), ((), ()))) for gch in gc_b]
    decay = [jnp.where(causal, jnp.exp(jnp.where(causal, gch[:, :C] - grh, 0.0)), 0.0)
             for gch, grh in zip(gc_b, gc_row)]
    kb = [kh * bh for kh, bh in zip(k, beta)]
    xpow = [-jnp.where(strict, _dot3_nt(kbh, kh) * dh, 0.0) for kbh, kh, dh in zip(kb, k, decay)]
    tinv = [eye + xh for xh in xpow]
    for _ in range(int(math.log2(C)) - 1):
        xpow = [_dot3(xh, xh) for xh in xpow]
        tinv = [th + _dot3(th, xh) for th, xh in zip(tinv, xpow)]
    eg = [jnp.exp(gch) for gch in gc_b]
    value = [_dot3(th, vh * bh) for th, vh, bh in zip(tinv, v, beta)]
    kcd = [_dot3(th, kbh * egh) for th, kbh, egh in zip(tinv, kb, eg)]
    attn = [jnp.where(causal, _dot_nt(qh.astype(BF16), kh.astype(BF16)) * dh, 0.0) for qh, kh, dh in zip(q, k, decay)]
    glast = [gch[C - 1:C, :] for gch in gc_b]
    kdec = [kh * jnp.exp(glh - gch) for kh, glh, gch in zip(k, glast, gc_b)]

    s = [s_scr[hh] for hh in heads]
    v_new = [vh - _dot3(kh, sh) for vh, kh, sh in zip(value, kcd, s)]
    bdot = lambda a, b: jnp.dot(a.astype(BF16), b.astype(BF16), preferred_element_type=F32)
    o = [bdot(qh * egh, sh) + bdot(ah, vnh) for qh, egh, sh, ah, vnh in zip(q, eg, s, attn, v_new)]
    for hh in heads:
        s_scr[hh] = s[hh] * jnp.exp(glast[hh]) + _dot3_tn(kdec[hh], v_new[hh])

    nw = nw_ref[...]
    for hh, sl in zip(heads, sls):
        z = z_ref[0, :, sl]
        on = o[hh] * lax.rsqrt(jnp.mean(o[hh] * o[hh], axis=-1, keepdims=True) + EPS) * nw
        o_ref[0, :, sl] = on * (z * _sigmoid(z))

    @pl.when(c == pl.num_programs(2) - 1)
    def _():
        sout_ref[0] = s_scr[...]


GDN_HEADS_PER_STEP = 8


def _gdn(h_main3, ab3, conv_w, tail8, s0, a_log, dt_bias, gdn_norm, chunk, t_total):
    bn, tp, _ = h_main3.shape
    nc = tp // chunk
    hps = GDN_HEADS_PER_STEP
    w = hps * LANE
    hq, hk, hv, hz = COL_Q // w, COL_K // w, COL_V // w, COL_Z // w
    cq, ck, cv = 0, GDN_QK // w, 2 * GDN_QK // w
    col = lambda off: pl.BlockSpec((1, chunk, w), lambda b, h, c: (b, c, off + h))
    cw = lambda off: pl.BlockSpec((CONV_W, w), lambda b, h, c: (0, off + h))
    tl = lambda off: pl.BlockSpec((1, SUBLANE, w), lambda b, h, c: (b, 0, off + h))
    smem = pl.BlockSpec(memory_space=pltpu.SMEM)
    kern = functools.partial(_gdn_kernel, chunk=chunk, t_total=t_total, hps=hps)
    return pl.pallas_call(
        kern,
        grid=(bn, GDN_HEADS // hps, nc),
        in_specs=[smem, smem, col(hq), col(hk), col(hv), col(hz),
                  pl.BlockSpec((1, chunk, LANE), lambda b, h, c: (b, c, 0)),
                  cw(cq), cw(ck), cw(cv), tl(cq), tl(ck), tl(cv),
                  pl.BlockSpec((1, hps, GDN_DK, GDN_DV), lambda b, h, c: (b, h, 0, 0)),
                  pl.BlockSpec((1, GDN_DV), lambda b, h, c: (0, 0))],
        out_specs=[pl.BlockSpec((1, chunk, w), lambda b, h, c: (b, c, h)),
                   pl.BlockSpec((1, hps, GDN_DK, GDN_DV), lambda b, h, c: (b, h, 0, 0))],
        out_shape=[jax.ShapeDtypeStruct((bn, tp, GDN_VW), F32),
                   jax.ShapeDtypeStruct((bn, GDN_HEADS, GDN_DK, GDN_DV), F32)],
        scratch_shapes=[pltpu.VMEM((hps, GDN_DK, GDN_DV), F32)] + [pltpu.VMEM((SUBLANE, w), F32)] * 3,
        compiler_params=_cparams("parallel", "parallel", "arbitrary"),
        name="gdn",
    )(a_log, dt_bias, h_main3, h_main3, h_main3, h_main3, ab3, conv_w, conv_w, conv_w,
      tail8, tail8, tail8, s0, gdn_norm.reshape(1, GDN_DV))


SWA_WIN_BLK = 128


def _swa_kernel(q_ref, kp_ref, kc_ref, vp_ref, vc_ref, o_ref, lse_ref, *, nsub):
    qi = pl.program_id(2)
    n = SWA_WIN_BLK
    ri = lax.broadcasted_iota(jnp.int32, (n, n), 0)
    ci = lax.broadcasted_iota(jnp.int32, (n, n), 1)
    mask_cur = ci <= ri
    scale = SWA_DH ** -0.5
    for sb in range(nsub):
        rows = slice(sb * n, (sb + 1) * n)
        prows = slice((sb - 1) * n, sb * n)
        mask_prev = jnp.logical_and(ci >= ri, qi > 0) if sb == 0 else ci >= ri
        for hh in range(SWA_HEADS):
            sl = slice(hh * SWA_DH, (hh + 1) * SWA_DH)
            kp = kp_ref[0, 0, :, sl] if sb == 0 else kc_ref[0, 0, prows, sl]
            vp = vp_ref[0, 0, :, sl] if sb == 0 else vc_ref[0, 0, prows, sl]
            q = q_ref[0, 0, rows, sl].astype(BF16)
            sp = jnp.where(mask_prev, _dot_nt(q, kp.astype(BF16)) * scale, NEG_BIG)
            sc = jnp.where(mask_cur, _dot_nt(q, kc_ref[0, 0, rows, sl].astype(BF16)) * scale, NEG_BIG)
            m = jnp.maximum(jnp.max(sp, axis=-1, keepdims=True), jnp.max(sc, axis=-1, keepdims=True))
            pp = jnp.exp(sp - m)
            pc = jnp.exp(sc - m)
            ssum = jnp.sum(pp, axis=-1, keepdims=True) + jnp.sum(pc, axis=-1, keepdims=True)
            acc = (jnp.dot(pp.astype(BF16), vp.astype(BF16), preferred_element_type=F32)
                   + jnp.dot(pc.astype(BF16), vc_ref[0, 0, rows, sl].astype(BF16), preferred_element_type=F32))
            o_ref[0, 0, rows, sl] = acc / ssum
            lse_ref[0, 0, rows, sl] = jnp.broadcast_to(m + jnp.log(ssum), (n, SWA_DH))


def _swa_prompt_group(q, k, v, gi, nsub):
    bn, dil, tl, _ = q.shape
    qblk = nsub * SWA_WIN_BLK
    cur = lambda b, r, i: (b, r, i, 0)
    prev = lambda b, r, i: (b, r, jnp.maximum(i * nsub - 1, 0), 0)
    blk = (1, 1, qblk, SWA_W)
    pblk = (1, 1, SWA_WIN_BLK, SWA_W)
    return pl.pallas_call(
        functools.partial(_swa_kernel, nsub=nsub),
        grid=(bn, dil, tl // qblk),
        in_specs=[pl.BlockSpec(blk, cur), pl.BlockSpec(pblk, prev), pl.BlockSpec(blk, cur),
                  pl.BlockSpec(pblk, prev), pl.BlockSpec(blk, cur)],
        out_specs=[pl.BlockSpec(blk, cur)] * 2,
        out_shape=[jax.ShapeDtypeStruct(q.shape, F32)] * 2,
        compiler_params=_cparams("parallel", "parallel", "arbitrary"),
        name=f"swa_prompt_g{gi}",
    )(q, k, k, v, v)


def _swa_sample_kernel(q_ref, kn_ref, vn_ref, c0_ref, c1_ref, c2_ref, *out_refs):
    scale = SWA_DH ** -0.5
    for gi, c_ref in enumerate((c0_ref, c1_ref, c2_ref)):
        o_ref, lse_ref = out_refs[2 * gi], out_refs[2 * gi + 1]
        q = q_ref[0, gi]
        kn = kn_ref[0, gi]
        vn = vn_ref[0, gi]
        kc = c_ref[0, :, 0, 0]
        vc = c_ref[0, :, 0, 1]
        s = jnp.sum(kc * q[None], axis=-1, keepdims=True) * scale
        sn = jnp.sum(kn * q, axis=-1, keepdims=True) * scale
        m = jnp.maximum(jnp.max(s, axis=0), sn)
        p = jnp.exp(s - m[None])
        pn = jnp.exp(sn - m)
        den = jnp.sum(p, axis=0) + pn
        o_ref[0] = (jnp.sum(p * vc, axis=0) + pn * vn) / den
        lse_ref[0] = jnp.broadcast_to(m + jnp.log(den), (SWA_HEADS, SWA_DH))


def _swa_sample(q_r, k_r, v_new, caches):
    bn = q_r.shape[0]
    nkeys = SWA_GROUPS[0][0] // SWA_GROUPS[0][1]
    views = []
    for (win, dil), cache in zip(SWA_GROUPS, caches):
        assert cache.shape[1] == win and win // dil == nkeys
        views.append(cache.reshape(bn, nkeys, dil, 2, SWA_HEADS, SWA_DH))
    heads = lambda a: a.reshape(bn, N_SWA, SWA_HEADS, SWA_DH)
    row = pl.BlockSpec((1, N_SWA, SWA_HEADS, SWA_DH), lambda b: (b, 0, 0, 0))
    cspec = pl.BlockSpec((1, nkeys, 1, 2, SWA_HEADS, SWA_DH), lambda b: (b, 0, 0, 0, 0, 0))
    ospec = pl.BlockSpec((1, SWA_HEADS, SWA_DH), lambda b: (b, 0, 0))
    outs = pl.pallas_call(
        _swa_sample_kernel,
        grid=(bn,),
        in_specs=[row, row, row, cspec, cspec, cspec],
        out_specs=[ospec] * (2 * N_SWA),
        out_shape=[jax.ShapeDtypeStruct((bn, SWA_HEADS, SWA_DH), F32)] * (2 * N_SWA),
        compiler_params=_cparams("parallel"),
        name="swa_sample",
    )(heads(q_r), heads(k_r), heads(v_new), *views)
    return [(outs[2 * g].reshape(bn, SWA_W), outs[2 * g + 1].reshape(bn, SWA_W)) for g in range(N_SWA)]


def _mix_kernel(oa_ref, o0_ref, l0_ref, o1_ref, l1_ref, o2_ref, l2_ref, ga_ref, gb_ref, x_ref,
                wa_ref, wb_ref, wo_ref, nw_ref, x1_ref, xn_ref, slab, *, dils, tm):
    def token_major(ref, dil):
        if dil == 1:
            return ref[0, 0]
        cols = []
        for hh in range(SWA_HEADS):
            for r in range(dil):
                slab[hh, pl.ds(r, tm // dil, stride=dil), :] = ref[0, r, :, hh * SWA_DH:(hh + 1) * SWA_DH]
            cols.append(slab[hh])
        return jnp.concatenate(cols, axis=1)

    os = [token_major(r, d) for r, d in zip((o0_ref, o1_ref, o2_ref), dils)]
    ls = [token_major(r, d) for r, d in zip((l0_ref, l1_ref, l2_ref), dils)]
    lmax = jnp.maximum(jnp.maximum(ls[0], ls[1]), ls[2])
    es = [jnp.exp(l - lmax) for l in ls]
    ob = (es[0] * os[0] + es[1] * os[1] + es[2] * os[2]) / (es[0] + es[1] + es[2])
    pa = jnp.dot(oa_ref[0].astype(BF16), wa_ref[...], preferred_element_type=F32)
    pb = jnp.dot(ob.astype(BF16), wb_ref[...], preferred_element_type=F32)
    merged = _sigmoid(ga_ref[0]) * pa + _sigmoid(gb_ref[0]) * pb
    x1 = x_ref[0] + jnp.dot(merged.astype(BF16), wo_ref[...], preferred_element_type=F32)
    x1_ref[0] = x1
    xn_ref[0] = x1 * lax.rsqrt(jnp.mean(x1 * x1, axis=-1, keepdims=True) + EPS) * nw_ref[...]


def _mix(o_a3, parts, dils, h3, x3, wa, wb, wo, norm_ffn, tm):
    bn, t, _ = x3.shape
    rowblk = lambda w, cb=0: pl.BlockSpec((1, tm, w), lambda b, i: (b, i, cb))
    full = lambda a: pl.BlockSpec(a.shape, lambda b, i: (0, 0))
    nw = norm_ffn.reshape(1, D_MODEL)
    part_specs, flat = [], []
    for (o, lse), dil in zip(parts, dils):
        spec = pl.BlockSpec((1, dil, tm // dil, SWA_W), lambda b, i: (b, 0, i, 0))
        part_specs += [spec, spec]
        flat += [o, lse]
    return pl.pallas_call(
        functools.partial(_mix_kernel, dils=dils, tm=tm),
        grid=(bn, t // tm),
        in_specs=[rowblk(GDN_VW)] + part_specs
                 + [rowblk(D_MODEL, COL_GA // D_MODEL), rowblk(D_MODEL, COL_GB // D_MODEL),
                    rowblk(D_MODEL), full(wa), full(wb), full(wo), full(nw)],
        out_specs=[rowblk(D_MODEL), rowblk(D_MODEL)],
        out_shape=[jax.ShapeDtypeStruct((bn, t, D_MODEL), F32)] * 2,
        scratch_shapes=[pltpu.VMEM((SWA_HEADS, tm, SWA_DH), F32)],
        compiler_params=_cparams("parallel", "parallel"),
        name="mix_out",
    )(o_a3, *flat, h3, h3, x3, wa, wb, wo, nw)


def _topk_rows(vals, k, payload=None):
    n, t = vals.shape
    idx = lax.broadcasted_iota(jnp.int32, (n, t), 0)
    out_v, out_i = [], []
    for _ in range(k):
        m = jnp.max(vals, axis=0, keepdims=True)
        am = jnp.min(jnp.where(vals == m, idx, n), axis=0, keepdims=True)
        hit = idx == am
        out_v.append(m)
        out_i.append(am if payload is None else jnp.sum(jnp.where(hit, payload, 0), axis=0, keepdims=True))
        vals = jnp.where(hit, -jnp.inf, vals)
    return jnp.concatenate(out_v, axis=0), jnp.concatenate(out_i, axis=0)


def _route_kernel(q_ref, keys_ref, eid_ref, gate_ref):
    half = PEER_DKEY // 2
    eids, gates = [], []
    for h in range(PEER_HEADS):
        tops = []
        for p in range(2):
            qs = q_ref[:, (2 * h + p) * half:(2 * h + p + 1) * half].astype(BF16)
            st = _dot_nt(keys_ref[h, p].astype(BF16), qs)
            tops.append(_topk_rows(st, PEER_TOPK))
        (v1, i1), (v2, i2) = tops
        sub = SUBLANE
        assert PEER_TOPK == 2 * sub
        rows = [(slice(0, 1), slice(0, PEER_TOPK))]
        rows += [(slice(i, i + 1), slice(0, sub)) for i in range(1, sub)]
        rows += [(slice(sub, PEER_TOPK), slice(0, 1))]
        cand = jnp.concatenate([v1[a] + v2[b] for a, b in rows], axis=0)
        ecand = jnp.concatenate([i1[a] * PEER_NKEYS + i2[b] for a, b in rows], axis=0)
        sc, e = _topk_rows(cand, PEER_TOPK, payload=ecand)
        ex = jnp.exp(sc - sc[0:1])
        gates.append(ex / jnp.sum(ex, axis=0, keepdims=True))
        eids.append(e)
    eid_ref[...] = jnp.concatenate(eids, axis=0).T
    gate_ref[...] = jnp.concatenate(gates, axis=0)


def _route(qp, sub_keys, tt):
    m = qp.shape[0]
    return pl.pallas_call(
        _route_kernel,
        grid=(m // tt,),
        in_specs=[pl.BlockSpec((tt, PEER_HEADS * PEER_DKEY), lambda i: (i, 0)),
                  pl.BlockSpec(sub_keys.shape, lambda i: (0, 0, 0, 0))],
        out_specs=[pl.BlockSpec((tt, PEER_SEL), lambda i: (i, 0)), pl.BlockSpec((PEER_SEL, tt), lambda i: (0, i))],
        out_shape=[jax.ShapeDtypeStruct((m, PEER_SEL), jnp.int32), jax.ShapeDtypeStruct((PEER_SEL, m), F32)],
        compiler_params=_cparams("parallel"),
        name="peer_route",
    )(qp, sub_keys)


PEER_TOKENS_PER_STEP = 256
PEER_NBUF = 8


PEER_CHUNKS = D_MODEL // LANE


def _pack_kernel(d_ref, u_ref, o_ref):
    hi = lax.bitcast_convert_type(d_ref[...].astype(BF16).astype(F32), jnp.uint32)
    lo = lax.bitcast_convert_type(u_ref[...].astype(BF16).astype(F32), jnp.uint32) >> 16
    word = hi | lo
    for c in range(PEER_CHUNKS):
        o_ref[:, c, :] = word[:, c * LANE:(c + 1) * LANE]


def _pack_peer_table(expert_down, expert_up, tr=256):
    e, d = expert_down.shape
    return pl.pallas_call(
        _pack_kernel,
        grid=(e // tr,),
        in_specs=[pl.BlockSpec((tr, d), lambda i: (i, 0))] * 2,
        out_specs=pl.BlockSpec((tr, PEER_CHUNKS, LANE), lambda i: (i, 0, 0)),
        out_shape=jax.ShapeDtypeStruct((e, PEER_CHUNKS, LANE), jnp.uint32),
        compiler_params=_cparams("parallel"),
        name="peer_pack",
    )(expert_down, expert_up)


def _peer_kernel(eid_ref, gate_ref, xn_ref, x1_ref, nw_ref, tbl_hbm, y_ref, *scratch, n_tok):
    nbuf = PEER_NBUF
    tbufs, (sem, xrow, yrow) = scratch[:nbuf], scratch[nbuf:]

    def issue(t, slot, rows):
        for j in rows:
            pltpu.make_async_copy(tbl_hbm.at[eid_ref[t, j]], tbufs[slot].at[:, j, :], sem.at[slot]).start(priority=j % 2)

    def wait(slot):
        pltpu.make_async_copy(tbufs[slot], tbufs[slot], sem.at[slot]).wait()

    lane = lax.broadcasted_iota(jnp.int32, (PEER_SEL, LANE), 1)
    hi_mask = jnp.uint32(0xFFFF0000)
    per_chunk = 3
    mid_rows = range(2 * PEER_CHUNKS * per_chunk, PEER_SEL)

    def compute(t, slot, t_ahead):
        def issue_rows(rows):
            if t_ahead is not None:
                issue(t_ahead, (slot - 1) % nbuf, rows)

        xrow[...] = xn_ref[pl.ds(t, 1), :]
        acc = jnp.zeros((PEER_SEL, LANE), F32)
        for c in range(PEER_CHUNKS):
            down = lax.bitcast_convert_type(tbufs[slot][c] & hi_mask, F32)
            acc = acc + down * xrow[:, c * LANE:(c + 1) * LANE]
            issue_rows(range(c * per_chunk, (c + 1) * per_chunk))
        act = jnp.sum(acc, axis=-1, keepdims=True)
        gate = jnp.sum(jnp.where(lane == t % LANE, gate_ref[t // LANE], 0.0), axis=-1, keepdims=True)
        issue_rows(mid_rows)
        gelu = 0.5 * act * (1.0 + lax.erf(act * (2.0 ** -0.5)))
        w = gate * gelu
        for c in range(PEER_CHUNKS):
            up = lax.bitcast_convert_type(tbufs[slot][c] << 16, F32)
            yrow[:, c * LANE:(c + 1) * LANE] = jnp.sum(up * w, axis=0, keepdims=True)
            issue_rows(range((PEER_CHUNKS + c) * per_chunk, (PEER_CHUNKS + c + 1) * per_chunk))
        y_ref[pl.ds(t, 1), :] = x1_ref[pl.ds(t, 1), :] + yrow[...]

    assert n_tok % nbuf == 0
    for k in range(nbuf - 1):
        issue(k, k, range(PEER_SEL))

    def group(p, carry):
        for k in range(nbuf):
            t = nbuf * p + k
            wait(k)
            compute(t, k, t + nbuf - 1)
        return carry

    lax.fori_loop(0, n_tok // nbuf - 1, group, 0)
    for k in range(nbuf):
        t = n_tok - nbuf + k
        wait(k)
        compute(t, k, t + nbuf - 1 if k == 0 else None)

    x2 = y_ref[...]
    y_ref[...] = x2 * lax.rsqrt(jnp.mean(x2 * x2, axis=-1, keepdims=True) + EPS) * nw_ref[...]


def _peer(eid, gates_t, xn2, x1, norm_final, table, n_tok):
    m = xn2.shape[0]
    kern = functools.partial(_peer_kernel, n_tok=n_tok)
    tiles = gates_t.shape[1] // LANE
    gates3 = gates_t.reshape(PEER_SEL, tiles, LANE).transpose(1, 0, 2)
    gblk = max(n_tok // LANE, 1)
    return pl.pallas_call(
        kern,
        grid=(m // n_tok,),
        in_specs=[pl.BlockSpec((gblk * LANE, PEER_SEL), lambda i: (i, 0), memory_space=pltpu.SMEM),
                  pl.BlockSpec((gblk, PEER_SEL, LANE), lambda i: (i, 0, 0)),
                  pl.BlockSpec((n_tok, D_MODEL), lambda i: (i, 0)),
                  pl.BlockSpec((n_tok, D_MODEL), lambda i: (i, 0)),
                  pl.BlockSpec((1, D_MODEL), lambda i: (0, 0)),
                  pl.BlockSpec(memory_space=pl.ANY)],
        out_specs=pl.BlockSpec((n_tok, D_MODEL), lambda i: (i, 0)),
        out_shape=jax.ShapeDtypeStruct((m, D_MODEL), F32),
        scratch_shapes=[pltpu.VMEM((PEER_CHUNKS, PEER_SEL, LANE), jnp.uint32)] * PEER_NBUF
                       + [pltpu.SemaphoreType.DMA((PEER_NBUF,)),
                          pltpu.VMEM((1, D_MODEL), F32), pltpu.VMEM((1, D_MODEL), F32)],
        compiler_params=_cparams("arbitrary"),
        name="peer_experts",
    )(eid, gates3, xn2, x1, norm_final.reshape(1, D_MODEL), table)


def _pick(m, candidates):
    for c in candidates:
        if m % c == 0:
            return c
    return m


def _layer(x, pos, tail8, s0, caches, w, chunk):
    bn, t, _ = x.shape
    m = bn * t
    x2d = x.reshape(m, D_MODEL)
    tm_big = _pick(m, (1024, 512, 256, 128, 64, 32))

    xn = _rmsnorm(x2d, w["norm_mix"], _pick(m, (512, 256, 128, 64, 32)), BF16)
    h_main = _matmul(xn, w["w_main"], tm_big, 1280, "proj_in")
    h_ab = _matmul(xn, w["w_ab"], tm_big, LANE, "proj_ab")

    tp = -(-t // chunk) * chunk
    h3 = h_main.reshape(bn, t, N_MAIN)
    ab3 = h_ab.reshape(bn, t, LANE)
    h3p = h3 if tp == t else jnp.pad(h3, ((0, 0), (0, tp - t), (0, 0)))
    ab3p = ab3 if tp == t else jnp.pad(ab3, ((0, 0), (0, tp - t), (0, 0)))
    o_a, s_new = _gdn(h3p, ab3p, w["conv_w"], tail8, s0, w["a_log"], w["dt_bias"], w["gdn_norm"], chunk, t)
    o_a = o_a[:, :t]

    cosf, sinf = _rope_tables(pos)
    kv_new = []
    if caches is None:
        qkv = _rope_split(h3, cosf, sinf, 256)
        parts = [_swa_prompt_group(q, k, v, gi, 2 if q.shape[2] % (2 * SWA_WIN_BLK) == 0 else 1)
                 for gi, (q, k, v) in enumerate(qkv)]
        dils = tuple(d for _, d in SWA_GROUPS)
        for (win, dil), (_, k, v) in zip(SWA_GROUPS, qkv):
            keep = min(win, t)
            tok = lambda a: (a[:, :, (t - keep) // dil:].transpose(0, 2, 1, 3)
                             .reshape(bn, keep, SWA_HEADS, SWA_DH))
            kv_new.append(jnp.stack([tok(k), tok(v)], axis=2))
        mix_shape = (bn, t)
    else:
        cosf, sinf = jnp.tile(cosf, (bn, 1)), jnp.tile(sinf, (bn, 1))
        q_r, k_r = _rope(h_main, cosf, sinf, _pick(m, (512, 256, 128, 64, 32)))
        v_new = h_main[:, COL_SV:COL_SV + SWA_QKV]
        parts = [(o.reshape(1, 1, m, SWA_W), lse.reshape(1, 1, m, SWA_W))
                 for o, lse in _swa_sample(q_r, k_r, v_new, caches)]
        dils = (1,) * N_SWA
        for gi in range(N_SWA):
            kk = k_r[:, gi * SWA_W:(gi + 1) * SWA_W].reshape(bn, t, SWA_HEADS, SWA_DH)
            vv = v_new[:, gi * SWA_W:(gi + 1) * SWA_W].reshape(bn, t, SWA_HEADS, SWA_DH)
            kv_new.append(jnp.stack([kk, vv], axis=2))
        mix_shape = (1, m)

    mb, mt = mix_shape
    x1, xn2 = _mix(o_a.reshape(mb, mt, GDN_VW), parts, dils, h_main.reshape(mb, mt, N_MAIN),
                   x.reshape(mb, mt, D_MODEL), w["w_br_a"], w["w_br_b"], w["w_out"], w["norm_ffn"],
                   _pick(mt, (256, 128, 64, 32)))
    x1, xn2 = x1.reshape(m, D_MODEL), xn2.reshape(m, D_MODEL)

    mp = -(-m // LANE) * LANE
    xq = xn2 if mp == m else jnp.pad(xn2, ((0, mp - m), (0, 0)))
    qp = _matmul(xq, w["w_query"], _pick(mp, (1024, 512, 256, 128)), 512, "peer_query")
    eid, gates_t = _route(qp, w["sub_keys"], LANE)
    y = _peer(eid, gates_t, xn2, x1, w["norm_final"], w["peer_table"], min(m, PEER_TOKENS_PER_STEP))
    return y.reshape(bn, t, D_MODEL), s_new, h3, kv_new


def kernel(x_prompt, x_sample, state_gdn, state_conv, cache_kv_w128, cache_kv_w512, cache_kv_w2048,
           norm_mix, w_in, conv_w, a_log, dt_bias, gdn_norm, w_br_a, w_br_b, w_out, norm_ffn,
           w_query, sub_keys, expert_down, expert_up, norm_final):
    w_main = jnp.concatenate([w_in[:, IN_GA:], w_in[:, :IN_AB], w_in[:, IN_SQ:IN_GA]], axis=1).astype(BF16)
    w_ab = jnp.pad(w_in[:, IN_AB:IN_SQ], ((0, 0), (0, LANE - 2 * GDN_HEADS))).astype(BF16)
    w = dict(norm_mix=norm_mix, w_main=w_main, w_ab=w_ab, conv_w=conv_w, a_log=a_log, dt_bias=dt_bias,
             gdn_norm=gdn_norm, w_br_a=w_br_a.astype(BF16), w_br_b=w_br_b.astype(BF16), w_out=w_out.astype(BF16),
             norm_ffn=norm_ffn, w_query=w_query.astype(BF16), sub_keys=sub_keys,
             peer_table=_pack_peer_table(expert_down, expert_up), norm_final=norm_final)

    bp, tlen = x_prompt.shape[:2]
    bs, ts = x_sample.shape[:2]
    caches = (cache_kv_w128, cache_kv_w512, cache_kv_w2048)

    tail8 = jnp.concatenate([jnp.zeros((bs, SUBLANE - (CONV_W - 1), GDN_CONV_CH), F32), state_conv], axis=1)
    y_s, gdn_s, h3_s, kv_new = _layer(
        x_sample, PAST_LEN + jnp.arange(ts, dtype=jnp.int32), tail8, state_gdn, caches, w, SUBLANE)
    conv_s = jnp.concatenate([state_conv, h3_s[:, :, COL_Q:COL_Q + GDN_CONV_CH]], axis=1)[:, ts:]
    kv_s = [jnp.concatenate([cache, new.astype(cache.dtype)], axis=1)[:, ts:] for cache, new in zip(caches, kv_new)]

    y_p, gdn_p, h3_p, kv_p = _layer(
        x_prompt, jnp.arange(tlen, dtype=jnp.int32),
        jnp.zeros((bp, SUBLANE, GDN_CONV_CH), F32), jnp.zeros((bp, GDN_HEADS, GDN_DK, GDN_DV), F32),
        None, w, CHUNK)
    conv_p = h3_p[:, tlen - (CONV_W - 1):, COL_Q:COL_Q + GDN_CONV_CH]

    return (y_p, y_s, gdn_p.astype(x_prompt.dtype), conv_p, kv_p[0], kv_p[1], kv_p[2],
            gdn_s.astype(state_gdn.dtype), conv_s, kv_s[0], kv_s[1], kv_s[2])
```

```python
import functools
import math

import jax
import jax.numpy as jnp
import numpy as np
from jax import lax
from jax.experimental import pallas as pl
from jax.experimental.pallas import tpu as pltpu

F32 = jnp.float32
BF16 = jnp.bfloat16

LANE = 128
SUBLANE = 8
VMEM_LIMIT = 56 * 1024 * 1024

D_MODEL = 2048
PAST_LEN = 16384
EPS = 1e-6
GDN_HEADS = 8
GDN_DK = 128
GDN_DV = 128
GDN_QK = GDN_HEADS * GDN_DK
GDN_VW = GDN_HEADS * GDN_DV
GDN_CONV_CH = 2 * GDN_QK + GDN_VW
CONV_W = 4
CHUNK = 64
SWA_GROUPS = ((128, 1), (512, 4), (2048, 16))
N_SWA = 3
SWA_HEADS = 4
SWA_DH = 128
SWA_W = SWA_HEADS * SWA_DH
SWA_QKV = N_SWA * SWA_W
ROT_DIM = SWA_DH // 4
ROPE_THETA = 500000.0
PEER_HEADS = 8
PEER_NKEYS = 128
PEER_DKEY = 256
PEER_TOPK = 16
PEER_SEL = PEER_HEADS * PEER_TOPK

COL_GA = 0
COL_GB = D_MODEL
COL_Q = 2 * D_MODEL
COL_K = COL_Q + GDN_QK
COL_V = COL_K + GDN_QK
COL_Z = COL_V + GDN_VW
COL_SQ = COL_Z + GDN_VW
COL_SK = COL_SQ + SWA_QKV
COL_SV = COL_SK + SWA_QKV
N_MAIN = COL_SV + SWA_QKV
IN_AB = 2 * GDN_QK + 2 * GDN_VW
IN_SQ = IN_AB + 2 * GDN_HEADS
IN_GA = IN_SQ + 3 * SWA_QKV
NEG_BIG = -1e30


def _cparams(*sem):
    return pltpu.CompilerParams(dimension_semantics=sem, vmem_limit_bytes=VMEM_LIMIT)


def _sigmoid(x):
    return 1.0 / (1.0 + jnp.exp(-x))


def _dot_sel(sel, b, dims=(((1,), (0,)), ((), ()))):
    b1 = b.astype(BF16)
    r1 = b - b1.astype(F32)
    b2 = r1.astype(BF16)
    b3 = (r1 - b2.astype(F32)).astype(BF16)
    s = sel.astype(BF16)
    dot = lambda y: lax.dot_general(s, y, dims, preferred_element_type=F32)
    return dot(b1) + (dot(b2) + dot(b3))


def _split_bf16(a):
    hi = a.astype(BF16)
    return hi, (a - hi.astype(F32)).astype(BF16)


def _dot3(a, b, dims=(((1,), (0,)), ((), ()))):
    a_hi, a_lo = _split_bf16(a)
    b_hi, b_lo = _split_bf16(b)
    dot = lambda x, y: lax.dot_general(x, y, dims, preferred_element_type=F32)
    return dot(a_hi, b_hi) + (dot(a_hi, b_lo) + dot(a_lo, b_hi))


def _dot3_nt(a, b):
    return _dot3(a, b, (((1,), (1,)), ((), ())))


def _dot3_tn(a, b):
    return _dot3(a, b, (((0,), (0,)), ((), ())))


def _dot_nt(a, b):
    return lax.dot_general(a, b, (((1,), (1,)), ((), ())), preferred_element_type=F32)


def _rmsnorm_kernel(x_ref, w_ref, o_ref):
    x = x_ref[...]
    y = x * lax.rsqrt(jnp.mean(x * x, axis=-1, keepdims=True) + EPS) * w_ref[...]
    o_ref[...] = y.astype(o_ref.dtype)


def _rmsnorm(x, w, tm, out_dtype):
    m, d = x.shape
    return pl.pallas_call(
        _rmsnorm_kernel,
        grid=(m // tm,),
        in_specs=[pl.BlockSpec((tm, d), lambda i: (i, 0)), pl.BlockSpec((1, d), lambda i: (0, 0))],
        out_specs=pl.BlockSpec((tm, d), lambda i: (i, 0)),
        out_shape=jax.ShapeDtypeStruct((m, d), out_dtype),
        compiler_params=_cparams("parallel"),
        name="rmsnorm",
    )(x, w.reshape(1, d))


def _mm_kernel(x_ref, w_ref, o_ref):
    o_ref[...] = jnp.dot(x_ref[...].astype(BF16), w_ref[...], preferred_element_type=F32)


def _matmul(x, w, tm, tn, name):
    m, k = x.shape
    n = w.shape[1]
    return pl.pallas_call(
        _mm_kernel,
        grid=(n // tn, m // tm),
        in_specs=[pl.BlockSpec((tm, k), lambda j, i: (i, 0)), pl.BlockSpec((k, tn), lambda j, i: (0, j))],
        out_specs=pl.BlockSpec((tm, tn), lambda j, i: (i, j)),
        out_shape=jax.ShapeDtypeStruct((m, n), F32),
        compiler_params=_cparams("parallel", "parallel"),
        name=name,
    )(x, w)


def _rope_kernel(q_ref, k_ref, cos_ref, sin_ref, qo_ref, ko_ref):
    cosf = cos_ref[...]
    sinf = sin_ref[...]
    lane = lax.broadcasted_iota(jnp.int32, cosf.shape, 1)
    first = lane < ROT_DIM // 2
    for src, dst in ((q_ref, qo_ref), (k_ref, ko_ref)):
        for hh in range(SWA_HEADS):
            sl = slice(hh * SWA_DH, (hh + 1) * SWA_DH)
            x = src[:, sl]
            partner = jnp.where(first, pltpu.roll(x, SWA_DH - ROT_DIM // 2, axis=1), pltpu.roll(x, ROT_DIM // 2, axis=1))
            dst[:, sl] = x * cosf + partner * sinf


def _rope(h_main, cosf, sinf, tm):
    m = h_main.shape[0]
    nt = cosf.shape[0] // tm
    qb, kb = COL_SQ // SWA_W, COL_SK // SWA_W
    return pl.pallas_call(
        _rope_kernel,
        grid=(m // tm, N_SWA),
        in_specs=[
            pl.BlockSpec((tm, SWA_W), lambda i, g: (i, qb + g)),
            pl.BlockSpec((tm, SWA_W), lambda i, g: (i, kb + g)),
            pl.BlockSpec((tm, SWA_DH), lambda i, g: (i % nt, 0)),
            pl.BlockSpec((tm, SWA_DH), lambda i, g: (i % nt, 0)),
        ],
        out_specs=[pl.BlockSpec((tm, SWA_W), lambda i, g: (i, g)), pl.BlockSpec((tm, SWA_W), lambda i, g: (i, g))],
        out_shape=[jax.ShapeDtypeStruct((m, SWA_QKV), F32), jax.ShapeDtypeStruct((m, SWA_QKV), F32)],
        compiler_params=_cparams("parallel", "parallel"),
        name="rope",
    )(h_main, h_main, cosf, sinf)


def _rope_split_kernel(*refs, dils, tm):
    ins, cos_ref, sin_ref = refs[:3 * N_SWA], refs[3 * N_SWA], refs[3 * N_SWA + 1]
    outs, slab = refs[3 * N_SWA + 2:6 * N_SWA + 2], refs[6 * N_SWA + 2]
    cosf = cos_ref[...]
    sinf = sin_ref[...]
    first = lax.broadcasted_iota(jnp.int32, cosf.shape, 1) < ROT_DIM // 2
    for gi, dil in enumerate(dils):
        for kind in range(3):
            src, dst = ins[3 * gi + kind], outs[3 * gi + kind]
            for hh in range(SWA_HEADS):
                sl = slice(hh * SWA_DH, (hh + 1) * SWA_DH)
                x = src[0, :, sl]
                if kind < 2:
                    partner = jnp.where(first, pltpu.roll(x, SWA_DH - ROT_DIM // 2, axis=1),
                                        pltpu.roll(x, ROT_DIM // 2, axis=1))
                    x = x * cosf + partner * sinf
                if dil == 1:
                    dst[0, 0, :, sl] = x
                else:
                    slab[hh] = x
                    for r in range(dil):
                        dst[0, r, :, sl] = slab[hh, pl.ds(r, tm // dil, stride=dil), :]


def _rope_split(h3, cosf, sinf, tm):
    bn, t, _ = h3.shape
    dils = tuple(d for _, d in SWA_GROUPS)
    in_specs, out_specs, out_shape = [], [], []
    for gi, dil in enumerate(dils):
        for off in (COL_SQ, COL_SK, COL_SV):
            cb = off // SWA_W + gi
            in_specs.append(pl.BlockSpec((1, tm, SWA_W), lambda b, i, cb=cb: (b, i, cb)))
            out_specs.append(pl.BlockSpec((1, dil, tm // dil, SWA_W), lambda b, i: (b, 0, i, 0)))
            out_shape.append(jax.ShapeDtypeStruct((bn, dil, t // dil, SWA_W), F32))
    tbl = pl.BlockSpec((tm, SWA_DH), lambda b, i: (i, 0))
    outs = pl.pallas_call(
        functools.partial(_rope_split_kernel, dils=dils, tm=tm),
        grid=(bn, t // tm),
        in_specs=in_specs + [tbl, tbl],
        out_specs=out_specs,
        out_shape=out_shape,
        scratch_shapes=[pltpu.VMEM((SWA_HEADS, tm, SWA_DH), F32)],
        compiler_params=_cparams("parallel", "parallel"),
        name="rope_split",
    )(*([h3] * (3 * N_SWA)), cosf, sinf)
    return [tuple(outs[3 * gi:3 * gi + 3]) for gi in range(N_SWA)]


def _rope_tables(pos):
    half = ROT_DIM // 2
    inv = ROPE_THETA ** (-jnp.arange(half, dtype=F32) * 2.0 / ROT_DIM)
    ang = pos.astype(F32)[:, None] * inv[None, :]
    cos, sin = jnp.cos(ang), jnp.sin(ang)
    n = pos.shape[0]
    cosf = jnp.concatenate([cos, cos, jnp.ones((n, SWA_DH - ROT_DIM), F32)], axis=1)
    sinf = jnp.concatenate([-sin, sin, jnp.zeros((n, SWA_DH - ROT_DIM), F32)], axis=1)
    return cosf, sinf


def _gdn_kernel(alog_ref, dtb_ref, q_ref, k_ref, v_ref, z_ref, ab_ref, cwq_ref, cwk_ref, cwv_ref,
                tq_ref, tk_ref, tv_ref, s0_ref, nw_ref, o_ref, sout_ref,
                s_scr, tailq, tailk, tailv, *, chunk, t_total, hps):
    c = pl.program_id(2)

    @pl.when(c == 0)
    def _():
        s_scr[...] = s0_ref[0]
        tailq[...] = tq_ref[0]
        tailk[...] = tk_ref[0]
        tailv[...] = tv_ref[0]

    C = chunk
    heads = range(hps)
    sls = [slice(hh * LANE, (hh + 1) * LANE) for hh in heads]
    hidx = [pl.program_id(1) * hps + hh for hh in heads]
    row8 = lax.broadcasted_iota(jnp.int32, (SUBLANE, LANE), 0)

    def conv(src_ref, tail_ref, w_ref, sl):
        raw = src_ref[0, :, sl]
        w = w_ref[:, sl]
        t8 = tail_ref[:, sl]
        y = raw * w[CONV_W - 1:CONV_W, :]
        for s in range(1, CONV_W):
            rolled = pltpu.roll(raw, s, axis=0)
            top = jnp.where(row8 < s, pltpu.roll(t8, s, axis=0), rolled[0:SUBLANE])
            sh = top if C == SUBLANE else jnp.concatenate([top, rolled[SUBLANE:]], axis=0)
            y = y + sh * w[CONV_W - 1 - s:CONV_W - s, :]
        tail_ref[:, sl] = raw[C - SUBLANE:C]
        return y * _sigmoid(y)

    def l2n(x):
        return x * lax.rsqrt(jnp.sum(x * x, axis=-1, keepdims=True) + EPS)

    q = [l2n(conv(q_ref, tailq, cwq_ref, sl)) * (GDN_DK ** -0.5) for sl in sls]
    k = [l2n(conv(k_ref, tailk, cwk_ref, sl)) for sl in sls]
    v = [conv(v_ref, tailv, cwv_ref, sl) for sl in sls]

    lane = lax.broadcasted_iota(jnp.int32, (C, LANE), 1)
    rowc = lax.broadcasted_iota(jnp.int32, (C, 1), 0)
    valid = (c * C + rowc) < t_total
    ab = ab_ref[0]
    beta, g = [], []
    for h in hidx:
        a = jnp.sum(jnp.where(lane == h, ab, 0.0), axis=-1, keepdims=True)
        b = jnp.sum(jnp.where(lane == h + GDN_HEADS, ab, 0.0), axis=-1, keepdims=True)
        sp_in = a + dtb_ref[h]
        softplus = jnp.maximum(sp_in, 0.0) + jnp.log(1.0 + jnp.exp(-jnp.abs(sp_in)))
        beta.append(jnp.where(valid, _sigmoid(b), 0.0))
        g.append(jnp.where(valid, -jnp.exp(jnp.full((C, 1), alog_ref[h], F32)) * softplus, 0.0))

    ri = lax.broadcasted_iota(jnp.int32, (C, C), 0)
    ci = lax.broadcasted_iota(jnp.int32, (C, C), 1)
    causal = ri >= ci
    strict = ri > ci
    tril = causal.astype(F32)
    e0 = (lane == 0).astype(F32)
    eye = (ri == ci).astype(F32)
    gc_b = [_dot_sel(tril, jnp.broadcast_to(gh, (C, LANE))) for gh in g]
    gc_row = [_dot_sel(e0, gch, (((1,), (1,)), ((), ()))) for gch in gc_b]
    decay = [jnp.where(causal, jnp.exp(jnp.where(causal, gch[:, :C] - grh, 0.0)), 0.0)
             for gch, grh in zip(gc_b, gc_row)]
    kb = [kh * bh for kh, bh in zip(k, beta)]
    xpow = [-jnp.where(strict, _dot3_nt(kbh, kh) * dh, 0.0) for kbh, kh, dh in zip(kb, k, decay)]
    tinv = [eye + xh for xh in xpow]
    for _ in range(int(math.log2(C)) - 1):
        xpow = [_dot3(xh, xh) for xh in xpow]
        tinv = [th + _dot3(th, xh) for th, xh in zip(tinv, xpow)]
    eg = [jnp.exp(gch) for gch in gc_b]
    value = [_dot3(th, vh * bh) for th, vh, bh in zip(tinv, v, beta)]
    kcd = [_dot3(th, kbh * egh) for th, kbh, egh in zip(tinv, kb, eg)]
    attn = [jnp.where(causal, _dot_nt(qh.astype(BF16), kh.astype(BF16)) * dh, 0.0) for qh, kh, dh in zip(q, k, decay)]
    glast = [gch[C - 1:C, :] for gch in gc_b]
    kdec = [kh * jnp.exp(glh - gch) for kh, glh, gch in zip(k, glast, gc_b)]

    s = [s_scr[hh] for hh in heads]
    v_new = [vh - _dot3(kh, sh) for vh, kh, sh in zip(value, kcd, s)]
    bdot = lambda a, b: jnp.dot(a.astype(BF16), b.astype(BF16), preferred_element_type=F32)
    o = [bdot(qh * egh, sh) + bdot(ah, vnh) for qh, egh, sh, ah, vnh in zip(q, eg, s, attn, v_new)]
    for hh in heads:
        s_scr[hh] = s[hh] * jnp.exp(glast[hh]) + _dot3_tn(kdec[hh], v_new[hh])

    nw = nw_ref[...]
    for hh, sl in zip(heads, sls):
        z = z_ref[0, :, sl]
        on = o[hh] * lax.rsqrt(jnp.mean(o[hh] * o[hh], axis=-1, keepdims=True) + EPS) * nw
        o_ref[0, :, sl] = on * (z * _sigmoid(z))

    @pl.when(c == pl.num_programs(2) - 1)
    def _():
        sout_ref[0] = s_scr[...]


GDN_HEADS_PER_STEP = 8


def _gdn(h_main3, ab3, conv_w, tail8, s0, a_log, dt_bias, gdn_norm, chunk, t_total):
    bn, tp, _ = h_main3.shape
    nc = tp // chunk
    hps = GDN_HEADS_PER_STEP
    w = hps * LANE
    hq, hk, hv, hz = COL_Q // w, COL_K // w, COL_V // w, COL_Z // w
    cq, ck, cv = 0, GDN_QK // w, 2 * GDN_QK // w
    col = lambda off: pl.BlockSpec((1, chunk, w), lambda b, h, c: (b, c, off + h))
    cw = lambda off: pl.BlockSpec((CONV_W, w), lambda b, h, c: (0, off + h))
    tl = lambda off: pl.BlockSpec((1, SUBLANE, w), lambda b, h, c: (b, 0, off + h))
    smem = pl.BlockSpec(memory_space=pltpu.SMEM)
    kern = functools.partial(_gdn_kernel, chunk=chunk, t_total=t_total, hps=hps)
    return pl.pallas_call(
        kern,
        grid=(bn, GDN_HEADS // hps, nc),
        in_specs=[smem, smem, col(hq), col(hk), col(hv), col(hz),
                  pl.BlockSpec((1, chunk, LANE), lambda b, h, c: (b, c, 0)),
                  cw(cq), cw(ck), cw(cv), tl(cq), tl(ck), tl(cv),
                  pl.BlockSpec((1, hps, GDN_DK, GDN_DV), lambda b, h, c: (b, h, 0, 0)),
                  pl.BlockSpec((1, GDN_DV), lambda b, h, c: (0, 0))],
        out_specs=[pl.BlockSpec((1, chunk, w), lambda b, h, c: (b, c, h)),
                   pl.BlockSpec((1, hps, GDN_DK, GDN_DV), lambda b, h, c: (b, h, 0, 0))],
        out_shape=[jax.ShapeDtypeStruct((bn, tp, GDN_VW), F32),
                   jax.ShapeDtypeStruct((bn, GDN_HEADS, GDN_DK, GDN_DV), F32)],
        scratch_shapes=[pltpu.VMEM((hps, GDN_DK, GDN_DV), F32)] + [pltpu.VMEM((SUBLANE, w), F32)] * 3,
        compiler_params=_cparams("parallel", "parallel", "arbitrary"),
        name="gdn",
    )(a_log, dt_bias, h_main3, h_main3, h_main3, h_main3, ab3, conv_w, conv_w, conv_w,
      tail8, tail8, tail8, s0, gdn_norm.reshape(1, GDN_DV))


SWA_WIN_BLK = 128


def _swa_kernel(q_ref, kp_ref, kc_ref, vp_ref, vc_ref, o_ref, lse_ref, *, nsub):
    qi = pl.program_id(2)
    n = SWA_WIN_BLK
    ri = lax.broadcasted_iota(jnp.int32, (n, n), 0)
    ci = lax.broadcasted_iota(jnp.int32, (n, n), 1)
    mask_cur = ci <= ri
    scale = SWA_DH ** -0.5
    for sb in range(nsub):
        rows = slice(sb * n, (sb + 1) * n)
        prows = slice((sb - 1) * n, sb * n)
        mask_prev = jnp.logical_and(ci >= ri, qi > 0) if sb == 0 else ci >= ri
        for hh in range(SWA_HEADS):
            sl = slice(hh * SWA_DH, (hh + 1) * SWA_DH)
            kp = kp_ref[0, 0, :, sl] if sb == 0 else kc_ref[0, 0, prows, sl]
            vp = vp_ref[0, 0, :, sl] if sb == 0 else vc_ref[0, 0, prows, sl]
            q = q_ref[0, 0, rows, sl].astype(BF16)
            sp = jnp.where(mask_prev, _dot_nt(q, kp.astype(BF16)) * scale, NEG_BIG)
            sc = jnp.where(mask_cur, _dot_nt(q, kc_ref[0, 0, rows, sl].astype(BF16)) * scale, NEG_BIG)
            m = jnp.maximum(jnp.max(sp, axis=-1, keepdims=True), jnp.max(sc, axis=-1, keepdims=True))
            pp = jnp.exp(sp - m)
            pc = jnp.exp(sc - m)
            ssum = jnp.sum(pp, axis=-1, keepdims=True) + jnp.sum(pc, axis=-1, keepdims=True)
            acc = (jnp.dot(pp.astype(BF16), vp.astype(BF16), preferred_element_type=F32)
                   + jnp.dot(pc.astype(BF16), vc_ref[0, 0, rows, sl].astype(BF16), preferred_element_type=F32))
            o_ref[0, 0, rows, sl] = acc / ssum
            lse_ref[0, 0, rows, sl] = jnp.broadcast_to(m + jnp.log(ssum), (n, SWA_DH))


def _swa_prompt_group(q, k, v, gi, nsub):
    bn, dil, tl, _ = q.shape
    qblk = nsub * SWA_WIN_BLK
    cur = lambda b, r, i: (b, r, i, 0)
    prev = lambda b, r, i: (b, r, jnp.maximum(i * nsub - 1, 0), 0)
    blk = (1, 1, qblk, SWA_W)
    pblk = (1, 1, SWA_WIN_BLK, SWA_W)
    return pl.pallas_call(
        functools.partial(_swa_kernel, nsub=nsub),
        grid=(bn, dil, tl // qblk),
        in_specs=[pl.BlockSpec(blk, cur), pl.BlockSpec(pblk, prev), pl.BlockSpec(blk, cur),
                  pl.BlockSpec(pblk, prev), pl.BlockSpec(blk, cur)],
        out_specs=[pl.BlockSpec(blk, cur)] * 2,
        out_shape=[jax.ShapeDtypeStruct(q.shape, F32)] * 2,
        compiler_params=_cparams("parallel", "parallel", "arbitrary"),
        name=f"swa_prompt_g{gi}",
    )(q, k, k, v, v)


def _swa_sample_kernel(q_ref, kn_ref, vn_ref, c0_ref, c1_ref, c2_ref, *out_refs):
    scale = SWA_DH ** -0.5
    for gi, c_ref in enumerate((c0_ref, c1_ref, c2_ref)):
        o_ref, lse_ref = out_refs[2 * gi], out_refs[2 * gi + 1]
        q = q_ref[0, gi]
        kn = kn_ref[0, gi]
        vn = vn_ref[0, gi]
        kc = c_ref[0, :, 0, 0]
        vc = c_ref[0, :, 0, 1]
        s = jnp.sum(kc * q[None], axis=-1, keepdims=True) * scale
        sn = jnp.sum(kn * q, axis=-1, keepdims=True) * scale
        m = jnp.maximum(jnp.max(s, axis=0), sn)
        p = jnp.exp(s - m[None])
        pn = jnp.exp(sn - m)
        den = jnp.sum(p, axis=0) + pn
        o_ref[0] = (jnp.sum(p * vc, axis=0) + pn * vn) / den
        lse_ref[0] = jnp.broadcast_to(m + jnp.log(den), (SWA_HEADS, SWA_DH))


def _swa_sample(q_r, k_r, v_new, caches):
    bn = q_r.shape[0]
    nkeys = SWA_GROUPS[0][0] // SWA_GROUPS[0][1]
    views = []
    for (win, dil), cache in zip(SWA_GROUPS, caches):
        assert cache.shape[1] == win and win // dil == nkeys
        views.append(cache.reshape(bn, nkeys, dil, 2, SWA_HEADS, SWA_DH))
    heads = lambda a: a.reshape(bn, N_SWA, SWA_HEADS, SWA_DH)
    row = pl.BlockSpec((1, N_SWA, SWA_HEADS, SWA_DH), lambda b: (b, 0, 0, 0))
    cspec = pl.BlockSpec((1, nkeys, 1, 2, SWA_HEADS, SWA_DH), lambda b: (b, 0, 0, 0, 0, 0))
    ospec = pl.BlockSpec((1, SWA_HEADS, SWA_DH), lambda b: (b, 0, 0))
    outs = pl.pallas_call(
        _swa_sample_kernel,
        grid=(bn,),
        in_specs=[row, row, row, cspec, cspec, cspec],
        out_specs=[ospec] * (2 * N_SWA),
        out_shape=[jax.ShapeDtypeStruct((bn, SWA_HEADS, SWA_DH), F32)] * (2 * N_SWA),
        compiler_params=_cparams("parallel"),
        name="swa_sample",
    )(heads(q_r), heads(k_r), heads(v_new), *views)
    return [(outs[2 * g].reshape(bn, SWA_W), outs[2 * g + 1].reshape(bn, SWA_W)) for g in range(N_SWA)]


def _mix_kernel(oa_ref, o0_ref, l0_ref, o1_ref, l1_ref, o2_ref, l2_ref, ga_ref, gb_ref, x_ref,
                wa_ref, wb_ref, wo_ref, nw_ref, x1_ref, xn_ref, slab, *, dils, tm):
    def token_major(ref, dil):
        if dil == 1:
            return ref[0, 0]
        cols = []
        for hh in range(SWA_HEADS):
            for r in range(dil):
                slab[hh, pl.ds(r, tm // dil, stride=dil), :] = ref[0, r, :, hh * SWA_DH:(hh + 1) * SWA_DH]
            cols.append(slab[hh])
        return jnp.concatenate(cols, axis=1)

    os = [token_major(r, d) for r, d in zip((o0_ref, o1_ref, o2_ref), dils)]
    ls = [token_major(r, d) for r, d in zip((l0_ref, l1_ref, l2_ref), dils)]
    lmax = jnp.maximum(jnp.maximum(ls[0], ls[1]), ls[2])
    es = [jnp.exp(l - lmax) for l in ls]
    ob = (es[0] * os[0] + es[1] * os[1] + es[2] * os[2]) / (es[0] + es[1] + es[2])
    pa = jnp.dot(oa_ref[0].astype(BF16), wa_ref[...], preferred_element_type=F32)
    pb = jnp.dot(ob.astype(BF16), wb_ref[...], preferred_element_type=F32)
    merged = _sigmoid(ga_ref[0]) * pa + _sigmoid(gb_ref[0]) * pb
    x1 = x_ref[0] + jnp.dot(merged.astype(BF16), wo_ref[...], preferred_element_type=F32)
    x1_ref[0] = x1
    xn_ref[0] = x1 * lax.rsqrt(jnp.mean(x1 * x1, axis=-1, keepdims=True) + EPS) * nw_ref[...]


def _mix(o_a3, parts, dils, h3, x3, wa, wb, wo, norm_ffn, tm):
    bn, t, _ = x3.shape
    rowblk = lambda w, cb=0: pl.BlockSpec((1, tm, w), lambda b, i: (b, i, cb))
    full = lambda a: pl.BlockSpec(a.shape, lambda b, i: (0, 0))
    nw = norm_ffn.reshape(1, D_MODEL)
    part_specs, flat = [], []
    for (o, lse), dil in zip(parts, dils):
        spec = pl.BlockSpec((1, dil, tm // dil, SWA_W), lambda b, i: (b, 0, i, 0))
        part_specs += [spec, spec]
        flat += [o, lse]
    return pl.pallas_call(
        functools.partial(_mix_kernel, dils=dils, tm=tm),
        grid=(bn, t // tm),
        in_specs=[rowblk(GDN_VW)] + part_specs
                 + [rowblk(D_MODEL, COL_GA // D_MODEL), rowblk(D_MODEL, COL_GB // D_MODEL),
                    rowblk(D_MODEL), full(wa), full(wb), full(wo), full(nw)],
        out_specs=[rowblk(D_MODEL), rowblk(D_MODEL)],
        out_shape=[jax.ShapeDtypeStruct((bn, t, D_MODEL), F32)] * 2,
        scratch_shapes=[pltpu.VMEM((SWA_HEADS, tm, SWA_DH), F32)],
        compiler_params=_cparams("parallel", "parallel"),
        name="mix_out",
    )(o_a3, *flat, h3, h3, x3, wa, wb, wo, nw)


def _topk_rows(vals, k, payload=None):
    n, t = vals.shape
    idx = lax.broadcasted_iota(jnp.int32, (n, t), 0).astype(F32)
    out_v, out_i = [], []
    for _ in range(k):
        m = jnp.max(vals, axis=0, keepdims=True)
        am = jnp.min(jnp.where(vals == m, idx, float(n)), axis=0, keepdims=True)
        hit = idx == am
        out_v.append(m)
        out_i.append(am if payload is None else jnp.sum(jnp.where(hit, payload, 0.0), axis=0, keepdims=True))
        vals = jnp.where(hit, -jnp.inf, vals)
    return jnp.concatenate(out_v, axis=0), jnp.concatenate(out_i, axis=0)


def _route_kernel(q_ref, keys_ref, eid_ref, gate_ref):
    half = PEER_DKEY // 2
    eids, gates = [], []
    for h in range(PEER_HEADS):
        tops = []
        for p in range(2):
            qs = q_ref[:, (2 * h + p) * half:(2 * h + p + 1) * half].astype(BF16)
            st = _dot_nt(keys_ref[h, p].astype(BF16), qs)
            tops.append(_topk_rows(st, PEER_TOPK))
        (v1, i1), (v2, i2) = tops
        sub = SUBLANE
        assert PEER_TOPK == 2 * sub
        rows = [(slice(0, 1), slice(0, PEER_TOPK))]
        rows += [(slice(i, i + 1), slice(0, sub)) for i in range(1, sub)]
        rows += [(slice(sub, PEER_TOPK), slice(0, 1))]
        cand = jnp.concatenate([v1[a] + v2[b] for a, b in rows], axis=0)
        ecand = jnp.concatenate([i1[a] * float(PEER_NKEYS) + i2[b] for a, b in rows], axis=0)
        sc, e = _topk_rows(cand, PEER_TOPK, payload=ecand)
        ex = jnp.exp(sc - sc[0:1])
        gates.append(ex / jnp.sum(ex, axis=0, keepdims=True))
        eids.append(e)
    eid_ref[...] = jnp.concatenate(eids, axis=0).astype(jnp.int32).T
    gate_ref[...] = jnp.concatenate(gates, axis=0)


def _route(qp, sub_keys, tt):
    m = qp.shape[0]
    return pl.pallas_call(
        _route_kernel,
        grid=(m // tt,),
        in_specs=[pl.BlockSpec((tt, PEER_HEADS * PEER_DKEY), lambda i: (i, 0)),
                  pl.BlockSpec(sub_keys.shape, lambda i: (0, 0, 0, 0))],
        out_specs=[pl.BlockSpec((tt, PEER_SEL), lambda i: (i, 0)), pl.BlockSpec((PEER_SEL, tt), lambda i: (0, i))],
        out_shape=[jax.ShapeDtypeStruct((m, PEER_SEL), jnp.int32), jax.ShapeDtypeStruct((PEER_SEL, m), F32)],
        compiler_params=_cparams("parallel"),
        name="peer_route",
    )(qp, sub_keys)


PEER_TOKENS_PER_STEP = 256
PEER_NBUF = 8


PEER_CHUNKS = D_MODEL // LANE


def _pack_kernel(d_ref, u_ref, o_ref):
    hi = lax.bitcast_convert_type(d_ref[...].astype(BF16).astype(F32), jnp.uint32)
    lo = lax.bitcast_convert_type(u_ref[...].astype(BF16).astype(F32), jnp.uint32) >> 16
    word = hi | lo
    for c in range(PEER_CHUNKS):
        o_ref[:, c, :] = word[:, c * LANE:(c + 1) * LANE]


def _pack_peer_table(expert_down, expert_up, tr=256):
    e, d = expert_down.shape
    return pl.pallas_call(
        _pack_kernel,
        grid=(e // tr,),
        in_specs=[pl.BlockSpec((tr, d), lambda i: (i, 0))] * 2,
        out_specs=pl.BlockSpec((tr, PEER_CHUNKS, LANE), lambda i: (i, 0, 0)),
        out_shape=jax.ShapeDtypeStruct((e, PEER_CHUNKS, LANE), jnp.uint32),
        compiler_params=_cparams("parallel"),
        name="peer_pack",
    )(expert_down, expert_up)


def _peer_kernel(eid_ref, gate_ref, xn_ref, x1_ref, nw_ref, tbl_hbm, y_ref, *scratch, n_tok):
    nbuf = PEER_NBUF
    tbufs, (sem, xrow, yrow) = scratch[:nbuf], scratch[nbuf:]

    def issue(t, slot, rows):
        for j in rows:
            pltpu.make_async_copy(tbl_hbm.at[eid_ref[t, j]], tbufs[slot].at[:, j, :], sem.at[slot]).start(priority=j % 2)

    def wait(slot):
        pltpu.make_async_copy(tbufs[slot], tbufs[slot], sem.at[slot]).wait()

    lane = lax.broadcasted_iota(jnp.int32, (PEER_SEL, LANE), 1)
    hi_mask = jnp.uint32(0xFFFF0000)
    per_chunk = 3
    mid_rows = range(2 * PEER_CHUNKS * per_chunk, PEER_SEL)

    def compute(t, slot, t_ahead):
        def issue_rows(rows):
            if t_ahead is not None:
                issue(t_ahead, (slot - 1) % nbuf, rows)

        xrow[...] = xn_ref[pl.ds(t, 1), :]
        acc = jnp.zeros((PEER_SEL, LANE), F32)
        for c in range(PEER_CHUNKS):
            down = lax.bitcast_convert_type(tbufs[slot][c] & hi_mask, F32)
            acc = acc + down * xrow[:, c * LANE:(c + 1) * LANE]
            issue_rows(range(c * per_chunk, (c + 1) * per_chunk))
        act = jnp.sum(acc, axis=-1, keepdims=True)
        gate = jnp.sum(jnp.where(lane == t % LANE, gate_ref[t // LANE], 0.0), axis=-1, keepdims=True)
        issue_rows(mid_rows)
        gelu = 0.5 * act * (1.0 + lax.erf(act * (2.0 ** -0.5)))
        w = gate * gelu
        for c in range(PEER_CHUNKS):
            up = lax.bitcast_convert_type(tbufs[slot][c] << 16, F32)
            yrow[:, c * LANE:(c + 1) * LANE] = jnp.sum(up * w, axis=0, keepdims=True)
            issue_rows(range((PEER_CHUNKS + c) * per_chunk, (PEER_CHUNKS + c + 1) * per_chunk))
        y_ref[pl.ds(t, 1), :] = x1_ref[pl.ds(t, 1), :] + yrow[...]

    assert n_tok % nbuf == 0
    for k in range(nbuf - 1):
        issue(k, k, range(PEER_SEL))

    def group(p, carry):
        for k in range(nbuf):
            t = nbuf * p + k
            wait(k)
            compute(t, k, t + nbuf - 1)
        return carry

    lax.fori_loop(0, n_tok // nbuf - 1, group, 0)
    for k in range(nbuf):
        t = n_tok - nbuf + k
        wait(k)
        compute(t, k, t + nbuf - 1 if k == 0 else None)

    x2 = y_ref[...]
    y_ref[...] = x2 * lax.rsqrt(jnp.mean(x2 * x2, axis=-1, keepdims=True) + EPS) * nw_ref[...]


def _peer(eid, gates_t, xn2, x1, norm_final, table, n_tok):
    m = xn2.shape[0]
    kern = functools.partial(_peer_kernel, n_tok=n_tok)
    tiles = gates_t.shape[1] // LANE
    gates3 = gates_t.reshape(PEER_SEL, tiles, LANE).transpose(1, 0, 2)
    gblk = max(n_tok // LANE, 1)
    return pl.pallas_call(
        kern,
        grid=(m // n_tok,),
        in_specs=[pl.BlockSpec((gblk * LANE, PEER_SEL), lambda i: (i, 0), memory_space=pltpu.SMEM),
                  pl.BlockSpec((gblk, PEER_SEL, LANE), lambda i: (i, 0, 0)),
                  pl.BlockSpec((n_tok, D_MODEL), lambda i: (i, 0)),
                  pl.BlockSpec((n_tok, D_MODEL), lambda i: (i, 0)),
                  pl.BlockSpec((1, D_MODEL), lambda i: (0, 0)),
                  pl.BlockSpec(memory_space=pl.ANY)],
        out_specs=pl.BlockSpec((n_tok, D_MODEL), lambda i: (i, 0)),
        out_shape=jax.ShapeDtypeStruct((m, D_MODEL), F32),
        scratch_shapes=[pltpu.VMEM((PEER_CHUNKS, PEER_SEL, LANE), jnp.uint32)] * PEER_NBUF
                       + [pltpu.SemaphoreType.DMA((PEER_NBUF,)),
                          pltpu.VMEM((1, D_MODEL), F32), pltpu.VMEM((1, D_MODEL), F32)],
        compiler_params=_cparams("arbitrary"),
        name="peer_experts",
    )(eid, gates3, xn2, x1, norm_final.reshape(1, D_MODEL), table)


def _pick(m, candidates):
    for c in candidates:
        if m % c == 0:
            return c
    return m


def _layer(x, pos, tail8, s0, caches, w, chunk):
    bn, t, _ = x.shape
    m = bn * t
    x2d = x.reshape(m, D_MODEL)
    tm_big = _pick(m, (1024, 512, 256, 128, 64, 32))

    xn = _rmsnorm(x2d, w["norm_mix"], _pick(m, (512, 256, 128, 64, 32)), BF16)
    h_main = _matmul(xn, w["w_main"], tm_big, 1280, "proj_in")
    h_ab = _matmul(xn, w["w_ab"], tm_big, LANE, "proj_ab")

    tp = -(-t // chunk) * chunk
    h3 = h_main.reshape(bn, t, N_MAIN)
    ab3 = h_ab.reshape(bn, t, LANE)
    h3p = h3 if tp == t else jnp.pad(h3, ((0, 0), (0, tp - t), (0, 0)))
    ab3p = ab3 if tp == t else jnp.pad(ab3, ((0, 0), (0, tp - t), (0, 0)))
    o_a, s_new = _gdn(h3p, ab3p, w["conv_w"], tail8, s0, w["a_log"], w["dt_bias"], w["gdn_norm"], chunk, t)
    o_a = o_a[:, :t]

    cosf, sinf = _rope_tables(pos)
    kv_new = []
    if caches is None:
        qkv = _rope_split(h3, cosf, sinf, 256)
        parts = [_swa_prompt_group(q, k, v, gi, 2 if q.shape[2] % (2 * SWA_WIN_BLK) == 0 else 1)
                 for gi, (q, k, v) in enumerate(qkv)]
        dils = tuple(d for _, d in SWA_GROUPS)
        for (win, dil), (_, k, v) in zip(SWA_GROUPS, qkv):
            keep = min(win, t)
            tok = lambda a: (a[:, :, (t - keep) // dil:].transpose(0, 2, 1, 3)
                             .reshape(bn, keep, SWA_HEADS, SWA_DH))
            kv_new.append(jnp.stack([tok(k), tok(v)], axis=2))
        mix_shape = (bn, t)
    else:
        cosf, sinf = jnp.tile(cosf, (bn, 1)), jnp.tile(sinf, (bn, 1))
        q_r, k_r = _rope(h_main, cosf, sinf, _pick(m, (512, 256, 128, 64, 32)))
        v_new = h_main[:, COL_SV:COL_SV + SWA_QKV]
        parts = [(o.reshape(1, 1, m, SWA_W), lse.reshape(1, 1, m, SWA_W))
                 for o, lse in _swa_sample(q_r, k_r, v_new, caches)]
        dils = (1,) * N_SWA
        for gi in range(N_SWA):
            kk = k_r[:, gi * SWA_W:(gi + 1) * SWA_W].reshape(bn, t, SWA_HEADS, SWA_DH)
            vv = v_new[:, gi * SWA_W:(gi + 1) * SWA_W].reshape(bn, t, SWA_HEADS, SWA_DH)
            kv_new.append(jnp.stack([kk, vv], axis=2))
        mix_shape = (1, m)

    mb, mt = mix_shape
    x1, xn2 = _mix(o_a.reshape(mb, mt, GDN_VW), parts, dils, h_main.reshape(mb, mt, N_MAIN),
                   x.reshape(mb, mt, D_MODEL), w["w_br_a"], w["w_br_b"], w["w_out"], w["norm_ffn"],
                   _pick(mt, (256, 128, 64, 32)))
    x1, xn2 = x1.reshape(m, D_MODEL), xn2.reshape(m, D_MODEL)

    mp = -(-m // LANE) * LANE
    xq = xn2 if mp == m else jnp.pad(xn2, ((0, mp - m), (0, 0)))
    qp = _matmul(xq, w["w_query"], _pick(mp, (1024, 512, 256, 128)), 512, "peer_query")
    eid, gates_t = _route(qp, w["sub_keys"], LANE)
    y = _peer(eid, gates_t, xn2, x1, w["norm_final"], w["peer_table"], min(m, PEER_TOKENS_PER_STEP))
    return y.reshape(bn, t, D_MODEL), s_new, h3, kv_new


def kernel(x_prompt, x_sample, state_gdn, state_conv, cache_kv_w128, cache_kv_w512, cache_kv_w2048,
           norm_mix, w_in, conv_w, a_log, dt_bias, gdn_norm, w_br_a, w_br_b, w_out, norm_ffn,
           w_query, sub_keys, expert_down, expert_up, norm_final):
    w_main = jnp.concatenate([w_in[:, IN_GA:], w_in[:, :IN_AB], w_in[:, IN_SQ:IN_GA]], axis=1).astype(BF16)
    w_ab = jnp.pad(w_in[:, IN_AB:IN_SQ], ((0, 0), (0, LANE - 2 * GDN_HEADS))).astype(BF16)
    w = dict(norm_mix=norm_mix, w_main=w_main, w_ab=w_ab, conv_w=conv_w, a_log=a_log, dt_bias=dt_bias,
             gdn_norm=gdn_norm, w_br_a=w_br_a.astype(BF16), w_br_b=w_br_b.astype(BF16), w_out=w_out.astype(BF16),
             norm_ffn=norm_ffn, w_query=w_query.astype(BF16), sub_keys=sub_keys,
             peer_table=_pack_peer_table(expert_down, expert_up), norm_final=norm_final)

    bp, tlen = x_prompt.shape[:2]
    bs, ts = x_sample.shape[:2]
    caches = (cache_kv_w128, cache_kv_w512, cache_kv_w2048)

    tail8 = jnp.concatenate([jnp.zeros((bs, SUBLANE - (CONV_W - 1), GDN_CONV_CH), F32), state_conv], axis=1)
    y_s, gdn_s, h3_s, kv_new = _layer(
        x_sample, PAST_LEN + jnp.arange(ts, dtype=jnp.int32), tail8, state_gdn, caches, w, SUBLANE)
    conv_s = jnp.concatenate([state_conv, h3_s[:, :, COL_Q:COL_Q + GDN_CONV_CH]], axis=1)[:, ts:]
    kv_s = [jnp.concatenate([cache, new.astype(cache.dtype)], axis=1)[:, ts:] for cache, new in zip(caches, kv_new)]

    y_p, gdn_p, h3_p, kv_p = _layer(
        x_prompt, jnp.arange(tlen, dtype=jnp.int32),
        jnp.zeros((bp, SUBLANE, GDN_CONV_CH), F32), jnp.zeros((bp, GDN_HEADS, GDN_DK, GDN_DV), F32),
        None, w, CHUNK)
    conv_p = h3_p[:, tlen - (CONV_W - 1):, COL_Q:COL_Q + GDN_CONV_CH]

    return (y_p, y_s, gdn_p.astype(x_prompt.dtype), conv_p, kv_p[0], kv_p[1], kv_p[2],
            gdn_s.astype(state_gdn.dtype), conv_s, kv_s[0], kv_s[1], kv_s[2])
```

```python
import functools
import math

import jax
import jax.numpy as jnp
import numpy as np
from jax import lax
from jax.experimental import pallas as pl
from jax.experimental.pallas import tpu as pltpu

F32 = jnp.float32
BF16 = jnp.bfloat16

LANE = 128
SUBLANE = 8
VMEM_LIMIT = 56 * 1024 * 1024

D_MODEL = 2048
PAST_LEN = 16384
EPS = 1e-6
GDN_HEADS = 8
GDN_DK = 128
GDN_DV = 128
GDN_QK = GDN_HEADS * GDN_DK
GDN_VW = GDN_HEADS * GDN_DV
GDN_CONV_CH = 2 * GDN_QK + GDN_VW
CONV_W = 4
CHUNK = 64
SWA_GROUPS = ((128, 1), (512, 4), (2048, 16))
N_SWA = 3
SWA_HEADS = 4
SWA_DH = 128
SWA_W = SWA_HEADS * SWA_DH
SWA_QKV = N_SWA * SWA_W
ROT_DIM = SWA_DH // 4
ROPE_THETA = 500000.0
PEER_HEADS = 8
PEER_NKEYS = 128
PEER_DKEY = 256
PEER_TOPK = 16
PEER_SEL = PEER_HEADS * PEER_TOPK

COL_GA = 0
COL_GB = D_MODEL
COL_Q = 2 * D_MODEL
COL_K = COL_Q + GDN_QK
COL_V = COL_K + GDN_QK
COL_Z = COL_V + GDN_VW
COL_SQ = COL_Z + GDN_VW
COL_SK = COL_SQ + SWA_QKV
COL_SV = COL_SK + SWA_QKV
N_MAIN = COL_SV + SWA_QKV
IN_AB = 2 * GDN_QK + 2 * GDN_VW
IN_SQ = IN_AB + 2 * GDN_HEADS
IN_GA = IN_SQ + 3 * SWA_QKV
NEG_BIG = -1e30


def _cparams(*sem):
    return pltpu.CompilerParams(dimension_semantics=sem, vmem_limit_bytes=VMEM_LIMIT)


def _sigmoid(x):
    return 1.0 / (1.0 + jnp.exp(-x))


def _dot_sel(sel, b, dims=(((1,), (0,)), ((), ()))):
    b1 = b.astype(BF16)
    r1 = b - b1.astype(F32)
    b2 = r1.astype(BF16)
    b3 = (r1 - b2.astype(F32)).astype(BF16)
    s = sel.astype(BF16)
    dot = lambda y: lax.dot_general(s, y, dims, preferred_element_type=F32)
    return dot(b1) + (dot(b2) + dot(b3))


def _split_bf16(a):
    hi = a.astype(BF16)
    return hi, (a - hi.astype(F32)).astype(BF16)


def _dot3(a, b, dims=(((1,), (0,)), ((), ()))):
    a_hi, a_lo = _split_bf16(a)
    b_hi, b_lo = _split_bf16(b)
    dot = lambda x, y: lax.dot_general(x, y, dims, preferred_element_type=F32)
    return dot(a_hi, b_hi) + (dot(a_hi, b_lo) + dot(a_lo, b_hi))


def _dot3_nt(a, b):
    return _dot3(a, b, (((1,), (1,)), ((), ())))


def _dot3_tn(a, b):
    return _dot3(a, b, (((0,), (0,)), ((), ())))


def _dot_nt(a, b):
    return lax.dot_general(a, b, (((1,), (1,)), ((), ())), preferred_element_type=F32)


def _rmsnorm_kernel(x_ref, w_ref, o_ref):
    x = x_ref[...]
    y = x * lax.rsqrt(jnp.mean(x * x, axis=-1, keepdims=True) + EPS) * w_ref[...]
    o_ref[...] = y.astype(o_ref.dtype)


def _rmsnorm(x, w, tm, out_dtype):
    m, d = x.shape
    return pl.pallas_call(
        _rmsnorm_kernel,
        grid=(m // tm,),
        in_specs=[pl.BlockSpec((tm, d), lambda i: (i, 0)), pl.BlockSpec((1, d), lambda i: (0, 0))],
        out_specs=pl.BlockSpec((tm, d), lambda i: (i, 0)),
        out_shape=jax.ShapeDtypeStruct((m, d), out_dtype),
        compiler_params=_cparams("parallel"),
        name="rmsnorm",
    )(x, w.reshape(1, d))


def _mm_kernel(x_ref, w_ref, o_ref):
    o_ref[...] = jnp.dot(x_ref[...].astype(BF16), w_ref[...], preferred_element_type=F32)


def _matmul(x, w, tm, tn, name):
    m, k = x.shape
    n = w.shape[1]
    return pl.pallas_call(
        _mm_kernel,
        grid=(n // tn, m // tm),
        in_specs=[pl.BlockSpec((tm, k), lambda j, i: (i, 0)), pl.BlockSpec((k, tn), lambda j, i: (0, j))],
        out_specs=pl.BlockSpec((tm, tn), lambda j, i: (i, j)),
        out_shape=jax.ShapeDtypeStruct((m, n), F32),
        compiler_params=_cparams("parallel", "parallel"),
        name=name,
    )(x, w)


def _rope_kernel(q_ref, k_ref, cos_ref, sin_ref, qo_ref, ko_ref):
    cosf = cos_ref[...]
    sinf = sin_ref[...]
    lane = lax.broadcasted_iota(jnp.int32, cosf.shape, 1)
    first = lane < ROT_DIM // 2
    for src, dst in ((q_ref, qo_ref), (k_ref, ko_ref)):
        for hh in range(SWA_HEADS):
            sl = slice(hh * SWA_DH, (hh + 1) * SWA_DH)
            x = src[:, sl]
            partner = jnp.where(first, pltpu.roll(x, SWA_DH - ROT_DIM // 2, axis=1), pltpu.roll(x, ROT_DIM // 2, axis=1))
            dst[:, sl] = x * cosf + partner * sinf


def _rope(h_main, cosf, sinf, tm):
    m = h_main.shape[0]
    nt = cosf.shape[0] // tm
    qb, kb = COL_SQ // SWA_W, COL_SK // SWA_W
    return pl.pallas_call(
        _rope_kernel,
        grid=(m // tm, N_SWA),
        in_specs=[
            pl.BlockSpec((tm, SWA_W), lambda i, g: (i, qb + g)),
            pl.BlockSpec((tm, SWA_W), lambda i, g: (i, kb + g)),
            pl.BlockSpec((tm, SWA_DH), lambda i, g: (i % nt, 0)),
            pl.BlockSpec((tm, SWA_DH), lambda i, g: (i % nt, 0)),
        ],
        out_specs=[pl.BlockSpec((tm, SWA_W), lambda i, g: (i, g)), pl.BlockSpec((tm, SWA_W), lambda i, g: (i, g))],
        out_shape=[jax.ShapeDtypeStruct((m, SWA_QKV), F32), jax.ShapeDtypeStruct((m, SWA_QKV), F32)],
        compiler_params=_cparams("parallel", "parallel"),
        name="rope",
    )(h_main, h_main, cosf, sinf)


def _rope_split_kernel(*refs, dils, tm):
    ins, cos_ref, sin_ref = refs[:3 * N_SWA], refs[3 * N_SWA], refs[3 * N_SWA + 1]
    outs, slab = refs[3 * N_SWA + 2:6 * N_SWA + 2], refs[6 * N_SWA + 2]
    cosf = cos_ref[...]
    sinf = sin_ref[...]
    first = lax.broadcasted_iota(jnp.int32, cosf.shape, 1) < ROT_DIM // 2
    for gi, dil in enumerate(dils):
        for kind in range(3):
            src, dst = ins[3 * gi + kind], outs[3 * gi + kind]
            for hh in range(SWA_HEADS):
                sl = slice(hh * SWA_DH, (hh + 1) * SWA_DH)
                x = src[0, :, sl]
                if kind < 2:
                    partner = jnp.where(first, pltpu.roll(x, SWA_DH - ROT_DIM // 2, axis=1),
                                        pltpu.roll(x, ROT_DIM // 2, axis=1))
                    x = x * cosf + partner * sinf
                if dil == 1:
                    dst[0, 0, :, sl] = x
                else:
                    slab[hh] = x
                    for r in range(dil):
                        dst[0, r, :, sl] = slab[hh, pl.ds(r, tm // dil, stride=dil), :]


def _rope_split(h3, cosf, sinf, tm):
    bn, t, _ = h3.shape
    dils = tuple(d for _, d in SWA_GROUPS)
    in_specs, out_specs, out_shape = [], [], []
    for gi, dil in enumerate(dils):
        for off in (COL_SQ, COL_SK, COL_SV):
            cb = off // SWA_W + gi
            in_specs.append(pl.BlockSpec((1, tm, SWA_W), lambda b, i, cb=cb: (b, i, cb)))
            out_specs.append(pl.BlockSpec((1, dil, tm // dil, SWA_W), lambda b, i: (b, 0, i, 0)))
            out_shape.append(jax.ShapeDtypeStruct((bn, dil, t // dil, SWA_W), F32))
    tbl = pl.BlockSpec((tm, SWA_DH), lambda b, i: (i, 0))
    outs = pl.pallas_call(
        functools.partial(_rope_split_kernel, dils=dils, tm=tm),
        grid=(bn, t // tm),
        in_specs=in_specs + [tbl, tbl],
        out_specs=out_specs,
        out_shape=out_shape,
        scratch_shapes=[pltpu.VMEM((SWA_HEADS, tm, SWA_DH), F32)],
        compiler_params=_cparams("parallel", "parallel"),
        name="rope_split",
    )(*([h3] * (3 * N_SWA)), cosf, sinf)
    return [tuple(outs[3 * gi:3 * gi + 3]) for gi in range(N_SWA)]


def _rope_tables(pos):
    half = ROT_DIM // 2
    inv = ROPE_THETA ** (-jnp.arange(half, dtype=F32) * 2.0 / ROT_DIM)
    ang = pos.astype(F32)[:, None] * inv[None, :]
    cos, sin = jnp.cos(ang), jnp.sin(ang)
    n = pos.shape[0]
    cosf = jnp.concatenate([cos, cos, jnp.ones((n, SWA_DH - ROT_DIM), F32)], axis=1)
    sinf = jnp.concatenate([-sin, sin, jnp.zeros((n, SWA_DH - ROT_DIM), F32)], axis=1)
    return cosf, sinf


def _gdn_kernel(alog_ref, dtb_ref, q_ref, k_ref, v_ref, z_ref, ab_ref, cwq_ref, cwk_ref, cwv_ref,
                tq_ref, tk_ref, tv_ref, s0_ref, nw_ref, o_ref, sout_ref,
                s_scr, tailq, tailk, tailv, *, chunk, t_total, hps):
    c = pl.program_id(2)

    @pl.when(c == 0)
    def _():
        s_scr[...] = s0_ref[0]
        tailq[...] = tq_ref[0]
        tailk[...] = tk_ref[0]
        tailv[...] = tv_ref[0]

    C = chunk
    heads = range(hps)
    sls = [slice(hh * LANE, (hh + 1) * LANE) for hh in heads]
    hidx = [pl.program_id(1) * hps + hh for hh in heads]
    row8 = lax.broadcasted_iota(jnp.int32, (SUBLANE, LANE), 0)

    def conv(src_ref, tail_ref, w_ref, sl):
        raw = src_ref[0, :, sl]
        w = w_ref[:, sl]
        t8 = tail_ref[:, sl]
        y = raw * w[CONV_W - 1:CONV_W, :]
        for s in range(1, CONV_W):
            rolled = pltpu.roll(raw, s, axis=0)
            top = jnp.where(row8 < s, pltpu.roll(t8, s, axis=0), rolled[0:SUBLANE])
            sh = top if C == SUBLANE else jnp.concatenate([top, rolled[SUBLANE:]], axis=0)
            y = y + sh * w[CONV_W - 1 - s:CONV_W - s, :]
        tail_ref[:, sl] = raw[C - SUBLANE:C]
        return y * _sigmoid(y)

    def l2n(x):
        return x * lax.rsqrt(jnp.sum(x * x, axis=-1, keepdims=True) + EPS)

    q = [l2n(conv(q_ref, tailq, cwq_ref, sl)) * (GDN_DK ** -0.5) for sl in sls]
    k = [l2n(conv(k_ref, tailk, cwk_ref, sl)) for sl in sls]
    v = [conv(v_ref, tailv, cwv_ref, sl) for sl in sls]

    lane = lax.broadcasted_iota(jnp.int32, (C, LANE), 1)
    rowc = lax.broadcasted_iota(jnp.int32, (C, 1), 0)
    valid = (c * C + rowc) < t_total
    ab = ab_ref[0]
    beta, g = [], []
    for h in hidx:
        a = jnp.sum(jnp.where(lane == h, ab, 0.0), axis=-1, keepdims=True)
        b = jnp.sum(jnp.where(lane == h + GDN_HEADS, ab, 0.0), axis=-1, keepdims=True)
        sp_in = a + dtb_ref[h]
        softplus = jnp.maximum(sp_in, 0.0) + jnp.log(1.0 + jnp.exp(-jnp.abs(sp_in)))
        beta.append(jnp.where(valid, _sigmoid(b), 0.0))
        g.append(jnp.where(valid, -jnp.exp(jnp.full((C, 1), alog_ref[h], F32)) * softplus, 0.0))

    ri = lax.broadcasted_iota(jnp.int32, (C, C), 0)
    ci = lax.broadcasted_iota(jnp.int32, (C, C), 1)
    causal = ri >= ci
    strict = ri > ci
    tril = causal.astype(F32)
    e0 = (lane == 0).astype(F32)
    eye = (ri == ci).astype(F32)
    gc_b = [_dot_sel(tril, jnp.broadcast_to(gh, (C, LANE))) for gh in g]
    gc_row = [_dot_sel(e0, gch, (((1,), (1,)), ((), ()))) for gch in gc_b]
    decay = [jnp.where(causal, jnp.exp(jnp.where(causal, gch[:, :C] - grh, 0.0)), 0.0)
             for gch, grh in zip(gc_b, gc_row)]
    kb = [kh * bh for kh, bh in zip(k, beta)]
    lmat = [jnp.where(strict, _dot3_nt(kbh, kh) * dh, 0.0) for kbh, kh, dh in zip(kb, k, decay)]
    tinv = [eye] * hps
    for lg in range(int(math.log2(C))):
        low_left = jnp.logical_and((ri >> (lg + 1)) == (ci >> (lg + 1)),
                                   jnp.logical_and(((ri >> lg) & 1) == 1, ((ci >> lg) & 1) == 0))
        ct = [_dot3(jnp.where(low_left, lh, 0.0), th) for lh, th in zip(lmat, tinv)]
        tinv = [th - _dot3(th, cth) for th, cth in zip(tinv, ct)]
    eg = [jnp.exp(gch) for gch in gc_b]
    value = [_dot3(th, vh * bh) for th, vh, bh in zip(tinv, v, beta)]
    kcd = [_dot3(th, kbh * egh) for th, kbh, egh in zip(tinv, kb, eg)]
    attn = [jnp.where(causal, _dot_nt(qh.astype(BF16), kh.astype(BF16)) * dh, 0.0) for qh, kh, dh in zip(q, k, decay)]
    glast = [gch[C - 1:C, :] for gch in gc_b]
    kdec = [kh * jnp.exp(glh - gch) for kh, glh, gch in zip(k, glast, gc_b)]

    s = [s_scr[hh] for hh in heads]
    v_new = [vh - _dot3(kh, sh) for vh, kh, sh in zip(value, kcd, s)]
    bdot = lambda a, b: jnp.dot(a.astype(BF16), b.astype(BF16), preferred_element_type=F32)
    o = [bdot(qh * egh, sh) + bdot(ah, vnh) for qh, egh, sh, ah, vnh in zip(q, eg, s, attn, v_new)]
    for hh in heads:
        s_scr[hh] = s[hh] * jnp.exp(glast[hh]) + _dot3_tn(kdec[hh], v_new[hh])

    nw = nw_ref[...]
    for hh, sl in zip(heads, sls):
        z = z_ref[0, :, sl]
        on = o[hh] * lax.rsqrt(jnp.mean(o[hh] * o[hh], axis=-1, keepdims=True) + EPS) * nw
        o_ref[0, :, sl] = on * (z * _sigmoid(z))

    @pl.when(c == pl.num_programs(2) - 1)
    def _():
        sout_ref[0] = s_scr[...]


GDN_HEADS_PER_STEP = 8


def _gdn(h_main3, ab3, conv_w, tail8, s0, a_log, dt_bias, gdn_norm, chunk, t_total):
    bn, tp, _ = h_main3.shape
    nc = tp // chunk
    hps = GDN_HEADS_PER_STEP
    w = hps * LANE
    hq, hk, hv, hz = COL_Q // w, COL_K // w, COL_V // w, COL_Z // w
    cq, ck, cv = 0, GDN_QK // w, 2 * GDN_QK // w
    col = lambda off: pl.BlockSpec((1, chunk, w), lambda b, h, c: (b, c, off + h))
    cw = lambda off: pl.BlockSpec((CONV_W, w), lambda b, h, c: (0, off + h))
    tl = lambda off: pl.BlockSpec((1, SUBLANE, w), lambda b, h, c: (b, 0, off + h))
    smem = pl.BlockSpec(memory_space=pltpu.SMEM)
    kern = functools.partial(_gdn_kernel, chunk=chunk, t_total=t_total, hps=hps)
    return pl.pallas_call(
        kern,
        grid=(bn, GDN_HEADS // hps, nc),
        in_specs=[smem, smem, col(hq), col(hk), col(hv), col(hz),
                  pl.BlockSpec((1, chunk, LANE), lambda b, h, c: (b, c, 0)),
                  cw(cq), cw(ck), cw(cv), tl(cq), tl(ck), tl(cv),
                  pl.BlockSpec((1, hps, GDN_DK, GDN_DV), lambda b, h, c: (b, h, 0, 0)),
                  pl.BlockSpec((1, GDN_DV), lambda b, h, c: (0, 0))],
        out_specs=[pl.BlockSpec((1, chunk, w), lambda b, h, c: (b, c, h)),
                   pl.BlockSpec((1, hps, GDN_DK, GDN_DV), lambda b, h, c: (b, h, 0, 0))],
        out_shape=[jax.ShapeDtypeStruct((bn, tp, GDN_VW), F32),
                   jax.ShapeDtypeStruct((bn, GDN_HEADS, GDN_DK, GDN_DV), F32)],
        scratch_shapes=[pltpu.VMEM((hps, GDN_DK, GDN_DV), F32)] + [pltpu.VMEM((SUBLANE, w), F32)] * 3,
        compiler_params=_cparams("parallel", "parallel", "arbitrary"),
        name="gdn",
    )(a_log, dt_bias, h_main3, h_main3, h_main3, h_main3, ab3, conv_w, conv_w, conv_w,
      tail8, tail8, tail8, s0, gdn_norm.reshape(1, GDN_DV))


SWA_WIN_BLK = 128


def _swa_kernel(q_ref, kp_ref, kc_ref, vp_ref, vc_ref, o_ref, lse_ref, *, nsub):
    qi = pl.program_id(2)
    n = SWA_WIN_BLK
    ri = lax.broadcasted_iota(jnp.int32, (n, n), 0)
    ci = lax.broadcasted_iota(jnp.int32, (n, n), 1)
    mask_cur = ci <= ri
    scale = SWA_DH ** -0.5
    for sb in range(nsub):
        rows = slice(sb * n, (sb + 1) * n)
        prows = slice((sb - 1) * n, sb * n)
        mask_prev = jnp.logical_and(ci >= ri, qi > 0) if sb == 0 else ci >= ri
        for hh in range(SWA_HEADS):
            sl = slice(hh * SWA_DH, (hh + 1) * SWA_DH)
            kp = kp_ref[0, 0, :, sl] if sb == 0 else kc_ref[0, 0, prows, sl]
            vp = vp_ref[0, 0, :, sl] if sb == 0 else vc_ref[0, 0, prows, sl]
            q = q_ref[0, 0, rows, sl].astype(BF16)
            sp = jnp.where(mask_prev, _dot_nt(q, kp.astype(BF16)) * scale, NEG_BIG)
            sc = jnp.where(mask_cur, _dot_nt(q, kc_ref[0, 0, rows, sl].astype(BF16)) * scale, NEG_BIG)
            m = jnp.maximum(jnp.max(sp, axis=-1, keepdims=True), jnp.max(sc, axis=-1, keepdims=True))
            pp = jnp.exp(sp - m)
            pc = jnp.exp(sc - m)
            ssum = jnp.sum(pp, axis=-1, keepdims=True) + jnp.sum(pc, axis=-1, keepdims=True)
            acc = (jnp.dot(pp.astype(BF16), vp.astype(BF16), preferred_element_type=F32)
                   + jnp.dot(pc.astype(BF16), vc_ref[0, 0, rows, sl].astype(BF16), preferred_element_type=F32))
            o_ref[0, 0, rows, sl] = acc / ssum
            lse_ref[0, 0, rows, sl] = jnp.broadcast_to(m + jnp.log(ssum), (n, SWA_DH))


def _swa_prompt_group(q, k, v, gi, nsub):
    bn, dil, tl, _ = q.shape
    qblk = nsub * SWA_WIN_BLK
    cur = lambda b, r, i: (b, r, i, 0)
    prev = lambda b, r, i: (b, r, jnp.maximum(i * nsub - 1, 0), 0)
    blk = (1, 1, qblk, SWA_W)
    pblk = (1, 1, SWA_WIN_BLK, SWA_W)
    return pl.pallas_call(
        functools.partial(_swa_kernel, nsub=nsub),
        grid=(bn, dil, tl // qblk),
        in_specs=[pl.BlockSpec(blk, cur), pl.BlockSpec(pblk, prev), pl.BlockSpec(blk, cur),
                  pl.BlockSpec(pblk, prev), pl.BlockSpec(blk, cur)],
        out_specs=[pl.BlockSpec(blk, cur)] * 2,
        out_shape=[jax.ShapeDtypeStruct(q.shape, F32)] * 2,
        compiler_params=_cparams("parallel", "parallel", "arbitrary"),
        name=f"swa_prompt_g{gi}",
    )(q, k, k, v, v)


def _swa_sample_kernel(q_ref, kn_ref, vn_ref, c0_ref, c1_ref, c2_ref, *out_refs):
    scale = SWA_DH ** -0.5
    for gi, c_ref in enumerate((c0_ref, c1_ref, c2_ref)):
        o_ref, lse_ref = out_refs[2 * gi], out_refs[2 * gi + 1]
        q = q_ref[0, gi]
        kn = kn_ref[0, gi]
        vn = vn_ref[0, gi]
        kc = c_ref[0, :, 0, 0]
        vc = c_ref[0, :, 0, 1]
        s = jnp.sum(kc * q[None], axis=-1, keepdims=True) * scale
        sn = jnp.sum(kn * q, axis=-1, keepdims=True) * scale
        m = jnp.maximum(jnp.max(s, axis=0), sn)
        p = jnp.exp(s - m[None])
        pn = jnp.exp(sn - m)
        den = jnp.sum(p, axis=0) + pn
        o_ref[0] = (jnp.sum(p * vc, axis=0) + pn * vn) / den
        lse_ref[0] = jnp.broadcast_to(m + jnp.log(den), (SWA_HEADS, SWA_DH))


def _swa_sample(q_r, k_r, v_new, caches):
    bn = q_r.shape[0]
    nkeys = SWA_GROUPS[0][0] // SWA_GROUPS[0][1]
    views = []
    for (win, dil), cache in zip(SWA_GROUPS, caches):
        assert cache.shape[1] == win and win // dil == nkeys
        views.append(cache.reshape(bn, nkeys, dil, 2, SWA_HEADS, SWA_DH))
    heads = lambda a: a.reshape(bn, N_SWA, SWA_HEADS, SWA_DH)
    row = pl.BlockSpec((1, N_SWA, SWA_HEADS, SWA_DH), lambda b: (b, 0, 0, 0))
    cspec = pl.BlockSpec((1, nkeys, 1, 2, SWA_HEADS, SWA_DH), lambda b: (b, 0, 0, 0, 0, 0))
    ospec = pl.BlockSpec((1, SWA_HEADS, SWA_DH), lambda b: (b, 0, 0))
    outs = pl.pallas_call(
        _swa_sample_kernel,
        grid=(bn,),
        in_specs=[row, row, row, cspec, cspec, cspec],
        out_specs=[ospec] * (2 * N_SWA),
        out_shape=[jax.ShapeDtypeStruct((bn, SWA_HEADS, SWA_DH), F32)] * (2 * N_SWA),
        compiler_params=_cparams("parallel"),
        name="swa_sample",
    )(heads(q_r), heads(k_r), heads(v_new), *views)
    return [(outs[2 * g].reshape(bn, SWA_W), outs[2 * g + 1].reshape(bn, SWA_W)) for g in range(N_SWA)]


def _mix_kernel(oa_ref, o0_ref, l0_ref, o1_ref, l1_ref, o2_ref, l2_ref, ga_ref, gb_ref, x_ref,
                wa_ref, wb_ref, wo_ref, nw_ref, x1_ref, xn_ref, slab, *, dils, tm):
    def token_major(ref, dil):
        if dil == 1:
            return ref[0, 0]
        cols = []
        for hh in range(SWA_HEADS):
            for r in range(dil):
                slab[hh, pl.ds(r, tm // dil, stride=dil), :] = ref[0, r, :, hh * SWA_DH:(hh + 1) * SWA_DH]
            cols.append(slab[hh])
        return jnp.concatenate(cols, axis=1)

    os = [token_major(r, d) for r, d in zip((o0_ref, o1_ref, o2_ref), dils)]
    ls = [token_major(r, d) for r, d in zip((l0_ref, l1_ref, l2_ref), dils)]
    lmax = jnp.maximum(jnp.maximum(ls[0], ls[1]), ls[2])
    es = [jnp.exp(l - lmax) for l in ls]
    ob = (es[0] * os[0] + es[1] * os[1] + es[2] * os[2]) / (es[0] + es[1] + es[2])
    pa = jnp.dot(oa_ref[0].astype(BF16), wa_ref[...], preferred_element_type=F32)
    pb = jnp.dot(ob.astype(BF16), wb_ref[...], preferred_element_type=F32)
    merged = _sigmoid(ga_ref[0]) * pa + _sigmoid(gb_ref[0]) * pb
    x1 = x_ref[0] + jnp.dot(merged.astype(BF16), wo_ref[...], preferred_element_type=F32)
    x1_ref[0] = x1
    xn_ref[0] = x1 * lax.rsqrt(jnp.mean(x1 * x1, axis=-1, keepdims=True) + EPS) * nw_ref[...]


def _mix(o_a3, parts, dils, h3, x3, wa, wb, wo, norm_ffn, tm):
    bn, t, _ = x3.shape
    rowblk = lambda w, cb=0: pl.BlockSpec((1, tm, w), lambda b, i: (b, i, cb))
    full = lambda a: pl.BlockSpec(a.shape, lambda b, i: (0, 0))
    nw = norm_ffn.reshape(1, D_MODEL)
    part_specs, flat = [], []
    for (o, lse), dil in zip(parts, dils):
        spec = pl.BlockSpec((1, dil, tm // dil, SWA_W), lambda b, i: (b, 0, i, 0))
        part_specs += [spec, spec]
        flat += [o, lse]
    return pl.pallas_call(
        functools.partial(_mix_kernel, dils=dils, tm=tm),
        grid=(bn, t // tm),
        in_specs=[rowblk(GDN_VW)] + part_specs
                 + [rowblk(D_MODEL, COL_GA // D_MODEL), rowblk(D_MODEL, COL_GB // D_MODEL),
                    rowblk(D_MODEL), full(wa), full(wb), full(wo), full(nw)],
        out_specs=[rowblk(D_MODEL), rowblk(D_MODEL)],
        out_shape=[jax.ShapeDtypeStruct((bn, t, D_MODEL), F32)] * 2,
        scratch_shapes=[pltpu.VMEM((SWA_HEADS, tm, SWA_DH), F32)],
        compiler_params=_cparams("parallel", "parallel"),
        name="mix_out",
    )(o_a3, *flat, h3, h3, x3, wa, wb, wo, nw)


def _topk_rows(vals, k, payload=None):
    n, t = vals.shape
    idx = lax.broadcasted_iota(jnp.int32, (n, t), 0).astype(F32)
    out_v, out_i = [], []
    for _ in range(k):
        m = jnp.max(vals, axis=0, keepdims=True)
        am = jnp.min(jnp.where(vals == m, idx, float(n)), axis=0, keepdims=True)
        hit = idx == am
        out_v.append(m)
        out_i.append(am if payload is None else jnp.sum(jnp.where(hit, payload, 0.0), axis=0, keepdims=True))
        vals = jnp.where(hit, -jnp.inf, vals)
    return jnp.concatenate(out_v, axis=0), jnp.concatenate(out_i, axis=0)


def _route_kernel(q_ref, keys_ref, eid_ref, gate_ref):
    half = PEER_DKEY // 2
    eids, gates = [], []
    for h in range(PEER_HEADS):
        tops = []
        for p in range(2):
            qs = q_ref[:, (2 * h + p) * half:(2 * h + p + 1) * half].astype(BF16)
            st = _dot_nt(keys_ref[h, p].astype(BF16), qs)
            tops.append(_topk_rows(st, PEER_TOPK))
        (v1, i1), (v2, i2) = tops
        sub = SUBLANE
        assert PEER_TOPK == 2 * sub
        rows = [(slice(0, 1), slice(0, PEER_TOPK))]
        rows += [(slice(i, i + 1), slice(0, sub)) for i in range(1, sub)]
        rows += [(slice(sub, PEER_TOPK), slice(0, 1))]
        cand = jnp.concatenate([v1[a] + v2[b] for a, b in rows], axis=0)
        ecand = jnp.concatenate([i1[a] * float(PEER_NKEYS) + i2[b] for a, b in rows], axis=0)
        sc, e = _topk_rows(cand, PEER_TOPK, payload=ecand)
        ex = jnp.exp(sc - sc[0:1])
        gates.append(ex / jnp.sum(ex, axis=0, keepdims=True))
        eids.append(e)
    eid_ref[...] = jnp.concatenate(eids, axis=0).astype(jnp.int32).T
    gate_ref[...] = jnp.concatenate(gates, axis=0)


def _route(qp, sub_keys, tt):
    m = qp.shape[0]
    return pl.pallas_call(
        _route_kernel,
        grid=(m // tt,),
        in_specs=[pl.BlockSpec((tt, PEER_HEADS * PEER_DKEY), lambda i: (i, 0)),
                  pl.BlockSpec(sub_keys.shape, lambda i: (0, 0, 0, 0))],
        out_specs=[pl.BlockSpec((tt, PEER_SEL), lambda i: (i, 0)), pl.BlockSpec((PEER_SEL, tt), lambda i: (0, i))],
        out_shape=[jax.ShapeDtypeStruct((m, PEER_SEL), jnp.int32), jax.ShapeDtypeStruct((PEER_SEL, m), F32)],
        compiler_params=_cparams("parallel"),
        name="peer_route",
    )(qp, sub_keys)


PEER_TOKENS_PER_STEP = 256
PEER_NBUF = 8


PEER_CHUNKS = D_MODEL // LANE


def _pack_kernel(d_ref, u_ref, o_ref):
    hi = lax.bitcast_convert_type(d_ref[...].astype(BF16).astype(F32), jnp.uint32)
    lo = lax.bitcast_convert_type(u_ref[...].astype(BF16).astype(F32), jnp.uint32) >> 16
    word = hi | lo
    for c in range(PEER_CHUNKS):
        o_ref[:, c, :] = word[:, c * LANE:(c + 1) * LANE]


def _pack_peer_table(expert_down, expert_up, tr=256):
    e, d = expert_down.shape
    return pl.pallas_call(
        _pack_kernel,
        grid=(e // tr,),
        in_specs=[pl.BlockSpec((tr, d), lambda i: (i, 0))] * 2,
        out_specs=pl.BlockSpec((tr, PEER_CHUNKS, LANE), lambda i: (i, 0, 0)),
        out_shape=jax.ShapeDtypeStruct((e, PEER_CHUNKS, LANE), jnp.uint32),
        compiler_params=_cparams("parallel"),
        name="peer_pack",
    )(expert_down, expert_up)


def _peer_kernel(eid_ref, gate_ref, xn_ref, x1_ref, nw_ref, tbl_hbm, y_ref, *scratch, n_tok):
    nbuf = PEER_NBUF
    tbufs, (sem, xrow, yrow) = scratch[:nbuf], scratch[nbuf:]

    def issue(t, slot, rows):
        for j in rows:
            pltpu.make_async_copy(tbl_hbm.at[eid_ref[t, j]], tbufs[slot].at[:, j, :], sem.at[slot]).start(priority=j % 2)

    def wait(slot):
        pltpu.make_async_copy(tbufs[slot], tbufs[slot], sem.at[slot]).wait()

    lane = lax.broadcasted_iota(jnp.int32, (PEER_SEL, LANE), 1)
    hi_mask = jnp.uint32(0xFFFF0000)
    per_chunk = 3
    mid_rows = range(2 * PEER_CHUNKS * per_chunk, PEER_SEL)

    def compute(t, slot, t_ahead):
        def issue_rows(rows):
            if t_ahead is not None:
                issue(t_ahead, (slot - 1) % nbuf, rows)

        xrow[...] = xn_ref[pl.ds(t, 1), :]
        acc = jnp.zeros((PEER_SEL, LANE), F32)
        for c in range(PEER_CHUNKS):
            down = lax.bitcast_convert_type(tbufs[slot][c] & hi_mask, F32)
            acc = acc + down * xrow[:, c * LANE:(c + 1) * LANE]
            issue_rows(range(c * per_chunk, (c + 1) * per_chunk))
        act = jnp.sum(acc, axis=-1, keepdims=True)
        gate = jnp.sum(jnp.where(lane == t % LANE, gate_ref[t // LANE], 0.0), axis=-1, keepdims=True)
        issue_rows(mid_rows)
        gelu = 0.5 * act * (1.0 + lax.erf(act * (2.0 ** -0.5)))
        w = gate * gelu
        for c in range(PEER_CHUNKS):
            up = lax.bitcast_convert_type(tbufs[slot][c] << 16, F32)
            yrow[:, c * LANE:(c + 1) * LANE] = jnp.sum(up * w, axis=0, keepdims=True)
            issue_rows(range((PEER_CHUNKS + c) * per_chunk, (PEER_CHUNKS + c + 1) * per_chunk))
        y_ref[pl.ds(t, 1), :] = x1_ref[pl.ds(t, 1), :] + yrow[...]

    assert n_tok % nbuf == 0
    for k in range(nbuf - 1):
        issue(k, k, range(PEER_SEL))

    def group(p, carry):
        for k in range(nbuf):
            t = nbuf * p + k
            wait(k)
            compute(t, k, t + nbuf - 1)
        return carry

    lax.fori_loop(0, n_tok // nbuf - 1, group, 0)
    for k in range(nbuf):
        t = n_tok - nbuf + k
        wait(k)
        compute(t, k, t + nbuf - 1 if k == 0 else None)

    x2 = y_ref[...]
    y_ref[...] = x2 * lax.rsqrt(jnp.mean(x2 * x2, axis=-1, keepdims=True) + EPS) * nw_ref[...]


def _peer(eid, gates_t, xn2, x1, norm_final, table, n_tok):
    m = xn2.shape[0]
    kern = functools.partial(_peer_kernel, n_tok=n_tok)
    tiles = gates_t.shape[1] // LANE
    gates3 = gates_t.reshape(PEER_SEL, tiles, LANE).transpose(1, 0, 2)
    gblk = max(n_tok // LANE, 1)
    return pl.pallas_call(
        kern,
        grid=(m // n_tok,),
        in_specs=[pl.BlockSpec((gblk * LANE, PEER_SEL), lambda i: (i, 0), memory_space=pltpu.SMEM),
                  pl.BlockSpec((gblk, PEER_SEL, LANE), lambda i: (i, 0, 0)),
                  pl.BlockSpec((n_tok, D_MODEL), lambda i: (i, 0)),
                  pl.BlockSpec((n_tok, D_MODEL), lambda i: (i, 0)),
                  pl.BlockSpec((1, D_MODEL), lambda i: (0, 0)),
                  pl.BlockSpec(memory_space=pl.ANY)],
        out_specs=pl.BlockSpec((n_tok, D_MODEL), lambda i: (i, 0)),
        out_shape=jax.ShapeDtypeStruct((m, D_MODEL), F32),
        scratch_shapes=[pltpu.VMEM((PEER_CHUNKS, PEER_SEL, LANE), jnp.uint32)] * PEER_NBUF
                       + [pltpu.SemaphoreType.DMA((PEER_NBUF,)),
                          pltpu.VMEM((1, D_MODEL), F32), pltpu.VMEM((1, D_MODEL), F32)],
        compiler_params=_cparams("arbitrary"),
        name="peer_experts",
    )(eid, gates3, xn2, x1, norm_final.reshape(1, D_MODEL), table)


def _pick(m, candidates):
    for c in candidates:
        if m % c == 0:
            return c
    return m


def _layer(x, pos, tail8, s0, caches, w, chunk):
    bn, t, _ = x.shape
    m = bn * t
    x2d = x.reshape(m, D_MODEL)
    tm_big = _pick(m, (1024, 512, 256, 128, 64, 32))

    xn = _rmsnorm(x2d, w["norm_mix"], _pick(m, (512, 256, 128, 64, 32)), BF16)
    h_main = _matmul(xn, w["w_main"], tm_big, 1280, "proj_in")
    h_ab = _matmul(xn, w["w_ab"], tm_big, LANE, "proj_ab")

    tp = -(-t // chunk) * chunk
    h3 = h_main.reshape(bn, t, N_MAIN)
    ab3 = h_ab.reshape(bn, t, LANE)
    h3p = h3 if tp == t else jnp.pad(h3, ((0, 0), (0, tp - t), (0, 0)))
    ab3p = ab3 if tp == t else jnp.pad(ab3, ((0, 0), (0, tp - t), (0, 0)))
    o_a, s_new = _gdn(h3p, ab3p, w["conv_w"], tail8, s0, w["a_log"], w["dt_bias"], w["gdn_norm"], chunk, t)
    o_a = o_a[:, :t]

    cosf, sinf = _rope_tables(pos)
    kv_new = []
    if caches is None:
        qkv = _rope_split(h3, cosf, sinf, 256)
        parts = [_swa_prompt_group(q, k, v, gi, 2 if q.shape[2] % (2 * SWA_WIN_BLK) == 0 else 1)
                 for gi, (q, k, v) in enumerate(qkv)]
        dils = tuple(d for _, d in SWA_GROUPS)
        for (win, dil), (_, k, v) in zip(SWA_GROUPS, qkv):
            keep = min(win, t)
            tok = lambda a: (a[:, :, (t - keep) // dil:].transpose(0, 2, 1, 3)
                             .reshape(bn, keep, SWA_HEADS, SWA_DH))
            kv_new.append(jnp.stack([tok(k), tok(v)], axis=2))
        mix_shape = (bn, t)
    else:
        cosf, sinf = jnp.tile(cosf, (bn, 1)), jnp.tile(sinf, (bn, 1))
        q_r, k_r = _rope(h_main, cosf, sinf, _pick(m, (512, 256, 128, 64, 32)))
        v_new = h_main[:, COL_SV:COL_SV + SWA_QKV]
        parts = [(o.reshape(1, 1, m, SWA_W), lse.reshape(1, 1, m, SWA_W))
                 for o, lse in _swa_sample(q_r, k_r, v_new, caches)]
        dils = (1,) * N_SWA
        for gi in range(N_SWA):
            kk = k_r[:, gi * SWA_W:(gi + 1) * SWA_W].reshape(bn, t, SWA_HEADS, SWA_DH)
            vv = v_new[:, gi * SWA_W:(gi + 1) * SWA_W].reshape(bn, t, SWA_HEADS, SWA_DH)
            kv_new.append(jnp.stack([kk, vv], axis=2))
        mix_shape = (1, m)

    mb, mt = mix_shape
    x1, xn2 = _mix(o_a.reshape(mb, mt, GDN_VW), parts, dils, h_main.reshape(mb, mt, N_MAIN),
                   x.reshape(mb, mt, D_MODEL), w["w_br_a"], w["w_br_b"], w["w_out"], w["norm_ffn"],
                   _pick(mt, (256, 128, 64, 32)))
    x1, xn2 = x1.reshape(m, D_MODEL), xn2.reshape(m, D_MODEL)

    mp = -(-m // LANE) * LANE
    xq = xn2 if mp == m else jnp.pad(xn2, ((0, mp - m), (0, 0)))
    qp = _matmul(xq, w["w_query"], _pick(mp, (1024, 512, 256, 128)), 512, "peer_query")
    eid, gates_t = _route(qp, w["sub_keys"], LANE)
    y = _peer(eid, gates_t, xn2, x1, w["norm_final"], w["peer_table"], min(m, PEER_TOKENS_PER_STEP))
    return y.reshape(bn, t, D_MODEL), s_new, h3, kv_new


def kernel(x_prompt, x_sample, state_gdn, state_conv, cache_kv_w128, cache_kv_w512, cache_kv_w2048,
           norm_mix, w_in, conv_w, a_log, dt_bias, gdn_norm, w_br_a, w_br_b, w_out, norm_ffn,
           w_query, sub_keys, expert_down, expert_up, norm_final):
    w_main = jnp.concatenate([w_in[:, IN_GA:], w_in[:, :IN_AB], w_in[:, IN_SQ:IN_GA]], axis=1).astype(BF16)
    w_ab = jnp.pad(w_in[:, IN_AB:IN_SQ], ((0, 0), (0, LANE - 2 * GDN_HEADS))).astype(BF16)
    w = dict(norm_mix=norm_mix, w_main=w_main, w_ab=w_ab, conv_w=conv_w, a_log=a_log, dt_bias=dt_bias,
             gdn_norm=gdn_norm, w_br_a=w_br_a.astype(BF16), w_br_b=w_br_b.astype(BF16), w_out=w_out.astype(BF16),
             norm_ffn=norm_ffn, w_query=w_query.astype(BF16), sub_keys=sub_keys,
             peer_table=_pack_peer_table(expert_down, expert_up), norm_final=norm_final)

    bp, tlen = x_prompt.shape[:2]
    bs, ts = x_sample.shape[:2]
    caches = (cache_kv_w128, cache_kv_w512, cache_kv_w2048)

    tail8 = jnp.concatenate([jnp.zeros((bs, SUBLANE - (CONV_W - 1), GDN_CONV_CH), F32), state_conv], axis=1)
    y_s, gdn_s, h3_s, kv_new = _layer(
        x_sample, PAST_LEN + jnp.arange(ts, dtype=jnp.int32), tail8, state_gdn, caches, w, SUBLANE)
    conv_s = jnp.concatenate([state_conv, h3_s[:, :, COL_Q:COL_Q + GDN_CONV_CH]], axis=1)[:, ts:]
    kv_s = [jnp.concatenate([cache, new.astype(cache.dtype)], axis=1)[:, ts:] for cache, new in zip(caches, kv_new)]

    y_p, gdn_p, h3_p, kv_p = _layer(
        x_prompt, jnp.arange(tlen, dtype=jnp.int32),
        jnp.zeros((bp, SUBLANE, GDN_CONV_CH), F32), jnp.zeros((bp, GDN_HEADS, GDN_DK, GDN_DV), F32),
        None, w, CHUNK)
    conv_p = h3_p[:, tlen - (CONV_W - 1):, COL_Q:COL_Q + GDN_CONV_CH]

    return (y_p, y_s, gdn_p.astype(x_prompt.dtype), conv_p, kv_p[0], kv_p[1], kv_p[2],
            gdn_s.astype(state_gdn.dtype), conv_s, kv_s[0], kv_s[1], kv_s[2])
```

```python
import functools
import math

import jax
import jax.numpy as jnp
import numpy as np
from jax import lax
from jax.experimental import pallas as pl
from jax.experimental.pallas import tpu as pltpu

F32 = jnp.float32
BF16 = jnp.bfloat16

LANE = 128
SUBLANE = 8
VMEM_LIMIT = 56 * 1024 * 1024

D_MODEL = 2048
PAST_LEN = 16384
EPS = 1e-6
GDN_HEADS = 8
GDN_DK = 128
GDN_DV = 128
GDN_QK = GDN_HEADS * GDN_DK
GDN_VW = GDN_HEADS * GDN_DV
GDN_CONV_CH = 2 * GDN_QK + GDN_VW
CONV_W = 4
CHUNK = 64
SWA_GROUPS = ((128, 1), (512, 4), (2048, 16))
N_SWA = 3
SWA_HEADS = 4
SWA_DH = 128
SWA_W = SWA_HEADS * SWA_DH
SWA_QKV = N_SWA * SWA_W
ROT_DIM = SWA_DH // 4
ROPE_THETA = 500000.0
PEER_HEADS = 8
PEER_NKEYS = 128
PEER_DKEY = 256
PEER_TOPK = 16
PEER_SEL = PEER_HEADS * PEER_TOPK

COL_GA = 0
COL_GB = D_MODEL
COL_Q = 2 * D_MODEL
COL_K = COL_Q + GDN_QK
COL_V = COL_K + GDN_QK
COL_Z = COL_V + GDN_VW
COL_SQ = COL_Z + GDN_VW
COL_SK = COL_SQ + SWA_QKV
COL_SV = COL_SK + SWA_QKV
N_MAIN = COL_SV + SWA_QKV
IN_AB = 2 * GDN_QK + 2 * GDN_VW
IN_SQ = IN_AB + 2 * GDN_HEADS
IN_GA = IN_SQ + 3 * SWA_QKV
NEG_BIG = -1e30


def _cparams(*sem):
    return pltpu.CompilerParams(dimension_semantics=sem, vmem_limit_bytes=VMEM_LIMIT)


def _sigmoid(x):
    return 1.0 / (1.0 + jnp.exp(-x))


def _dot_sel(sel, b, dims=(((1,), (0,)), ((), ()))):
    b1 = b.astype(BF16)
    r1 = b - b1.astype(F32)
    b2 = r1.astype(BF16)
    b3 = (r1 - b2.astype(F32)).astype(BF16)
    s = sel.astype(BF16)
    dot = lambda y: lax.dot_general(s, y, dims, preferred_element_type=F32)
    return dot(b1) + (dot(b2) + dot(b3))


def _split_bf16(a):
    hi = a.astype(BF16)
    return hi, (a - hi.astype(F32)).astype(BF16)


def _dot3(a, b, dims=(((1,), (0,)), ((), ()))):
    a_hi, a_lo = _split_bf16(a)
    b_hi, b_lo = _split_bf16(b)
    dot = lambda x, y: lax.dot_general(x, y, dims, preferred_element_type=F32)
    return dot(a_hi, b_hi) + (dot(a_hi, b_lo) + dot(a_lo, b_hi))


def _dot3_nt(a, b):
    return _dot3(a, b, (((1,), (1,)), ((), ())))


def _dot3_tn(a, b):
    return _dot3(a, b, (((0,), (0,)), ((), ())))


def _dot_nt(a, b):
    return lax.dot_general(a, b, (((1,), (1,)), ((), ())), preferred_element_type=F32)


def _rmsnorm_kernel(x_ref, w_ref, o_ref):
    x = x_ref[...]
    y = x * lax.rsqrt(jnp.mean(x * x, axis=-1, keepdims=True) + EPS) * w_ref[...]
    o_ref[...] = y.astype(o_ref.dtype)


def _rmsnorm(x, w, tm, out_dtype):
    m, d = x.shape
    return pl.pallas_call(
        _rmsnorm_kernel,
        grid=(m // tm,),
        in_specs=[pl.BlockSpec((tm, d), lambda i: (i, 0)), pl.BlockSpec((1, d), lambda i: (0, 0))],
        out_specs=pl.BlockSpec((tm, d), lambda i: (i, 0)),
        out_shape=jax.ShapeDtypeStruct((m, d), out_dtype),
        compiler_params=_cparams("parallel"),
        name="rmsnorm",
    )(x, w.reshape(1, d))


def _mm_kernel(x_ref, w_ref, o_ref):
    o_ref[...] = jnp.dot(x_ref[...].astype(BF16), w_ref[...], preferred_element_type=F32)


def _matmul(x, w, tm, tn, name):
    m, k = x.shape
    n = w.shape[1]
    return pl.pallas_call(
        _mm_kernel,
        grid=(n // tn, m // tm),
        in_specs=[pl.BlockSpec((tm, k), lambda j, i: (i, 0)), pl.BlockSpec((k, tn), lambda j, i: (0, j))],
        out_specs=pl.BlockSpec((tm, tn), lambda j, i: (i, j)),
        out_shape=jax.ShapeDtypeStruct((m, n), F32),
        compiler_params=_cparams("parallel", "parallel"),
        name=name,
    )(x, w)


def _rope_kernel(q_ref, k_ref, cos_ref, sin_ref, qo_ref, ko_ref):
    cosf = cos_ref[...]
    sinf = sin_ref[...]
    lane = lax.broadcasted_iota(jnp.int32, cosf.shape, 1)
    first = lane < ROT_DIM // 2
    for src, dst in ((q_ref, qo_ref), (k_ref, ko_ref)):
        for hh in range(SWA_HEADS):
            sl = slice(hh * SWA_DH, (hh + 1) * SWA_DH)
            x = src[:, sl]
            partner = jnp.where(first, pltpu.roll(x, SWA_DH - ROT_DIM // 2, axis=1), pltpu.roll(x, ROT_DIM // 2, axis=1))
            dst[:, sl] = x * cosf + partner * sinf


def _rope(h_main, cosf, sinf, tm):
    m = h_main.shape[0]
    nt = cosf.shape[0] // tm
    qb, kb = COL_SQ // SWA_W, COL_SK // SWA_W
    return pl.pallas_call(
        _rope_kernel,
        grid=(m // tm, N_SWA),
        in_specs=[
            pl.BlockSpec((tm, SWA_W), lambda i, g: (i, qb + g)),
            pl.BlockSpec((tm, SWA_W), lambda i, g: (i, kb + g)),
            pl.BlockSpec((tm, SWA_DH), lambda i, g: (i % nt, 0)),
            pl.BlockSpec((tm, SWA_DH), lambda i, g: (i % nt, 0)),
        ],
        out_specs=[pl.BlockSpec((tm, SWA_W), lambda i, g: (i, g)), pl.BlockSpec((tm, SWA_W), lambda i, g: (i, g))],
        out_shape=[jax.ShapeDtypeStruct((m, SWA_QKV), F32), jax.ShapeDtypeStruct((m, SWA_QKV), F32)],
        compiler_params=_cparams("parallel", "parallel"),
        name="rope",
    )(h_main, h_main, cosf, sinf)


def _rope_split_kernel(*refs, dils, tm):
    ins, cos_ref, sin_ref = refs[:3 * N_SWA], refs[3 * N_SWA], refs[3 * N_SWA + 1]
    outs, slab = refs[3 * N_SWA + 2:6 * N_SWA + 2], refs[6 * N_SWA + 2]
    cosf = cos_ref[...]
    sinf = sin_ref[...]
    first = lax.broadcasted_iota(jnp.int32, cosf.shape, 1) < ROT_DIM // 2
    for gi, dil in enumerate(dils):
        for kind in range(3):
            src, dst = ins[3 * gi + kind], outs[3 * gi + kind]
            for hh in range(SWA_HEADS):
                sl = slice(hh * SWA_DH, (hh + 1) * SWA_DH)
                x = src[0, :, sl]
                if kind < 2:
                    partner = jnp.where(first, pltpu.roll(x, SWA_DH - ROT_DIM // 2, axis=1),
                                        pltpu.roll(x, ROT_DIM // 2, axis=1))
                    x = x * cosf + partner * sinf
                if dil == 1:
                    dst[0, 0, :, sl] = x
                else:
                    slab[hh] = x
                    for r in range(dil):
                        dst[0, r, :, sl] = slab[hh, pl.ds(r, tm // dil, stride=dil), :]


def _rope_split(h3, cosf, sinf, tm):
    bn, t, _ = h3.shape
    dils = tuple(d for _, d in SWA_GROUPS)
    in_specs, out_specs, out_shape = [], [], []
    for gi, dil in enumerate(dils):
        for off in (COL_SQ, COL_SK, COL_SV):
            cb = off // SWA_W + gi
            in_specs.append(pl.BlockSpec((1, tm, SWA_W), lambda b, i, cb=cb: (b, i, cb)))
            out_specs.append(pl.BlockSpec((1, dil, tm // dil, SWA_W), lambda b, i: (b, 0, i, 0)))
            out_shape.append(jax.ShapeDtypeStruct((bn, dil, t // dil, SWA_W), F32))
    tbl = pl.BlockSpec((tm, SWA_DH), lambda b, i: (i, 0))
    outs = pl.pallas_call(
        functools.partial(_rope_split_kernel, dils=dils, tm=tm),
        grid=(bn, t // tm),
        in_specs=in_specs + [tbl, tbl],
        out_specs=out_specs,
        out_shape=out_shape,
        scratch_shapes=[pltpu.VMEM((SWA_HEADS, tm, SWA_DH), F32)],
        compiler_params=_cparams("parallel", "parallel"),
        name="rope_split",
    )(*([h3] * (3 * N_SWA)), cosf, sinf)
    return [tuple(outs[3 * gi:3 * gi + 3]) for gi in range(N_SWA)]


def _rope_tables(pos):
    half = ROT_DIM // 2
    inv = ROPE_THETA ** (-jnp.arange(half, dtype=F32) * 2.0 / ROT_DIM)
    ang = pos.astype(F32)[:, None] * inv[None, :]
    cos, sin = jnp.cos(ang), jnp.sin(ang)
    n = pos.shape[0]
    cosf = jnp.concatenate([cos, cos, jnp.ones((n, SWA_DH - ROT_DIM), F32)], axis=1)
    sinf = jnp.concatenate([-sin, sin, jnp.zeros((n, SWA_DH - ROT_DIM), F32)], axis=1)
    return cosf, sinf


def _gdn_kernel(alog_ref, dtb_ref, q_ref, k_ref, v_ref, z_ref, ab_ref, cwq_ref, cwk_ref, cwv_ref,
                tq_ref, tk_ref, tv_ref, s0_ref, nw_ref, o_ref, sout_ref,
                s_scr, tailq, tailk, tailv, *, chunk, t_total, hps):
    c = pl.program_id(2)

    @pl.when(c == 0)
    def _():
        s_scr[...] = s0_ref[0]
        tailq[...] = tq_ref[0]
        tailk[...] = tk_ref[0]
        tailv[...] = tv_ref[0]

    C = chunk
    heads = range(hps)
    sls = [slice(hh * LANE, (hh + 1) * LANE) for hh in heads]
    hidx = [pl.program_id(1) * hps + hh for hh in heads]
    row8 = lax.broadcasted_iota(jnp.int32, (SUBLANE, LANE), 0)

    def conv(src_ref, tail_ref, w_ref, sl):
        raw = src_ref[0, :, sl]
        w = w_ref[:, sl]
        t8 = tail_ref[:, sl]
        y = raw * w[CONV_W - 1:CONV_W, :]
        for s in range(1, CONV_W):
            rolled = pltpu.roll(raw, s, axis=0)
            top = jnp.where(row8 < s, pltpu.roll(t8, s, axis=0), rolled[0:SUBLANE])
            sh = top if C == SUBLANE else jnp.concatenate([top, rolled[SUBLANE:]], axis=0)
            y = y + sh * w[CONV_W - 1 - s:CONV_W - s, :]
        tail_ref[:, sl] = raw[C - SUBLANE:C]
        return y * _sigmoid(y)

    def l2n(x):
        return x * lax.rsqrt(jnp.sum(x * x, axis=-1, keepdims=True) + EPS)

    q = [l2n(conv(q_ref, tailq, cwq_ref, sl)) * (GDN_DK ** -0.5) for sl in sls]
    k = [l2n(conv(k_ref, tailk, cwk_ref, sl)) for sl in sls]
    v = [conv(v_ref, tailv, cwv_ref, sl) for sl in sls]

    lane = lax.broadcasted_iota(jnp.int32, (C, LANE), 1)
    rowc = lax.broadcasted_iota(jnp.int32, (C, 1), 0)
    valid = (c * C + rowc) < t_total
    ab = ab_ref[0]
    beta, g = [], []
    for h in hidx:
        a = jnp.sum(jnp.where(lane == h, ab, 0.0), axis=-1, keepdims=True)
        b = jnp.sum(jnp.where(lane == h + GDN_HEADS, ab, 0.0), axis=-1, keepdims=True)
        sp_in = a + dtb_ref[h]
        softplus = jnp.maximum(sp_in, 0.0) + jnp.log(1.0 + jnp.exp(-jnp.abs(sp_in)))
        beta.append(jnp.where(valid, _sigmoid(b), 0.0))
        g.append(jnp.where(valid, -jnp.exp(jnp.full((C, 1), alog_ref[h], F32)) * softplus, 0.0))

    ri = lax.broadcasted_iota(jnp.int32, (C, C), 0)
    ci = lax.broadcasted_iota(jnp.int32, (C, C), 1)
    causal = ri >= ci
    strict = ri > ci
    tril = causal.astype(F32)
    e0 = (lane == 0).astype(F32)
    eye = (ri == ci).astype(F32)
    gc_b = [_dot_sel(tril, jnp.broadcast_to(gh, (C, LANE))) for gh in g]
    gc_row = [_dot_sel(e0, gch, (((1,), (1,)), ((), ()))) for gch in gc_b]
    decay = [jnp.where(causal, jnp.exp(jnp.where(causal, gch[:, :C] - grh, 0.0)), 0.0)
             for gch, grh in zip(gc_b, gc_row)]
    kb = [kh * bh for kh, bh in zip(k, beta)]
    lmat = [jnp.where(strict, _dot3_nt(kbh, kh) * dh, 0.0) for kbh, kh, dh in zip(kb, k, decay)]
    def low_left(lg):
        return jnp.logical_and((ri >> (lg + 1)) == (ci >> (lg + 1)),
                               jnp.logical_and(((ri >> lg) & 1) == 1, ((ci >> lg) & 1) == 0))

    tinv = [eye - jnp.where(low_left(0), lh, 0.0) for lh in lmat]
    for lg in range(1, int(math.log2(C))):
        ct = [_dot3(jnp.where(low_left(lg), lh, 0.0), th) for lh, th in zip(lmat, tinv)]
        tinv = [th - _dot3(th, cth) for th, cth in zip(tinv, ct)]
    eg = [jnp.exp(gch) for gch in gc_b]
    value = [_dot3(th, vh * bh) for th, vh, bh in zip(tinv, v, beta)]
    kcd = [_dot3(th, kbh * egh) for th, kbh, egh in zip(tinv, kb, eg)]
    attn = [jnp.where(causal, _dot_nt(qh.astype(BF16), kh.astype(BF16)) * dh, 0.0) for qh, kh, dh in zip(q, k, decay)]
    glast = [gch[C - 1:C, :] for gch in gc_b]
    kdec = [kh * jnp.exp(glh - gch) for kh, glh, gch in zip(k, glast, gc_b)]

    s = [s_scr[hh] for hh in heads]
    v_new = [vh - _dot3(kh, sh) for vh, kh, sh in zip(value, kcd, s)]
    bdot = lambda a, b: jnp.dot(a.astype(BF16), b.astype(BF16), preferred_element_type=F32)
    o = [bdot(qh * egh, sh) + bdot(ah, vnh) for qh, egh, sh, ah, vnh in zip(q, eg, s, attn, v_new)]
    for hh in heads:
        s_scr[hh] = s[hh] * jnp.exp(glast[hh]) + _dot3_tn(kdec[hh], v_new[hh])

    nw = nw_ref[...]
    for hh, sl in zip(heads, sls):
        z = z_ref[0, :, sl]
        on = o[hh] * lax.rsqrt(jnp.mean(o[hh] * o[hh], axis=-1, keepdims=True) + EPS) * nw
        o_ref[0, :, sl] = on * (z * _sigmoid(z))

    @pl.when(c == pl.num_programs(2) - 1)
    def _():
        sout_ref[0] = s_scr[...]


GDN_HEADS_PER_STEP = 8


def _gdn(h_main3, ab3, conv_w, tail8, s0, a_log, dt_bias, gdn_norm, chunk, t_total):
    bn, tp, _ = h_main3.shape
    nc = tp // chunk
    hps = GDN_HEADS_PER_STEP
    w = hps * LANE
    hq, hk, hv, hz = COL_Q // w, COL_K // w, COL_V // w, COL_Z // w
    cq, ck, cv = 0, GDN_QK // w, 2 * GDN_QK // w
    col = lambda off: pl.BlockSpec((1, chunk, w), lambda b, h, c: (b, c, off + h))
    cw = lambda off: pl.BlockSpec((CONV_W, w), lambda b, h, c: (0, off + h))
    tl = lambda off: pl.BlockSpec((1, SUBLANE, w), lambda b, h, c: (b, 0, off + h))
    smem = pl.BlockSpec(memory_space=pltpu.SMEM)
    kern = functools.partial(_gdn_kernel, chunk=chunk, t_total=t_total, hps=hps)
    return pl.pallas_call(
        kern,
        grid=(bn, GDN_HEADS // hps, nc),
        in_specs=[smem, smem, col(hq), col(hk), col(hv), col(hz),
                  pl.BlockSpec((1, chunk, LANE), lambda b, h, c: (b, c, 0)),
                  cw(cq), cw(ck), cw(cv), tl(cq), tl(ck), tl(cv),
                  pl.BlockSpec((1, hps, GDN_DK, GDN_DV), lambda b, h, c: (b, h, 0, 0)),
                  pl.BlockSpec((1, GDN_DV), lambda b, h, c: (0, 0))],
        out_specs=[pl.BlockSpec((1, chunk, w), lambda b, h, c: (b, c, h)),
                   pl.BlockSpec((1, hps, GDN_DK, GDN_DV), lambda b, h, c: (b, h, 0, 0))],
        out_shape=[jax.ShapeDtypeStruct((bn, tp, GDN_VW), F32),
                   jax.ShapeDtypeStruct((bn, GDN_HEADS, GDN_DK, GDN_DV), F32)],
        scratch_shapes=[pltpu.VMEM((hps, GDN_DK, GDN_DV), F32)] + [pltpu.VMEM((SUBLANE, w), F32)] * 3,
        compiler_params=_cparams("parallel", "parallel", "arbitrary"),
        name="gdn",
    )(a_log, dt_bias, h_main3, h_main3, h_main3, h_main3, ab3, conv_w, conv_w, conv_w,
      tail8, tail8, tail8, s0, gdn_norm.reshape(1, GDN_DV))


SWA_WIN_BLK = 128


def _swa_kernel(q_ref, kp_ref, kc_ref, vp_ref, vc_ref, o_ref, lse_ref, *, nsub):
    qi = pl.program_id(2)
    n = SWA_WIN_BLK
    ri = lax.broadcasted_iota(jnp.int32, (n, n), 0)
    ci = lax.broadcasted_iota(jnp.int32, (n, n), 1)
    mask_cur = ci <= ri
    scale = SWA_DH ** -0.5
    for sb in range(nsub):
        rows = slice(sb * n, (sb + 1) * n)
        prows = slice((sb - 1) * n, sb * n)
        mask_prev = jnp.logical_and(ci >= ri, qi > 0) if sb == 0 else ci >= ri
        for hh in range(SWA_HEADS):
            sl = slice(hh * SWA_DH, (hh + 1) * SWA_DH)
            kp = kp_ref[0, 0, :, sl] if sb == 0 else kc_ref[0, 0, prows, sl]
            vp = vp_ref[0, 0, :, sl] if sb == 0 else vc_ref[0, 0, prows, sl]
            q = q_ref[0, 0, rows, sl].astype(BF16)
            sp = jnp.where(mask_prev, _dot_nt(q, kp.astype(BF16)) * scale, NEG_BIG)
            sc = jnp.where(mask_cur, _dot_nt(q, kc_ref[0, 0, rows, sl].astype(BF16)) * scale, NEG_BIG)
            m = jnp.maximum(jnp.max(sp, axis=-1, keepdims=True), jnp.max(sc, axis=-1, keepdims=True))
            pp = jnp.exp(sp - m)
            pc = jnp.exp(sc - m)
            ssum = jnp.sum(pp, axis=-1, keepdims=True) + jnp.sum(pc, axis=-1, keepdims=True)
            acc = (jnp.dot(pp.astype(BF16), vp.astype(BF16), preferred_element_type=F32)
                   + jnp.dot(pc.astype(BF16), vc_ref[0, 0, rows, sl].astype(BF16), preferred_element_type=F32))
            o_ref[0, 0, rows, sl] = acc / ssum
            lse_ref[0, 0, rows, sl] = jnp.broadcast_to(m + jnp.log(ssum), (n, SWA_DH))


def _swa_prompt_group(q, k, v, gi, nsub):
    bn, dil, tl, _ = q.shape
    qblk = nsub * SWA_WIN_BLK
    cur = lambda b, r, i: (b, r, i, 0)
    prev = lambda b, r, i: (b, r, jnp.maximum(i * nsub - 1, 0), 0)
    blk = (1, 1, qblk, SWA_W)
    pblk = (1, 1, SWA_WIN_BLK, SWA_W)
    return pl.pallas_call(
        functools.partial(_swa_kernel, nsub=nsub),
        grid=(bn, dil, tl // qblk),
        in_specs=[pl.BlockSpec(blk, cur), pl.BlockSpec(pblk, prev), pl.BlockSpec(blk, cur),
                  pl.BlockSpec(pblk, prev), pl.BlockSpec(blk, cur)],
        out_specs=[pl.BlockSpec(blk, cur)] * 2,
        out_shape=[jax.ShapeDtypeStruct(q.shape, F32)] * 2,
        compiler_params=_cparams("parallel", "parallel", "arbitrary"),
        name=f"swa_prompt_g{gi}",
    )(q, k, k, v, v)


def _swa_sample_kernel(q_ref, kn_ref, vn_ref, c0_ref, c1_ref, c2_ref, *out_refs):
    scale = SWA_DH ** -0.5
    for gi, c_ref in enumerate((c0_ref, c1_ref, c2_ref)):
        o_ref, lse_ref = out_refs[2 * gi], out_refs[2 * gi + 1]
        q = q_ref[0, gi]
        kn = kn_ref[0, gi]
        vn = vn_ref[0, gi]
        kc = c_ref[0, :, 0, 0]
        vc = c_ref[0, :, 0, 1]
        s = jnp.sum(kc * q[None], axis=-1, keepdims=True) * scale
        sn = jnp.sum(kn * q, axis=-1, keepdims=True) * scale
        m = jnp.maximum(jnp.max(s, axis=0), sn)
        p = jnp.exp(s - m[None])
        pn = jnp.exp(sn - m)
        den = jnp.sum(p, axis=0) + pn
        o_ref[0] = (jnp.sum(p * vc, axis=0) + pn * vn) / den
        lse_ref[0] = jnp.broadcast_to(m + jnp.log(den), (SWA_HEADS, SWA_DH))


def _swa_sample(q_r, k_r, v_new, caches):
    bn = q_r.shape[0]
    nkeys = SWA_GROUPS[0][0] // SWA_GROUPS[0][1]
    views = []
    for (win, dil), cache in zip(SWA_GROUPS, caches):
        assert cache.shape[1] == win and win // dil == nkeys
        views.append(cache.reshape(bn, nkeys, dil, 2, SWA_HEADS, SWA_DH))
    heads = lambda a: a.reshape(bn, N_SWA, SWA_HEADS, SWA_DH)
    row = pl.BlockSpec((1, N_SWA, SWA_HEADS, SWA_DH), lambda b: (b, 0, 0, 0))
    cspec = pl.BlockSpec((1, nkeys, 1, 2, SWA_HEADS, SWA_DH), lambda b: (b, 0, 0, 0, 0, 0))
    ospec = pl.BlockSpec((1, SWA_HEADS, SWA_DH), lambda b: (b, 0, 0))
    outs = pl.pallas_call(
        _swa_sample_kernel,
        grid=(bn,),
        in_specs=[row, row, row, cspec, cspec, cspec],
        out_specs=[ospec] * (2 * N_SWA),
        out_shape=[jax.ShapeDtypeStruct((bn, SWA_HEADS, SWA_DH), F32)] * (2 * N_SWA),
        compiler_params=_cparams("parallel"),
        name="swa_sample",
    )(heads(q_r), heads(k_r), heads(v_new), *views)
    return [(outs[2 * g].reshape(bn, SWA_W), outs[2 * g + 1].reshape(bn, SWA_W)) for g in range(N_SWA)]


def _mix_kernel(oa_ref, o0_ref, l0_ref, o1_ref, l1_ref, o2_ref, l2_ref, ga_ref, gb_ref, x_ref,
                wa_ref, wb_ref, wo_ref, nw_ref, x1_ref, xn_ref, slab, *, dils, tm):
    def token_major(ref, dil):
        if dil == 1:
            return ref[0, 0]
        cols = []
        for hh in range(SWA_HEADS):
            for r in range(dil):
                slab[hh, pl.ds(r, tm // dil, stride=dil), :] = ref[0, r, :, hh * SWA_DH:(hh + 1) * SWA_DH]
            cols.append(slab[hh])
        return jnp.concatenate(cols, axis=1)

    os = [token_major(r, d) for r, d in zip((o0_ref, o1_ref, o2_ref), dils)]
    ls = [token_major(r, d) for r, d in zip((l0_ref, l1_ref, l2_ref), dils)]
    lmax = jnp.maximum(jnp.maximum(ls[0], ls[1]), ls[2])
    es = [jnp.exp(l - lmax) for l in ls]
    ob = (es[0] * os[0] + es[1] * os[1] + es[2] * os[2]) / (es[0] + es[1] + es[2])
    pa = jnp.dot(oa_ref[0].astype(BF16), wa_ref[...], preferred_element_type=F32)
    pb = jnp.dot(ob.astype(BF16), wb_ref[...], preferred_element_type=F32)
    merged = _sigmoid(ga_ref[0]) * pa + _sigmoid(gb_ref[0]) * pb
    x1 = x_ref[0] + jnp.dot(merged.astype(BF16), wo_ref[...], preferred_element_type=F32)
    x1_ref[0] = x1
    xn_ref[0] = x1 * lax.rsqrt(jnp.mean(x1 * x1, axis=-1, keepdims=True) + EPS) * nw_ref[...]


def _mix(o_a3, parts, dils, h3, x3, wa, wb, wo, norm_ffn, tm):
    bn, t, _ = x3.shape
    rowblk = lambda w, cb=0: pl.BlockSpec((1, tm, w), lambda b, i: (b, i, cb))
    full = lambda a: pl.BlockSpec(a.shape, lambda b, i: (0, 0))
    nw = norm_ffn.reshape(1, D_MODEL)
    part_specs, flat = [], []
    for (o, lse), dil in zip(parts, dils):
        spec = pl.BlockSpec((1, dil, tm // dil, SWA_W), lambda b, i: (b, 0, i, 0))
        part_specs += [spec, spec]
        flat += [o, lse]
    return pl.pallas_call(
        functools.partial(_mix_kernel, dils=dils, tm=tm),
        grid=(bn, t // tm),
        in_specs=[rowblk(GDN_VW)] + part_specs
                 + [rowblk(D_MODEL, COL_GA // D_MODEL), rowblk(D_MODEL, COL_GB // D_MODEL),
                    rowblk(D_MODEL), full(wa), full(wb), full(wo), full(nw)],
        out_specs=[rowblk(D_MODEL), rowblk(D_MODEL)],
        out_shape=[jax.ShapeDtypeStruct((bn, t, D_MODEL), F32)] * 2,
        scratch_shapes=[pltpu.VMEM((SWA_HEADS, tm, SWA_DH), F32)],
        compiler_params=_cparams("parallel", "parallel"),
        name="mix_out",
    )(o_a3, *flat, h3, h3, x3, wa, wb, wo, nw)


def _topk_rows(vals, k, payload=None):
    n, t = vals.shape
    idx = lax.broadcasted_iota(jnp.int32, (n, t), 0).astype(F32)
    out_v, out_i = [], []
    for _ in range(k):
        m = jnp.max(vals, axis=0, keepdims=True)
        am = jnp.min(jnp.where(vals == m, idx, float(n)), axis=0, keepdims=True)
        hit = idx == am
        out_v.append(m)
        out_i.append(am if payload is None else jnp.sum(jnp.where(hit, payload, 0.0), axis=0, keepdims=True))
        vals = jnp.where(hit, -jnp.inf, vals)
    return jnp.concatenate(out_v, axis=0), jnp.concatenate(out_i, axis=0)


def _route_kernel(q_ref, keys_ref, eid_ref, gate_ref):
    half = PEER_DKEY // 2
    eids, gates = [], []
    for h in range(PEER_HEADS):
        tops = []
        for p in range(2):
            qs = q_ref[:, (2 * h + p) * half:(2 * h + p + 1) * half].astype(BF16)
            st = _dot_nt(keys_ref[h, p].astype(BF16), qs)
            tops.append(_topk_rows(st, PEER_TOPK))
        (v1, i1), (v2, i2) = tops
        sub = SUBLANE
        assert PEER_TOPK == 2 * sub
        rows = [(slice(0, 1), slice(0, PEER_TOPK))]
        rows += [(slice(i, i + 1), slice(0, sub)) for i in range(1, sub)]
        rows += [(slice(sub, PEER_TOPK), slice(0, 1))]
        cand = jnp.concatenate([v1[a] + v2[b] for a, b in rows], axis=0)
        ecand = jnp.concatenate([i1[a] * float(PEER_NKEYS) + i2[b] for a, b in rows], axis=0)
        sc, e = _topk_rows(cand, PEER_TOPK, payload=ecand)
        ex = jnp.exp(sc - sc[0:1])
        gates.append(ex / jnp.sum(ex, axis=0, keepdims=True))
        eids.append(e)
    eid_ref[...] = jnp.concatenate(eids, axis=0).astype(jnp.int32).T
    gate_ref[...] = jnp.concatenate(gates, axis=0)


def _route(qp, sub_keys, tt):
    m = qp.shape[0]
    return pl.pallas_call(
        _route_kernel,
        grid=(m // tt,),
        in_specs=[pl.BlockSpec((tt, PEER_HEADS * PEER_DKEY), lambda i: (i, 0)),
                  pl.BlockSpec(sub_keys.shape, lambda i: (0, 0, 0, 0))],
        out_specs=[pl.BlockSpec((tt, PEER_SEL), lambda i: (i, 0)), pl.BlockSpec((PEER_SEL, tt), lambda i: (0, i))],
        out_shape=[jax.ShapeDtypeStruct((m, PEER_SEL), jnp.int32), jax.ShapeDtypeStruct((PEER_SEL, m), F32)],
        compiler_params=_cparams("parallel"),
        name="peer_route",
    )(qp, sub_keys)


PEER_TOKENS_PER_STEP = 256
PEER_NBUF = 8


PEER_CHUNKS = D_MODEL // LANE


def _pack_kernel(d_ref, u_ref, o_ref):
    hi = lax.bitcast_convert_type(d_ref[...].astype(BF16).astype(F32), jnp.uint32)
    lo = lax.bitcast_convert_type(u_ref[...].astype(BF16).astype(F32), jnp.uint32) >> 16
    word = hi | lo
    for c in range(PEER_CHUNKS):
        o_ref[:, c, :] = word[:, c * LANE:(c + 1) * LANE]


def _pack_peer_table(expert_down, expert_up, tr=256):
    e, d = expert_down.shape
    return pl.pallas_call(
        _pack_kernel,
        grid=(e // tr,),
        in_specs=[pl.BlockSpec((tr, d), lambda i: (i, 0))] * 2,
        out_specs=pl.BlockSpec((tr, PEER_CHUNKS, LANE), lambda i: (i, 0, 0)),
        out_shape=jax.ShapeDtypeStruct((e, PEER_CHUNKS, LANE), jnp.uint32),
        compiler_params=_cparams("parallel"),
        name="peer_pack",
    )(expert_down, expert_up)


def _peer_kernel(eid_ref, gate_ref, xn_ref, x1_ref, nw_ref, tbl_hbm, y_ref, *scratch, n_tok):
    nbuf = PEER_NBUF
    tbufs, (sem, xrow, yrow) = scratch[:nbuf], scratch[nbuf:]

    def issue(t, slot, rows):
        for j in rows:
            pltpu.make_async_copy(tbl_hbm.at[eid_ref[t, j]], tbufs[slot].at[:, j, :], sem.at[slot]).start(priority=j % 2)

    def wait(slot):
        pltpu.make_async_copy(tbufs[slot], tbufs[slot], sem.at[slot]).wait()

    lane = lax.broadcasted_iota(jnp.int32, (PEER_SEL, LANE), 1)
    hi_mask = jnp.uint32(0xFFFF0000)
    per_chunk = 3
    mid_rows = range(2 * PEER_CHUNKS * per_chunk, PEER_SEL)

    def compute(t, slot, t_ahead):
        def issue_rows(rows):
            if t_ahead is not None:
                issue(t_ahead, (slot - 1) % nbuf, rows)

        xrow[...] = xn_ref[pl.ds(t, 1), :]
        acc = jnp.zeros((PEER_SEL, LANE), F32)
        for c in range(PEER_CHUNKS):
            down = lax.bitcast_convert_type(tbufs[slot][c] & hi_mask, F32)
            acc = acc + down * xrow[:, c * LANE:(c + 1) * LANE]
            issue_rows(range(c * per_chunk, (c + 1) * per_chunk))
        act = jnp.sum(acc, axis=-1, keepdims=True)
        gate = jnp.sum(jnp.where(lane == t % LANE, gate_ref[t // LANE], 0.0), axis=-1, keepdims=True)
        issue_rows(mid_rows)
        gelu = 0.5 * act * (1.0 + lax.erf(act * (2.0 ** -0.5)))
        w = gate * gelu
        for c in range(PEER_CHUNKS):
            up = lax.bitcast_convert_type(tbufs[slot][c] << 16, F32)
            yrow[:, c * LANE:(c + 1) * LANE] = jnp.sum(up * w, axis=0, keepdims=True)
            issue_rows(range((PEER_CHUNKS + c) * per_chunk, (PEER_CHUNKS + c + 1) * per_chunk))
        y_ref[pl.ds(t, 1), :] = x1_ref[pl.ds(t, 1), :] + yrow[...]

    assert n_tok % nbuf == 0
    for k in range(nbuf - 1):
        issue(k, k, range(PEER_SEL))

    def group(p, carry):
        for k in range(nbuf):
            t = nbuf * p + k
            wait(k)
            compute(t, k, t + nbuf - 1)
        return carry

    lax.fori_loop(0, n_tok // nbuf - 1, group, 0)
    for k in range(nbuf):
        t = n_tok - nbuf + k
        wait(k)
        compute(t, k, t + nbuf - 1 if k == 0 else None)

    x2 = y_ref[...]
    y_ref[...] = x2 * lax.rsqrt(jnp.mean(x2 * x2, axis=-1, keepdims=True) + EPS) * nw_ref[...]


def _peer(eid, gates_t, xn2, x1, norm_final, table, n_tok):
    m = xn2.shape[0]
    kern = functools.partial(_peer_kernel, n_tok=n_tok)
    tiles = gates_t.shape[1] // LANE
    gates3 = gates_t.reshape(PEER_SEL, tiles, LANE).transpose(1, 0, 2)
    gblk = max(n_tok // LANE, 1)
    return pl.pallas_call(
        kern,
        grid=(m // n_tok,),
        in_specs=[pl.BlockSpec((gblk * LANE, PEER_SEL), lambda i: (i, 0), memory_space=pltpu.SMEM),
                  pl.BlockSpec((gblk, PEER_SEL, LANE), lambda i: (i, 0, 0)),
                  pl.BlockSpec((n_tok, D_MODEL), lambda i: (i, 0)),
                  pl.BlockSpec((n_tok, D_MODEL), lambda i: (i, 0)),
                  pl.BlockSpec((1, D_MODEL), lambda i: (0, 0)),
                  pl.BlockSpec(memory_space=pl.ANY)],
        out_specs=pl.BlockSpec((n_tok, D_MODEL), lambda i: (i, 0)),
        out_shape=jax.ShapeDtypeStruct((m, D_MODEL), F32),
        scratch_shapes=[pltpu.VMEM((PEER_CHUNKS, PEER_SEL, LANE), jnp.uint32)] * PEER_NBUF
                       + [pltpu.SemaphoreType.DMA((PEER_NBUF,)),
                          pltpu.VMEM((1, D_MODEL), F32), pltpu.VMEM((1, D_MODEL), F32)],
        compiler_params=_cparams("arbitrary"),
        name="peer_experts",
    )(eid, gates3, xn2, x1, norm_final.reshape(1, D_MODEL), table)


def _pick(m, candidates):
    for c in candidates:
        if m % c == 0:
            return c
    return m


def _layer(x, pos, tail8, s0, caches, w, chunk):
    bn, t, _ = x.shape
    m = bn * t
    x2d = x.reshape(m, D_MODEL)
    tm_big = _pick(m, (1024, 512, 256, 128, 64, 32))

    xn = _rmsnorm(x2d, w["norm_mix"], _pick(m, (512, 256, 128, 64, 32)), BF16)
    h_main = _matmul(xn, w["w_main"], tm_big, 1280, "proj_in")
    h_ab = _matmul(xn, w["w_ab"], tm_big, LANE, "proj_ab")

    tp = -(-t // chunk) * chunk
    h3 = h_main.reshape(bn, t, N_MAIN)
    ab3 = h_ab.reshape(bn, t, LANE)
    h3p = h3 if tp == t else jnp.pad(h3, ((0, 0), (0, tp - t), (0, 0)))
    ab3p = ab3 if tp == t else jnp.pad(ab3, ((0, 0), (0, tp - t), (0, 0)))
    o_a, s_new = _gdn(h3p, ab3p, w["conv_w"], tail8, s0, w["a_log"], w["dt_bias"], w["gdn_norm"], chunk, t)
    o_a = o_a[:, :t]

    cosf, sinf = _rope_tables(pos)
    kv_new = []
    if caches is None:
        qkv = _rope_split(h3, cosf, sinf, 256)
        parts = [_swa_prompt_group(q, k, v, gi, 2 if q.shape[2] % (2 * SWA_WIN_BLK) == 0 else 1)
                 for gi, (q, k, v) in enumerate(qkv)]
        dils = tuple(d for _, d in SWA_GROUPS)
        for (win, dil), (_, k, v) in zip(SWA_GROUPS, qkv):
            keep = min(win, t)
            tok = lambda a: (a[:, :, (t - keep) // dil:].transpose(0, 2, 1, 3)
                             .reshape(bn, keep, SWA_HEADS, SWA_DH))
            kv_new.append(jnp.stack([tok(k), tok(v)], axis=2))
        mix_shape = (bn, t)
    else:
        cosf, sinf = jnp.tile(cosf, (bn, 1)), jnp.tile(sinf, (bn, 1))
        q_r, k_r = _rope(h_main, cosf, sinf, _pick(m, (512, 256, 128, 64, 32)))
        v_new = h_main[:, COL_SV:COL_SV + SWA_QKV]
        parts = [(o.reshape(1, 1, m, SWA_W), lse.reshape(1, 1, m, SWA_W))
                 for o, lse in _swa_sample(q_r, k_r, v_new, caches)]
        dils = (1,) * N_SWA
        for gi in range(N_SWA):
            kk = k_r[:, gi * SWA_W:(gi + 1) * SWA_W].reshape(bn, t, SWA_HEADS, SWA_DH)
            vv = v_new[:, gi * SWA_W:(gi + 1) * SWA_W].reshape(bn, t, SWA_HEADS, SWA_DH)
            kv_new.append(jnp.stack([kk, vv], axis=2))
        mix_shape = (1, m)

    mb, mt = mix_shape
    x1, xn2 = _mix(o_a.reshape(mb, mt, GDN_VW), parts, dils, h_main.reshape(mb, mt, N_MAIN),
                   x.reshape(mb, mt, D_MODEL), w["w_br_a"], w["w_br_b"], w["w_out"], w["norm_ffn"],
                   _pick(mt, (256, 128, 64, 32)))
    x1, xn2 = x1.reshape(m, D_MODEL), xn2.reshape(m, D_MODEL)

    mp = -(-m // LANE) * LANE
    xq = xn2 if mp == m else jnp.pad(xn2, ((0, mp - m), (0, 0)))
    qp = _matmul(xq, w["w_query"], _pick(mp, (1024, 512, 256, 128)), 512, "peer_query")
    eid, gates_t = _route(qp, w["sub_keys"], LANE)
    y = _peer(eid, gates_t, xn2, x1, w["norm_final"], w["peer_table"], min(m, PEER_TOKENS_PER_STEP))
    return y.reshape(bn, t, D_MODEL), s_new, h3, kv_new


def kernel(x_prompt, x_sample, state_gdn, state_conv, cache_kv_w128, cache_kv_w512, cache_kv_w2048,
           norm_mix, w_in, conv_w, a_log, dt_bias, gdn_norm, w_br_a, w_br_b, w_out, norm_ffn,
           w_query, sub_keys, expert_down, expert_up, norm_final):
    w_main = jnp.concatenate([w_in[:, IN_GA:], w_in[:, :IN_AB], w_in[:, IN_SQ:IN_GA]], axis=1).astype(BF16)
    w_ab = jnp.pad(w_in[:, IN_AB:IN_SQ], ((0, 0), (0, LANE - 2 * GDN_HEADS))).astype(BF16)
    w = dict(norm_mix=norm_mix, w_main=w_main, w_ab=w_ab, conv_w=conv_w, a_log=a_log, dt_bias=dt_bias,
             gdn_norm=gdn_norm, w_br_a=w_br_a.astype(BF16), w_br_b=w_br_b.astype(BF16), w_out=w_out.astype(BF16),
             norm_ffn=norm_ffn, w_query=w_query.astype(BF16), sub_keys=sub_keys,
             peer_table=_pack_peer_table(expert_down, expert_up), norm_final=norm_final)

    bp, tlen = x_prompt.shape[:2]
    bs, ts = x_sample.shape[:2]
    caches = (cache_kv_w128, cache_kv_w512, cache_kv_w2048)

    tail8 = jnp.concatenate([jnp.zeros((bs, SUBLANE - (CONV_W - 1), GDN_CONV_CH), F32), state_conv], axis=1)
    y_s, gdn_s, h3_s, kv_new = _layer(
        x_sample, PAST_LEN + jnp.arange(ts, dtype=jnp.int32), tail8, state_gdn, caches, w, SUBLANE)
    conv_s = jnp.concatenate([state_conv, h3_s[:, :, COL_Q:COL_Q + GDN_CONV_CH]], axis=1)[:, ts:]
    kv_s = [jnp.concatenate([cache, new.astype(cache.dtype)], axis=1)[:, ts:] for cache, new in zip(caches, kv_new)]

    y_p, gdn_p, h3_p, kv_p = _layer(
        x_prompt, jnp.arange(tlen, dtype=jnp.int32),
        jnp.zeros((bp, SUBLANE, GDN_CONV_CH), F32), jnp.zeros((bp, GDN_HEADS, GDN_DK, GDN_DV), F32),
        None, w, CHUNK)
    conv_p = h3_p[:, tlen - (CONV_W - 1):, COL_Q:COL_Q + GDN_CONV_CH]

    return (y_p, y_s, gdn_p.astype(x_prompt.dtype), conv_p, kv_p[0], kv_p[1], kv_p[2],
            gdn_s.astype(state_gdn.dtype), conv_s, kv_s[0], kv_s[1], kv_s[2])
```

```python
import functools
import math

import jax
import jax.numpy as jnp
import numpy as np
from jax import lax
from jax.experimental import pallas as pl
from jax.experimental.pallas import tpu as pltpu

F32 = jnp.float32
BF16 = jnp.bfloat16

LANE = 128
SUBLANE = 8
VMEM_LIMIT = 56 * 1024 * 1024

D_MODEL = 2048
PAST_LEN = 16384
EPS = 1e-6
GDN_HEADS = 8
GDN_DK = 128
GDN_DV = 128
GDN_QK = GDN_HEADS * GDN_DK
GDN_VW = GDN_HEADS * GDN_DV
GDN_CONV_CH = 2 * GDN_QK + GDN_VW
CONV_W = 4
CHUNK = 64
SWA_GROUPS = ((128, 1), (512, 4), (2048, 16))
N_SWA = 3
SWA_HEADS = 4
SWA_DH = 128
SWA_W = SWA_HEADS * SWA_DH
SWA_QKV = N_SWA * SWA_W
ROT_DIM = SWA_DH // 4
ROPE_THETA = 500000.0
PEER_HEADS = 8
PEER_NKEYS = 128
PEER_DKEY = 256
PEER_TOPK = 16
PEER_SEL = PEER_HEADS * PEER_TOPK

COL_GA = 0
COL_GB = D_MODEL
COL_Q = 2 * D_MODEL
COL_K = COL_Q + GDN_QK
COL_V = COL_K + GDN_QK
COL_Z = COL_V + GDN_VW
COL_SQ = COL_Z + GDN_VW
COL_SK = COL_SQ + SWA_QKV
COL_SV = COL_SK + SWA_QKV
N_MAIN = COL_SV + SWA_QKV
IN_AB = 2 * GDN_QK + 2 * GDN_VW
IN_SQ = IN_AB + 2 * GDN_HEADS
IN_GA = IN_SQ + 3 * SWA_QKV
NEG_BIG = -1e30


def _cparams(*sem):
    return pltpu.CompilerParams(dimension_semantics=sem, vmem_limit_bytes=VMEM_LIMIT)


def _sigmoid(x):
    return 1.0 / (1.0 + jnp.exp(-x))


def _dot_sel(sel, b, dims=(((1,), (0,)), ((), ()))):
    b1 = b.astype(BF16)
    r1 = b - b1.astype(F32)
    b2 = r1.astype(BF16)
    b3 = (r1 - b2.astype(F32)).astype(BF16)
    s = sel.astype(BF16)
    dot = lambda y: lax.dot_general(s, y, dims, preferred_element_type=F32)
    return dot(b1) + (dot(b2) + dot(b3))


def _split_bf16(a):
    hi = a.astype(BF16)
    return hi, (a - hi.astype(F32)).astype(BF16)


def _dot3(a, b, dims=(((1,), (0,)), ((), ()))):
    a_hi, a_lo = _split_bf16(a)
    b_hi, b_lo = _split_bf16(b)
    dot = lambda x, y: lax.dot_general(x, y, dims, preferred_element_type=F32)
    return dot(a_hi, b_hi) + (dot(a_hi, b_lo) + dot(a_lo, b_hi))


def _dot3_nt(a, b):
    return _dot3(a, b, (((1,), (1,)), ((), ())))


def _dot3_tn(a, b):
    return _dot3(a, b, (((0,), (0,)), ((), ())))


def _dot_nt(a, b):
    return lax.dot_general(a, b, (((1,), (1,)), ((), ())), preferred_element_type=F32)


def _rmsnorm_kernel(x_ref, w_ref, o_ref):
    x = x_ref[...]
    y = x * lax.rsqrt(jnp.mean(x * x, axis=-1, keepdims=True) + EPS) * w_ref[...]
    o_ref[...] = y.astype(o_ref.dtype)


def _rmsnorm(x, w, tm, out_dtype):
    m, d = x.shape
    return pl.pallas_call(
        _rmsnorm_kernel,
        grid=(m // tm,),
        in_specs=[pl.BlockSpec((tm, d), lambda i: (i, 0)), pl.BlockSpec((1, d), lambda i: (0, 0))],
        out_specs=pl.BlockSpec((tm, d), lambda i: (i, 0)),
        out_shape=jax.ShapeDtypeStruct((m, d), out_dtype),
        compiler_params=_cparams("parallel"),
        name="rmsnorm",
    )(x, w.reshape(1, d))


def _mm_kernel(x_ref, w_ref, o_ref):
    o_ref[...] = jnp.dot(x_ref[...].astype(BF16), w_ref[...], preferred_element_type=F32)


def _matmul(x, w, tm, tn, name):
    m, k = x.shape
    n = w.shape[1]
    return pl.pallas_call(
        _mm_kernel,
        grid=(n // tn, m // tm),
        in_specs=[pl.BlockSpec((tm, k), lambda j, i: (i, 0)), pl.BlockSpec((k, tn), lambda j, i: (0, j))],
        out_specs=pl.BlockSpec((tm, tn), lambda j, i: (i, j)),
        out_shape=jax.ShapeDtypeStruct((m, n), F32),
        compiler_params=_cparams("parallel", "parallel"),
        name=name,
    )(x, w)


def _rope_kernel(q_ref, k_ref, cos_ref, sin_ref, qo_ref, ko_ref):
    cosf = cos_ref[...]
    sinf = sin_ref[...]
    lane = lax.broadcasted_iota(jnp.int32, cosf.shape, 1)
    first = lane < ROT_DIM // 2
    for src, dst in ((q_ref, qo_ref), (k_ref, ko_ref)):
        for hh in range(SWA_HEADS):
            sl = slice(hh * SWA_DH, (hh + 1) * SWA_DH)
            x = src[:, sl]
            partner = jnp.where(first, pltpu.roll(x, SWA_DH - ROT_DIM // 2, axis=1), pltpu.roll(x, ROT_DIM // 2, axis=1))
            dst[:, sl] = x * cosf + partner * sinf


def _rope(h_main, cosf, sinf, tm):
    m = h_main.shape[0]
    nt = cosf.shape[0] // tm
    qb, kb = COL_SQ // SWA_W, COL_SK // SWA_W
    return pl.pallas_call(
        _rope_kernel,
        grid=(m // tm, N_SWA),
        in_specs=[
            pl.BlockSpec((tm, SWA_W), lambda i, g: (i, qb + g)),
            pl.BlockSpec((tm, SWA_W), lambda i, g: (i, kb + g)),
            pl.BlockSpec((tm, SWA_DH), lambda i, g: (i % nt, 0)),
            pl.BlockSpec((tm, SWA_DH), lambda i, g: (i % nt, 0)),
        ],
        out_specs=[pl.BlockSpec((tm, SWA_W), lambda i, g: (i, g)), pl.BlockSpec((tm, SWA_W), lambda i, g: (i, g))],
        out_shape=[jax.ShapeDtypeStruct((m, SWA_QKV), F32), jax.ShapeDtypeStruct((m, SWA_QKV), F32)],
        compiler_params=_cparams("parallel", "parallel"),
        name="rope",
    )(h_main, h_main, cosf, sinf)


def _rope_split_kernel(*refs, dils, tm):
    ins, cos_ref, sin_ref = refs[:3 * N_SWA], refs[3 * N_SWA], refs[3 * N_SWA + 1]
    outs, slab = refs[3 * N_SWA + 2:6 * N_SWA + 2], refs[6 * N_SWA + 2]
    cosf = cos_ref[...]
    sinf = sin_ref[...]
    first = lax.broadcasted_iota(jnp.int32, cosf.shape, 1) < ROT_DIM // 2
    for gi, dil in enumerate(dils):
        for kind in range(3):
            src, dst = ins[3 * gi + kind], outs[3 * gi + kind]
            for hh in range(SWA_HEADS):
                sl = slice(hh * SWA_DH, (hh + 1) * SWA_DH)
                x = src[0, :, sl]
                if kind < 2:
                    partner = jnp.where(first, pltpu.roll(x, SWA_DH - ROT_DIM // 2, axis=1),
                                        pltpu.roll(x, ROT_DIM // 2, axis=1))
                    x = x * cosf + partner * sinf
                if dil == 1:
                    dst[0, 0, :, sl] = x
                else:
                    slab[hh] = x
                    for r in range(dil):
                        dst[0, r, :, sl] = slab[hh, pl.ds(r, tm // dil, stride=dil), :]


def _rope_split(h3, cosf, sinf, tm):
    bn, t, _ = h3.shape
    dils = tuple(d for _, d in SWA_GROUPS)
    in_specs, out_specs, out_shape = [], [], []
    for gi, dil in enumerate(dils):
        for off in (COL_SQ, COL_SK, COL_SV):
            cb = off // SWA_W + gi
            in_specs.append(pl.BlockSpec((1, tm, SWA_W), lambda b, i, cb=cb: (b, i, cb)))
            out_specs.append(pl.BlockSpec((1, dil, tm // dil, SWA_W), lambda b, i: (b, 0, i, 0)))
            out_shape.append(jax.ShapeDtypeStruct((bn, dil, t // dil, SWA_W), F32))
    tbl = pl.BlockSpec((tm, SWA_DH), lambda b, i: (i, 0))
    outs = pl.pallas_call(
        functools.partial(_rope_split_kernel, dils=dils, tm=tm),
        grid=(bn, t // tm),
        in_specs=in_specs + [tbl, tbl],
        out_specs=out_specs,
        out_shape=out_shape,
        scratch_shapes=[pltpu.VMEM((SWA_HEADS, tm, SWA_DH), F32)],
        compiler_params=_cparams("parallel", "parallel"),
        name="rope_split",
    )(*([h3] * (3 * N_SWA)), cosf, sinf)
    return [tuple(outs[3 * gi:3 * gi + 3]) for gi in range(N_SWA)]


def _rope_tables(pos):
    half = ROT_DIM // 2
    inv = ROPE_THETA ** (-jnp.arange(half, dtype=F32) * 2.0 / ROT_DIM)
    ang = pos.astype(F32)[:, None] * inv[None, :]
    cos, sin = jnp.cos(ang), jnp.sin(ang)
    n = pos.shape[0]
    cosf = jnp.concatenate([cos, cos, jnp.ones((n, SWA_DH - ROT_DIM), F32)], axis=1)
    sinf = jnp.concatenate([-sin, sin, jnp.zeros((n, SWA_DH - ROT_DIM), F32)], axis=1)
    return cosf, sinf


def _gdn_kernel(alog_ref, dtb_ref, q_ref, k_ref, v_ref, z_ref, ab_ref, cwq_ref, cwk_ref, cwv_ref,
                tq_ref, tk_ref, tv_ref, s0_ref, nw_ref, o_ref, sout_ref,
                s_scr, tailq, tailk, tailv, *, chunk, t_total, hps):
    c = pl.program_id(2)

    @pl.when(c == 0)
    def _():
        s_scr[...] = s0_ref[0]
        tailq[...] = tq_ref[0]
        tailk[...] = tk_ref[0]
        tailv[...] = tv_ref[0]

    C = chunk
    heads = range(hps)
    sls = [slice(hh * LANE, (hh + 1) * LANE) for hh in heads]
    hidx = [pl.program_id(1) * hps + hh for hh in heads]
    row8 = lax.broadcasted_iota(jnp.int32, (SUBLANE, LANE), 0)

    def conv(src_ref, tail_ref, w_ref, sl):
        raw = src_ref[0, :, sl]
        w = w_ref[:, sl]
        t8 = tail_ref[:, sl]
        y = raw * w[CONV_W - 1:CONV_W, :]
        for s in range(1, CONV_W):
            rolled = pltpu.roll(raw, s, axis=0)
            top = jnp.where(row8 < s, pltpu.roll(t8, s, axis=0), rolled[0:SUBLANE])
            sh = top if C == SUBLANE else jnp.concatenate([top, rolled[SUBLANE:]], axis=0)
            y = y + sh * w[CONV_W - 1 - s:CONV_W - s, :]
        tail_ref[:, sl] = raw[C - SUBLANE:C]
        return y * _sigmoid(y)

    def l2n(x):
        return x * lax.rsqrt(jnp.sum(x * x, axis=-1, keepdims=True) + EPS)

    q = [l2n(conv(q_ref, tailq, cwq_ref, sl)) * (GDN_DK ** -0.5) for sl in sls]
    k = [l2n(conv(k_ref, tailk, cwk_ref, sl)) for sl in sls]
    v = [conv(v_ref, tailv, cwv_ref, sl) for sl in sls]

    lane = lax.broadcasted_iota(jnp.int32, (C, LANE), 1)
    rowc = lax.broadcasted_iota(jnp.int32, (C, 1), 0)
    valid = (c * C + rowc) < t_total
    ab = ab_ref[0]
    beta, g = [], []
    for h in hidx:
        a = jnp.sum(jnp.where(lane == h, ab, 0.0), axis=-1, keepdims=True)
        b = jnp.sum(jnp.where(lane == h + GDN_HEADS, ab, 0.0), axis=-1, keepdims=True)
        sp_in = a + dtb_ref[h]
        softplus = jnp.maximum(sp_in, 0.0) + jnp.log(1.0 + jnp.exp(-jnp.abs(sp_in)))
        beta.append(jnp.where(valid, _sigmoid(b), 0.0))
        g.append(jnp.where(valid, -jnp.exp(jnp.full((C, 1), alog_ref[h], F32)) * softplus, 0.0))

    ri = lax.broadcasted_iota(jnp.int32, (C, C), 0)
    ci = lax.broadcasted_iota(jnp.int32, (C, C), 1)
    causal = ri >= ci
    strict = ri > ci
    tril = causal.astype(F32)
    e0 = (lane == 0).astype(F32)
    eye = (ri == ci).astype(F32)
    gc_b = [_dot_sel(tril, jnp.broadcast_to(gh, (C, LANE))) for gh in g]
    gc_row = [_dot_sel(e0, gch, (((1,), (1,)), ((), ()))) for gch in gc_b]
    decay = [jnp.where(causal, jnp.exp(jnp.where(causal, gch[:, :C] - grh, 0.0)), 0.0)
             for gch, grh in zip(gc_b, gc_row)]
    kb = [kh * bh for kh, bh in zip(k, beta)]
    lmat = [jnp.where(strict, _dot3_nt(kbh, kh) * dh, 0.0) for kbh, kh, dh in zip(kb, k, decay)]
    def low_left(lg):
        return jnp.logical_and((ri >> (lg + 1)) == (ci >> (lg + 1)),
                               jnp.logical_and(((ri >> lg) & 1) == 1, ((ci >> lg) & 1) == 0))

    tinv = [eye - jnp.where(low_left(0), lh, 0.0) for lh in lmat]
    for lg in range(1, int(math.log2(C))):
        ct = [_dot3(jnp.where(low_left(lg), lh, 0.0), th) for lh, th in zip(lmat, tinv)]
        tinv = [th - _dot3(th, cth) for th, cth in zip(tinv, ct)]
    eg = [jnp.exp(gch) for gch in gc_b]
    value = [_dot3(th, vh * bh) for th, vh, bh in zip(tinv, v, beta)]
    kcd = [_dot3(th, kbh * egh) for th, kbh, egh in zip(tinv, kb, eg)]
    attn = [jnp.where(causal, _dot_nt(qh.astype(BF16), kh.astype(BF16)) * dh, 0.0) for qh, kh, dh in zip(q, k, decay)]
    glast = [gch[C - 1:C, :] for gch in gc_b]
    kdec = [kh * jnp.exp(glh - gch) for kh, glh, gch in zip(k, glast, gc_b)]

    s = [s_scr[hh] for hh in heads]
    v_new = [vh - _dot3(kh, sh) for vh, kh, sh in zip(value, kcd, s)]
    bdot = lambda a, b: jnp.dot(a.astype(BF16), b.astype(BF16), preferred_element_type=F32)
    o = [bdot(qh * egh, sh) + bdot(ah, vnh) for qh, egh, sh, ah, vnh in zip(q, eg, s, attn, v_new)]
    for hh in heads:
        s_scr[hh] = s[hh] * jnp.exp(glast[hh]) + _dot3_tn(kdec[hh], v_new[hh])

    nw = nw_ref[...]
    for hh, sl in zip(heads, sls):
        z = z_ref[0, :, sl]
        on = o[hh] * lax.rsqrt(jnp.mean(o[hh] * o[hh], axis=-1, keepdims=True) + EPS) * nw
        o_ref[0, :, sl] = on * (z * _sigmoid(z))

    @pl.when(c == pl.num_programs(2) - 1)
    def _():
        sout_ref[0] = s_scr[...]


GDN_HEADS_PER_STEP = 8


def _gdn(h_main3, ab3, conv_w, tail8, s0, a_log, dt_bias, gdn_norm, chunk, t_total):
    bn, tp, _ = h_main3.shape
    nc = tp // chunk
    hps = GDN_HEADS_PER_STEP
    w = hps * LANE
    hq, hk, hv, hz = COL_Q // w, COL_K // w, COL_V // w, COL_Z // w
    cq, ck, cv = 0, GDN_QK // w, 2 * GDN_QK // w
    col = lambda off: pl.BlockSpec((1, chunk, w), lambda b, h, c: (b, c, off + h))
    cw = lambda off: pl.BlockSpec((CONV_W, w), lambda b, h, c: (0, off + h))
    tl = lambda off: pl.BlockSpec((1, SUBLANE, w), lambda b, h, c: (b, 0, off + h))
    smem = pl.BlockSpec(memory_space=pltpu.SMEM)
    kern = functools.partial(_gdn_kernel, chunk=chunk, t_total=t_total, hps=hps)
    return pl.pallas_call(
        kern,
        grid=(bn, GDN_HEADS // hps, nc),
        in_specs=[smem, smem, col(hq), col(hk), col(hv), col(hz),
                  pl.BlockSpec((1, chunk, LANE), lambda b, h, c: (b, c, 0)),
                  cw(cq), cw(ck), cw(cv), tl(cq), tl(ck), tl(cv),
                  pl.BlockSpec((1, hps, GDN_DK, GDN_DV), lambda b, h, c: (b, h, 0, 0)),
                  pl.BlockSpec((1, GDN_DV), lambda b, h, c: (0, 0))],
        out_specs=[pl.BlockSpec((1, chunk, w), lambda b, h, c: (b, c, h)),
                   pl.BlockSpec((1, hps, GDN_DK, GDN_DV), lambda b, h, c: (b, h, 0, 0))],
        out_shape=[jax.ShapeDtypeStruct((bn, tp, GDN_VW), F32),
                   jax.ShapeDtypeStruct((bn, GDN_HEADS, GDN_DK, GDN_DV), F32)],
        scratch_shapes=[pltpu.VMEM((hps, GDN_DK, GDN_DV), F32)] + [pltpu.VMEM((SUBLANE, w), F32)] * 3,
        compiler_params=_cparams("parallel", "parallel", "arbitrary"),
        name="gdn",
    )(a_log, dt_bias, h_main3, h_main3, h_main3, h_main3, ab3, conv_w, conv_w, conv_w,
      tail8, tail8, tail8, s0, gdn_norm.reshape(1, GDN_DV))


SWA_WIN_BLK = 128


def _swa_kernel(q_ref, kp_ref, kc_ref, vp_ref, vc_ref, o_ref, lse_ref, *, nsub):
    qi = pl.program_id(2)
    n = SWA_WIN_BLK
    ri = lax.broadcasted_iota(jnp.int32, (n, n), 0)
    ci = lax.broadcasted_iota(jnp.int32, (n, n), 1)
    mask_cur = ci <= ri
    scale = SWA_DH ** -0.5
    for sb in range(nsub):
        rows = slice(sb * n, (sb + 1) * n)
        prows = slice((sb - 1) * n, sb * n)
        mask_prev = jnp.logical_and(ci >= ri, qi > 0) if sb == 0 else ci >= ri
        for hh in range(SWA_HEADS):
            sl = slice(hh * SWA_DH, (hh + 1) * SWA_DH)
            kp = kp_ref[0, 0, :, sl] if sb == 0 else kc_ref[0, 0, prows, sl]
            vp = vp_ref[0, 0, :, sl] if sb == 0 else vc_ref[0, 0, prows, sl]
            q = q_ref[0, 0, rows, sl].astype(BF16)
            sp = jnp.where(mask_prev, _dot_nt(q, kp.astype(BF16)) * scale, NEG_BIG)
            sc = jnp.where(mask_cur, _dot_nt(q, kc_ref[0, 0, rows, sl].astype(BF16)) * scale, NEG_BIG)
            m = jnp.maximum(jnp.max(sp, axis=-1, keepdims=True), jnp.max(sc, axis=-1, keepdims=True))
            pp = jnp.exp(sp - m)
            pc = jnp.exp(sc - m)
            ssum = jnp.sum(pp, axis=-1, keepdims=True) + jnp.sum(pc, axis=-1, keepdims=True)
            acc = (jnp.dot(pp.astype(BF16), vp.astype(BF16), preferred_element_type=F32)
                   + jnp.dot(pc.astype(BF16), vc_ref[0, 0, rows, sl].astype(BF16), preferred_element_type=F32))
            o_ref[0, 0, rows, sl] = acc / ssum
            lse_ref[0, 0, rows, sl] = jnp.broadcast_to(m + jnp.log(ssum), (n, SWA_DH))


def _swa_prompt_group(q, k, v, gi, nsub):
    bn, dil, tl, _ = q.shape
    qblk = nsub * SWA_WIN_BLK
    cur = lambda b, r, i: (b, r, i, 0)
    prev = lambda b, r, i: (b, r, jnp.maximum(i * nsub - 1, 0), 0)
    blk = (1, 1, qblk, SWA_W)
    pblk = (1, 1, SWA_WIN_BLK, SWA_W)
    return pl.pallas_call(
        functools.partial(_swa_kernel, nsub=nsub),
        grid=(bn, dil, tl // qblk),
        in_specs=[pl.BlockSpec(blk, cur), pl.BlockSpec(pblk, prev), pl.BlockSpec(blk, cur),
                  pl.BlockSpec(pblk, prev), pl.BlockSpec(blk, cur)],
        out_specs=[pl.BlockSpec(blk, cur)] * 2,
        out_shape=[jax.ShapeDtypeStruct(q.shape, F32)] * 2,
        compiler_params=_cparams("parallel", "parallel", "arbitrary"),
        name=f"swa_prompt_g{gi}",
    )(q, k, k, v, v)


def _swa_sample_kernel(q_ref, kn_ref, vn_ref, c0_ref, c1_ref, c2_ref, *out_refs):
    scale = SWA_DH ** -0.5
    for gi, c_ref in enumerate((c0_ref, c1_ref, c2_ref)):
        o_ref, lse_ref = out_refs[2 * gi], out_refs[2 * gi + 1]
        q = q_ref[0, gi]
        kn = kn_ref[0, gi]
        vn = vn_ref[0, gi]
        kc = c_ref[0, :, 0, 0]
        vc = c_ref[0, :, 0, 1]
        s = jnp.sum(kc * q[None], axis=-1, keepdims=True) * scale
        sn = jnp.sum(kn * q, axis=-1, keepdims=True) * scale
        m = jnp.maximum(jnp.max(s, axis=0), sn)
        p = jnp.exp(s - m[None])
        pn = jnp.exp(sn - m)
        den = jnp.sum(p, axis=0) + pn
        o_ref[0] = (jnp.sum(p * vc, axis=0) + pn * vn) / den
        lse_ref[0] = jnp.broadcast_to(m + jnp.log(den), (SWA_HEADS, SWA_DH))


def _swa_sample(q_r, k_r, v_new, caches):
    bn = q_r.shape[0]
    nkeys = SWA_GROUPS[0][0] // SWA_GROUPS[0][1]
    views = []
    for (win, dil), cache in zip(SWA_GROUPS, caches):
        assert cache.shape[1] == win and win // dil == nkeys
        views.append(cache.reshape(bn, nkeys, dil, 2, SWA_HEADS, SWA_DH))
    heads = lambda a: a.reshape(bn, N_SWA, SWA_HEADS, SWA_DH)
    row = pl.BlockSpec((1, N_SWA, SWA_HEADS, SWA_DH), lambda b: (b, 0, 0, 0))
    cspec = pl.BlockSpec((1, nkeys, 1, 2, SWA_HEADS, SWA_DH), lambda b: (b, 0, 0, 0, 0, 0))
    ospec = pl.BlockSpec((1, SWA_HEADS, SWA_DH), lambda b: (b, 0, 0))
    outs = pl.pallas_call(
        _swa_sample_kernel,
        grid=(bn,),
        in_specs=[row, row, row, cspec, cspec, cspec],
        out_specs=[ospec] * (2 * N_SWA),
        out_shape=[jax.ShapeDtypeStruct((bn, SWA_HEADS, SWA_DH), F32)] * (2 * N_SWA),
        compiler_params=_cparams("parallel"),
        name="swa_sample",
    )(heads(q_r), heads(k_r), heads(v_new), *views)
    return [(outs[2 * g].reshape(bn, SWA_W), outs[2 * g + 1].reshape(bn, SWA_W)) for g in range(N_SWA)]


def _mix_kernel(oa_ref, o0_ref, l0_ref, o1_ref, l1_ref, o2_ref, l2_ref, ga_ref, gb_ref, x_ref,
                wa_ref, wb_ref, wo_ref, nw_ref, x1_ref, xn_ref, slab, *, dils, tm):
    def token_major(ref, dil):
        if dil == 1:
            return ref[0, 0]
        cols = []
        for hh in range(SWA_HEADS):
            for r in range(dil):
                slab[hh, pl.ds(r, tm // dil, stride=dil), :] = ref[0, r, :, hh * SWA_DH:(hh + 1) * SWA_DH]
            cols.append(slab[hh])
        return jnp.concatenate(cols, axis=1)

    os = [token_major(r, d) for r, d in zip((o0_ref, o1_ref, o2_ref), dils)]
    ls = [token_major(r, d) for r, d in zip((l0_ref, l1_ref, l2_ref), dils)]
    lmax = jnp.maximum(jnp.maximum(ls[0], ls[1]), ls[2])
    es = [jnp.exp(l - lmax) for l in ls]
    ob = (es[0] * os[0] + es[1] * os[1] + es[2] * os[2]) / (es[0] + es[1] + es[2])
    pa = jnp.dot(oa_ref[0].astype(BF16), wa_ref[...], preferred_element_type=F32)
    pb = jnp.dot(ob.astype(BF16), wb_ref[...], preferred_element_type=F32)
    merged = _sigmoid(ga_ref[0]) * pa + _sigmoid(gb_ref[0]) * pb
    x1 = x_ref[0] + jnp.dot(merged.astype(BF16), wo_ref[...], preferred_element_type=F32)
    x1_ref[0] = x1
    xn_ref[0] = x1 * lax.rsqrt(jnp.mean(x1 * x1, axis=-1, keepdims=True) + EPS) * nw_ref[...]


def _mix(o_a3, parts, dils, h3, x3, wa, wb, wo, norm_ffn, tm):
    bn, t, _ = x3.shape
    rowblk = lambda w, cb=0: pl.BlockSpec((1, tm, w), lambda b, i: (b, i, cb))
    full = lambda a: pl.BlockSpec(a.shape, lambda b, i: (0, 0))
    nw = norm_ffn.reshape(1, D_MODEL)
    part_specs, flat = [], []
    for (o, lse), dil in zip(parts, dils):
        spec = pl.BlockSpec((1, dil, tm // dil, SWA_W), lambda b, i: (b, 0, i, 0))
        part_specs += [spec, spec]
        flat += [o, lse]
    return pl.pallas_call(
        functools.partial(_mix_kernel, dils=dils, tm=tm),
        grid=(bn, t // tm),
        in_specs=[rowblk(GDN_VW)] + part_specs
                 + [rowblk(D_MODEL, COL_GA // D_MODEL), rowblk(D_MODEL, COL_GB // D_MODEL),
                    rowblk(D_MODEL), full(wa), full(wb), full(wo), full(nw)],
        out_specs=[rowblk(D_MODEL), rowblk(D_MODEL)],
        out_shape=[jax.ShapeDtypeStruct((bn, t, D_MODEL), F32)] * 2,
        scratch_shapes=[pltpu.VMEM((SWA_HEADS, tm, SWA_DH), F32)],
        compiler_params=_cparams("parallel", "parallel"),
        name="mix_out",
    )(o_a3, *flat, h3, h3, x3, wa, wb, wo, nw)


def _topk_rows(vals, k, payload=None):
    n, t = vals.shape
    idx = lax.broadcasted_iota(jnp.int32, (n, t), 0).astype(F32)
    out_v, out_i = [], []
    for _ in range(k):
        m = jnp.max(vals, axis=0, keepdims=True)
        am = jnp.min(jnp.where(vals == m, idx, float(n)), axis=0, keepdims=True)
        hit = idx == am
        out_v.append(m)
        out_i.append(am if payload is None else jnp.sum(jnp.where(hit, payload, 0.0), axis=0, keepdims=True))
        vals = jnp.where(hit, -jnp.inf, vals)
    return jnp.concatenate(out_v, axis=0), jnp.concatenate(out_i, axis=0)


def _route_kernel(q_ref, keys_ref, eid_ref, gate_ref):
    half = PEER_DKEY // 2
    eids, gates = [], []
    for h in range(PEER_HEADS):
        tops = []
        for p in range(2):
            qs = q_ref[:, (2 * h + p) * half:(2 * h + p + 1) * half].astype(BF16)
            st = _dot_nt(keys_ref[h, p].astype(BF16), qs)
            tops.append(_topk_rows(st, PEER_TOPK))
        (v1, i1), (v2, i2) = tops
        sub = SUBLANE
        assert PEER_TOPK == 2 * sub
        rows = [(slice(0, 1), slice(0, PEER_TOPK))]
        rows += [(slice(i, i + 1), slice(0, sub)) for i in range(1, sub)]
        rows += [(slice(sub, PEER_TOPK), slice(0, 1))]
        cand = jnp.concatenate([v1[a] + v2[b] for a, b in rows], axis=0)
        ecand = jnp.concatenate([i1[a] * float(PEER_NKEYS) + i2[b] for a, b in rows], axis=0)
        sc, e = _topk_rows(cand, PEER_TOPK, payload=ecand)
        ex = jnp.exp(sc - sc[0:1])
        gates.append(ex / jnp.sum(ex, axis=0, keepdims=True))
        eids.append(e)
    eid_ref[...] = jnp.concatenate(eids, axis=0).astype(jnp.int32).T
    gate_ref[...] = jnp.concatenate(gates, axis=0)


def _route(qp, sub_keys, tt):
    m = qp.shape[0]
    return pl.pallas_call(
        _route_kernel,
        grid=(m // tt,),
        in_specs=[pl.BlockSpec((tt, PEER_HEADS * PEER_DKEY), lambda i: (i, 0)),
                  pl.BlockSpec(sub_keys.shape, lambda i: (0, 0, 0, 0))],
        out_specs=[pl.BlockSpec((tt, PEER_SEL), lambda i: (i, 0)), pl.BlockSpec((PEER_SEL, tt), lambda i: (0, i))],
        out_shape=[jax.ShapeDtypeStruct((m, PEER_SEL), jnp.int32), jax.ShapeDtypeStruct((PEER_SEL, m), F32)],
        compiler_params=_cparams("parallel"),
        name="peer_route",
    )(qp, sub_keys)


PEER_TOKENS_PER_STEP = 512
PEER_NBUF = 8


PEER_CHUNKS = D_MODEL // LANE


def _pack_kernel(d_ref, u_ref, o_ref):
    hi = lax.bitcast_convert_type(d_ref[...].astype(BF16).astype(F32), jnp.uint32)
    lo = lax.bitcast_convert_type(u_ref[...].astype(BF16).astype(F32), jnp.uint32) >> 16
    word = hi | lo
    for c in range(PEER_CHUNKS):
        o_ref[:, c, :] = word[:, c * LANE:(c + 1) * LANE]


def _pack_peer_table(expert_down, expert_up, tr=256):
    e, d = expert_down.shape
    return pl.pallas_call(
        _pack_kernel,
        grid=(e // tr,),
        in_specs=[pl.BlockSpec((tr, d), lambda i: (i, 0))] * 2,
        out_specs=pl.BlockSpec((tr, PEER_CHUNKS, LANE), lambda i: (i, 0, 0)),
        out_shape=jax.ShapeDtypeStruct((e, PEER_CHUNKS, LANE), jnp.uint32),
        compiler_params=_cparams("parallel"),
        name="peer_pack",
    )(expert_down, expert_up)


def _peer_kernel(eid_ref, gate_ref, xn_ref, x1_ref, nw_ref, tbl_hbm, y_ref, *scratch, n_tok):
    nbuf = PEER_NBUF
    tbufs, (sem, xrow, yrow) = scratch[:nbuf], scratch[nbuf:]

    def issue(t, slot, rows):
        for j in rows:
            pltpu.make_async_copy(tbl_hbm.at[eid_ref[t, j]], tbufs[slot].at[:, j, :], sem.at[slot]).start(priority=j % 2)

    def wait(slot):
        pltpu.make_async_copy(tbufs[slot], tbufs[slot], sem.at[slot]).wait()

    lane = lax.broadcasted_iota(jnp.int32, (PEER_SEL, LANE), 1)
    hi_mask = jnp.uint32(0xFFFF0000)
    per_chunk = 3
    mid_rows = range(2 * PEER_CHUNKS * per_chunk, PEER_SEL)

    def compute(t, slot, t_ahead):
        def issue_rows(rows):
            if t_ahead is not None:
                issue(t_ahead, (slot - 1) % nbuf, rows)

        xrow[...] = xn_ref[pl.ds(t, 1), :]
        acc = jnp.zeros((PEER_SEL, LANE), F32)
        for c in range(PEER_CHUNKS):
            down = lax.bitcast_convert_type(tbufs[slot][c] & hi_mask, F32)
            acc = acc + down * xrow[:, c * LANE:(c + 1) * LANE]
            issue_rows(range(c * per_chunk, (c + 1) * per_chunk))
        act = jnp.sum(acc, axis=-1, keepdims=True)
        gate = jnp.sum(jnp.where(lane == t % LANE, gate_ref[t // LANE], 0.0), axis=-1, keepdims=True)
        issue_rows(mid_rows)
        gelu = 0.5 * act * (1.0 + lax.erf(act * (2.0 ** -0.5)))
        w = gate * gelu
        for c in range(PEER_CHUNKS):
            up = lax.bitcast_convert_type(tbufs[slot][c] << 16, F32)
            yrow[:, c * LANE:(c + 1) * LANE] = jnp.sum(up * w, axis=0, keepdims=True)
            issue_rows(range((PEER_CHUNKS + c) * per_chunk, (PEER_CHUNKS + c + 1) * per_chunk))
        y_ref[pl.ds(t, 1), :] = x1_ref[pl.ds(t, 1), :] + yrow[...]

    assert n_tok % nbuf == 0
    for k in range(nbuf - 1):
        issue(k, k, range(PEER_SEL))

    def group(p, carry):
        for k in range(nbuf):
            t = nbuf * p + k
            wait(k)
            compute(t, k, t + nbuf - 1)
        return carry

    lax.fori_loop(0, n_tok // nbuf - 1, group, 0)
    for k in range(nbuf):
        t = n_tok - nbuf + k
        wait(k)
        compute(t, k, t + nbuf - 1 if k == 0 else None)

    x2 = y_ref[...]
    y_ref[...] = x2 * lax.rsqrt(jnp.mean(x2 * x2, axis=-1, keepdims=True) + EPS) * nw_ref[...]


def _peer(eid, gates_t, xn2, x1, norm_final, table, n_tok):
    m = xn2.shape[0]
    kern = functools.partial(_peer_kernel, n_tok=n_tok)
    tiles = gates_t.shape[1] // LANE
    gates3 = gates_t.reshape(PEER_SEL, tiles, LANE).transpose(1, 0, 2)
    gblk = max(n_tok // LANE, 1)
    return pl.pallas_call(
        kern,
        grid=(m // n_tok,),
        in_specs=[pl.BlockSpec((gblk * LANE, PEER_SEL), lambda i: (i, 0), memory_space=pltpu.SMEM),
                  pl.BlockSpec((gblk, PEER_SEL, LANE), lambda i: (i, 0, 0)),
                  pl.BlockSpec((n_tok, D_MODEL), lambda i: (i, 0)),
                  pl.BlockSpec((n_tok, D_MODEL), lambda i: (i, 0)),
                  pl.BlockSpec((1, D_MODEL), lambda i: (0, 0)),
                  pl.BlockSpec(memory_space=pl.ANY)],
        out_specs=pl.BlockSpec((n_tok, D_MODEL), lambda i: (i, 0)),
        out_shape=jax.ShapeDtypeStruct((m, D_MODEL), F32),
        scratch_shapes=[pltpu.VMEM((PEER_CHUNKS, PEER_SEL, LANE), jnp.uint32)] * PEER_NBUF
                       + [pltpu.SemaphoreType.DMA((PEER_NBUF,)),
                          pltpu.VMEM((1, D_MODEL), F32), pltpu.VMEM((1, D_MODEL), F32)],
        compiler_params=_cparams("arbitrary"),
        name="peer_experts",
    )(eid, gates3, xn2, x1, norm_final.reshape(1, D_MODEL), table)


def _pick(m, candidates):
    for c in candidates:
        if m % c == 0:
            return c
    return m


def _layer(x, pos, tail8, s0, caches, w, chunk):
    bn, t, _ = x.shape
    m = bn * t
    x2d = x.reshape(m, D_MODEL)
    tm_big = _pick(m, (1024, 512, 256, 128, 64, 32))

    xn = _rmsnorm(x2d, w["norm_mix"], _pick(m, (512, 256, 128, 64, 32)), BF16)
    h_main = _matmul(xn, w["w_main"], tm_big, 1280, "proj_in")
    h_ab = _matmul(xn, w["w_ab"], tm_big, LANE, "proj_ab")

    tp = -(-t // chunk) * chunk
    h3 = h_main.reshape(bn, t, N_MAIN)
    ab3 = h_ab.reshape(bn, t, LANE)
    h3p = h3 if tp == t else jnp.pad(h3, ((0, 0), (0, tp - t), (0, 0)))
    ab3p = ab3 if tp == t else jnp.pad(ab3, ((0, 0), (0, tp - t), (0, 0)))
    o_a, s_new = _gdn(h3p, ab3p, w["conv_w"], tail8, s0, w["a_log"], w["dt_bias"], w["gdn_norm"], chunk, t)
    o_a = o_a[:, :t]

    cosf, sinf = _rope_tables(pos)
    kv_new = []
    if caches is None:
        qkv = _rope_split(h3, cosf, sinf, 256)
        parts = [_swa_prompt_group(q, k, v, gi, 2 if q.shape[2] % (2 * SWA_WIN_BLK) == 0 else 1)
                 for gi, (q, k, v) in enumerate(qkv)]
        dils = tuple(d for _, d in SWA_GROUPS)
        for (win, dil), (_, k, v) in zip(SWA_GROUPS, qkv):
            keep = min(win, t)
            tok = lambda a: (a[:, :, (t - keep) // dil:].transpose(0, 2, 1, 3)
                             .reshape(bn, keep, SWA_HEADS, SWA_DH))
            kv_new.append(jnp.stack([tok(k), tok(v)], axis=2))
        mix_shape = (bn, t)
    else:
        cosf, sinf = jnp.tile(cosf, (bn, 1)), jnp.tile(sinf, (bn, 1))
        q_r, k_r = _rope(h_main, cosf, sinf, _pick(m, (512, 256, 128, 64, 32)))
        v_new = h_main[:, COL_SV:COL_SV + SWA_QKV]
        parts = [(o.reshape(1, 1, m, SWA_W), lse.reshape(1, 1, m, SWA_W))
                 for o, lse in _swa_sample(q_r, k_r, v_new, caches)]
        dils = (1,) * N_SWA
        for gi in range(N_SWA):
            kk = k_r[:, gi * SWA_W:(gi + 1) * SWA_W].reshape(bn, t, SWA_HEADS, SWA_DH)
            vv = v_new[:, gi * SWA_W:(gi + 1) * SWA_W].reshape(bn, t, SWA_HEADS, SWA_DH)
            kv_new.append(jnp.stack([kk, vv], axis=2))
        mix_shape = (1, m)

    mb, mt = mix_shape
    x1, xn2 = _mix(o_a.reshape(mb, mt, GDN_VW), parts, dils, h_main.reshape(mb, mt, N_MAIN),
                   x.reshape(mb, mt, D_MODEL), w["w_br_a"], w["w_br_b"], w["w_out"], w["norm_ffn"],
                   _pick(mt, (256, 128, 64, 32)))
    x1, xn2 = x1.reshape(m, D_MODEL), xn2.reshape(m, D_MODEL)

    mp = -(-m // LANE) * LANE
    xq = xn2 if mp == m else jnp.pad(xn2, ((0, mp - m), (0, 0)))
    qp = _matmul(xq, w["w_query"], _pick(mp, (1024, 512, 256, 128)), 512, "peer_query")
    eid, gates_t = _route(qp, w["sub_keys"], LANE)
    y = _peer(eid, gates_t, xn2, x1, w["norm_final"], w["peer_table"], min(m, PEER_TOKENS_PER_STEP))
    return y.reshape(bn, t, D_MODEL), s_new, h3, kv_new


def kernel(x_prompt, x_sample, state_gdn, state_conv, cache_kv_w128, cache_kv_w512, cache_kv_w2048,
           norm_mix, w_in, conv_w, a_log, dt_bias, gdn_norm, w_br_a, w_br_b, w_out, norm_ffn,
           w_query, sub_keys, expert_down, expert_up, norm_final):
    w_main = jnp.concatenate([w_in[:, IN_GA:], w_in[:, :IN_AB], w_in[:, IN_SQ:IN_GA]], axis=1).astype(BF16)
    w_ab = jnp.pad(w_in[:, IN_AB:IN_SQ], ((0, 0), (0, LANE - 2 * GDN_HEADS))).astype(BF16)
    w = dict(norm_mix=norm_mix, w_main=w_main, w_ab=w_ab, conv_w=conv_w, a_log=a_log, dt_bias=dt_bias,
             gdn_norm=gdn_norm, w_br_a=w_br_a.astype(BF16), w_br_b=w_br_b.astype(BF16), w_out=w_out.astype(BF16),
             norm_ffn=norm_ffn, w_query=w_query.astype(BF16), sub_keys=sub_keys,
             peer_table=_pack_peer_table(expert_down, expert_up), norm_final=norm_final)

    bp, tlen = x_prompt.shape[:2]
    bs, ts = x_sample.shape[:2]
    caches = (cache_kv_w128, cache_kv_w512, cache_kv_w2048)

    tail8 = jnp.concatenate([jnp.zeros((bs, SUBLANE - (CONV_W - 1), GDN_CONV_CH), F32), state_conv], axis=1)
    y_s, gdn_s, h3_s, kv_new = _layer(
        x_sample, PAST_LEN + jnp.arange(ts, dtype=jnp.int32), tail8, state_gdn, caches, w, SUBLANE)
    conv_s = jnp.concatenate([state_conv, h3_s[:, :, COL_Q:COL_Q + GDN_CONV_CH]], axis=1)[:, ts:]
    kv_s = [jnp.concatenate([cache, new.astype(cache.dtype)], axis=1)[:, ts:] for cache, new in zip(caches, kv_new)]

    y_p, gdn_p, h3_p, kv_p = _layer(
        x_prompt, jnp.arange(tlen, dtype=jnp.int32),
        jnp.zeros((bp, SUBLANE, GDN_CONV_CH), F32), jnp.zeros((bp, GDN_HEADS, GDN_DK, GDN_DV), F32),
        None, w, CHUNK)
    conv_p = h3_p[:, tlen - (CONV_W - 1):, COL_Q:COL_Q + GDN_CONV_CH]

    return (y_p, y_s, gdn_p.astype(x_prompt.dtype), conv_p, kv_p[0], kv_p[1], kv_p[2],
            gdn_s.astype(state_gdn.dtype), conv_s, kv_s[0], kv_s[1], kv_s[2])
```
